```python
import math
import jax, jax.numpy as jnp
from jax import lax
import numpy as np

D_MODEL = 1024
BATCH = 2
SEQ = 8192
DEPTH = 1

N_HEADS = 8
HEAD_DIM = 64
V_DIM = 2 * HEAD_DIM
QK_WIDTH = N_HEADS * 2 * HEAD_DIM
ATTN_WIDTH = N_HEADS * V_DIM
Q_BLOCK = 128
SUBLN_EPS = 1e-5
LRU_WIDTH = D_MODEL
LRU_BLOCKS = 16
LRU_BLOCK_W = LRU_WIDTH // LRU_BLOCKS
CONV_WIDTH = 4
LRU_C = 8.0
N_GROUPS = 4
EXPERTS_PER_GROUP = 8
N_EXPERTS = N_GROUPS * EXPERTS_PER_GROUP
TOP_K_INNER = 2
EXPERT_FF = 512
ROUTER_BIAS_SCALE = 0.01
PLE_DIM = 256
NORM_EPS = 1e-6

IN_SIZES = (QK_WIDTH, QK_WIDTH, ATTN_WIDTH, LRU_WIDTH, LRU_WIDTH, D_MODEL, D_MODEL)
IN_COLS = sum(IN_SIZES)
IN_SPLITS = tuple(int(s) for s in np.cumsum(IN_SIZES)[:-1])

kernel_name = "hybrid_diffattn_rglru_hiermoe_ple"


def rmsnorm(x, g, eps=NORM_EPS):
    xf = x.astype(jnp.float32)
    y = xf * lax.rsqrt(jnp.mean(xf * xf, axis=-1, keepdims=True) + eps)
    return (y * g.astype(jnp.float32)).astype(x.dtype)


def diff_attention(q, k, v, lam):
    B, S = q.shape[0], q.shape[1]
    n_blk = S // Q_BLOCK
    scale = HEAD_DIM ** -0.5
    qb = jnp.moveaxis(q.reshape(B, n_blk, Q_BLOCK, N_HEADS, 2, HEAD_DIM), 1, 0)
    kf = k.astype(jnp.float32)
    vf = v.astype(jnp.float32)
    kpos = jnp.arange(S)
    neg = jnp.finfo(jnp.float32).min

    def one_block(args):
        qi, blk = args
        s = jnp.einsum('bqhcd,bkhcd->bhcqk', qi.astype(jnp.float32), kf) * scale
        qpos = blk * Q_BLOCK + jnp.arange(Q_BLOCK)
        causal = kpos[None, :] <= qpos[:, None]
        s = jnp.where(causal, s, neg)
        pr = jax.nn.softmax(s, axis=-1)
        a = pr[:, :, 0] - lam * pr[:, :, 1]
        return jnp.einsum('bhqk,bkhe->bqhe', a, vf)

    o = lax.map(one_block, (qb, jnp.arange(n_blk)))
    return jnp.moveaxis(o, 0, 1).reshape(B, S, N_HEADS, V_DIM).astype(q.dtype)


def causal_depthwise_conv(x, w, b):
    S = x.shape[1]
    xp = jnp.pad(x, ((0, 0), (CONV_WIDTH - 1, 0), (0, 0)))
    y = b
    for j in range(CONV_WIDTH):
        y = y + xp[:, j:j + S] * w[j]
    return y


def rg_lru(x, w_rg, b_rg, w_ig, b_ig, lru_lambda):
    B, S, W = x.shape
    xb = x.reshape(B, S, LRU_BLOCKS, LRU_BLOCK_W)
    r = jax.nn.sigmoid(jnp.einsum('bsnw,nwv->bsnv', xb, w_rg) + b_rg).reshape(B, S, W)
    ig = jax.nn.sigmoid(jnp.einsum('bsnw,nwv->bsnv', xb, w_ig) + b_ig).reshape(B, S, W)
    log_a = -LRU_C * r.astype(jnp.float32) * jax.nn.softplus(-lru_lambda.astype(jnp.float32))
    a = jnp.exp(log_a)
    mult = jnp.sqrt(-jnp.expm1(2.0 * log_a))
    u = mult * (ig * x).astype(jnp.float32)

    def combine(left, right):
        a_l, b_l = left
        a_r, b_r = right
        return a_l * a_r, a_r * b_l + b_r

    _, h = lax.associative_scan(combine, (a, u), axis=1)
    return h.astype(x.dtype)


def hier_moe(h, w_rt_group, b_rt_group, w_rt_expert, b_rt_expert, w_e_gate, w_e_up, w_e_down):
    B, S, D = h.shape
    t = h.reshape(B * S, D)
    grp_logits = (t @ w_rt_group + b_rt_group).astype(jnp.float32)
    grp_prob = jax.nn.softmax(grp_logits, axis=-1)
    g_idx = jnp.argmax(grp_logits, axis=-1)
    g_w = jnp.take_along_axis(grp_prob, g_idx[:, None], axis=1)[:, 0]
    fine = (jnp.einsum('nd,gde->nge', t, w_rt_expert) + b_rt_expert).astype(jnp.float32)
    fine_sel = jnp.take_along_axis(fine, g_idx[:, None, None], axis=1)[:, 0]
    top_logit, top_idx = lax.top_k(fine_sel, TOP_K_INNER)
    top_w = jax.nn.softmax(top_logit, axis=-1)
    expert_id = g_idx[:, None] * EXPERTS_PER_GROUP + top_idx
    weights = g_w[:, None] * top_w
    combine = jnp.sum(jax.nn.one_hot(expert_id, N_EXPERTS, dtype=jnp.float32)
                      * weights[..., None], axis=1)

    def expert_step(acc, params):
        wg, wu, wd, c = params
        hid = jax.nn.silu(t @ wg) * (t @ wu)
        return acc + c[:, None] * (hid @ wd).astype(jnp.float32), None

    acc0 = jnp.zeros((B * S, D), jnp.float32)
    y, _ = lax.scan(expert_step, acc0, (w_e_gate, w_e_up, w_e_down, combine.T))
    return y.reshape(B, S, D).astype(h.dtype)


def setup_inputs(seed: int = 0) -> dict:
    key = jax.random.key(seed)
    ks = jax.random.split(key, 32)
    f32 = jnp.float32
    L, D = DEPTH, D_MODEL

    def nrm(k, shape, fan_in):
        return jax.random.normal(k, shape, f32) * (fan_in ** -0.5)

    def gain(k, shape):
        return 1.0 + 0.05 * jax.random.normal(k, shape, f32)

    u = jax.random.uniform(ks[16], (L, LRU_WIDTH), f32, 0.9, 0.999)
    s = u ** (1.0 / LRU_C)
    lru_lambda = jnp.log(s) - jnp.log1p(-s)
    return {
        "x": jax.random.normal(ks[0], (BATCH, SEQ, D), f32),
        "p": jax.random.normal(ks[1], (L, BATCH, SEQ, PLE_DIM), f32),
        "g_mix": gain(ks[2], (L, D)),
        "w_in": nrm(ks[3], (L, D, IN_COLS), D),
        "lam_q1": 0.1 * jax.random.normal(ks[4], (L, HEAD_DIM), f32),
        "lam_k1": 0.1 * jax.random.normal(ks[5], (L, HEAD_DIM), f32),
        "lam_q2": 0.1 * jax.random.normal(ks[6], (L, HEAD_DIM), f32),
        "lam_k2": 0.1 * jax.random.normal(ks[7], (L, HEAD_DIM), f32),
        "g_subln": gain(ks[8], (L, V_DIM)),
        "w_conv": nrm(ks[9], (L, CONV_WIDTH, LRU_WIDTH), CONV_WIDTH),
        "b_conv": 0.02 * jax.random.normal(ks[10], (L, LRU_WIDTH), f32),
        "w_rg": nrm(ks[11], (L, LRU_BLOCKS, LRU_BLOCK_W, LRU_BLOCK_W), LRU_BLOCK_W),
        "b_rg": 0.02 * jax.random.normal(ks[12], (L, LRU_BLOCKS, LRU_BLOCK_W), f32),
        "w_ig": nrm(ks[13], (L, LRU_BLOCKS, LRU_BLOCK_W, LRU_BLOCK_W), LRU_BLOCK_W),
        "b_ig": 0.02 * jax.random.normal(ks[14], (L, LRU_BLOCKS, LRU_BLOCK_W), f32),
        "lru_lambda": lru_lambda,
        "w_attn_br": nrm(ks[17], (L, ATTN_WIDTH, D), ATTN_WIDTH),
        "w_lru_br": nrm(ks[18], (L, LRU_WIDTH, D), LRU_WIDTH),
        "w_out": nrm(ks[19], (L, D, D), D),
        "g_moe": gain(ks[20], (L, D)),
        "w_rt_group": nrm(ks[21], (L, D, N_GROUPS), D),
        "b_rt_group": ROUTER_BIAS_SCALE * jax.random.normal(ks[22], (L, N_GROUPS), f32),
        "w_rt_expert": nrm(ks[23], (L, N_GROUPS, D, EXPERTS_PER_GROUP), D),
        "b_rt_expert": ROUTER_BIAS_SCALE * jax.random.normal(ks[24], (L, N_GROUPS, EXPERTS_PER_GROUP), f32),
        "w_e_gate": nrm(ks[25], (L, N_EXPERTS, D, EXPERT_FF), D),
        "w_e_up": nrm(ks[26], (L, N_EXPERTS, D, EXPERT_FF), D),
        "w_e_down": nrm(ks[27], (L, N_EXPERTS, EXPERT_FF, D), EXPERT_FF),
        "g_ple": gain(ks[28], (L, D)),
        "w_ple_gate": nrm(ks[29], (L, D, D), D),
        "w_ple_proj": nrm(ks[30], (L, PLE_DIM, D), PLE_DIM),
        "g_final": gain(ks[31], (D,)),
    }


def reference(x, p, g_mix, w_in, lam_q1, lam_k1, lam_q2, lam_k2, g_subln, w_conv, b_conv,
              w_rg, b_rg, w_ig, b_ig, lru_lambda, w_attn_br, w_lru_br, w_out, g_moe,
              w_rt_group, b_rt_group, w_rt_expert, b_rt_expert, w_e_gate, w_e_up, w_e_down,
              g_ple, w_ple_gate, w_ple_proj, g_final):
    B, S = x.shape[0], x.shape[1]
    for i in range(DEPTH):
        lam_init = 0.8 - 0.6 * math.exp(-0.3 * i)
        h = rmsnorm(x, g_mix[i])
        z = h @ w_in[i]
        q, k, v, xr, gr, ga, gb = jnp.split(z, IN_SPLITS, axis=-1)
        q = q.reshape(B, S, N_HEADS, 2, HEAD_DIM)
        k = k.reshape(B, S, N_HEADS, 2, HEAD_DIM)
        v = v.reshape(B, S, N_HEADS, V_DIM)
        lam = (jnp.exp(jnp.sum(lam_q1[i].astype(jnp.float32) * lam_k1[i].astype(jnp.float32)))
               - jnp.exp(jnp.sum(lam_q2[i].astype(jnp.float32) * lam_k2[i].astype(jnp.float32)))
               + lam_init)
        o = diff_attention(q, k, v, lam)
        o = rmsnorm(o, g_subln[i], SUBLN_EPS) * (1.0 - lam_init)
        attn_branch = o.reshape(B, S, ATTN_WIDTH) @ w_attn_br[i]
        xc = causal_depthwise_conv(xr, w_conv[i], b_conv[i])
        hr = rg_lru(xc, w_rg[i], b_rg[i], w_ig[i], b_ig[i], lru_lambda[i])
        lru_branch = (hr * jax.nn.gelu(gr)) @ w_lru_br[i]
        mixed = jax.nn.sigmoid(ga) * attn_branch + jax.nn.sigmoid(gb) * lru_branch
        x = x + mixed @ w_out[i]
        x = x + hier_moe(rmsnorm(x, g_moe[i]), w_rt_group[i], b_rt_group[i], w_rt_expert[i],
                         b_rt_expert[i], w_e_gate[i], w_e_up[i], w_e_down[i])
        gate = jax.nn.sigmoid(rmsnorm(x, g_ple[i]) @ w_ple_gate[i])
        x = x + gate * (p[i] @ w_ple_proj[i])
    return rmsnorm(x, g_final)
```

```python
import functools
import math

import jax
import jax.numpy as jnp
from jax import lax
from jax.experimental import pallas as pl
from jax.experimental.pallas import tpu as pltpu

F32 = jnp.float32
BF16 = jnp.bfloat16
I32 = jnp.int32

N_HEADS = 8
HEAD_DIM = 64
V_DIM = 2 * HEAD_DIM
SUBLN_EPS = 1e-5
NORM_EPS = 1e-6
LRU_BLOCKS = 16
CONV_WIDTH = 4
LRU_C = 8.0
N_GROUPS = 4
EXPERTS_PER_GROUP = 8
N_EXPERTS = N_GROUPS * EXPERTS_PER_GROUP
N_IN_PARTS = 7
Q_SCALE = HEAD_DIM ** -0.5

LANES = 128
SUBLANES = 8
MXU_DIM = 256

TM_INPROJ = 256
TQ_ATTN = 256
TK_ATTN = 256
T_LRU = 256
TM_MERGE = 256
T_ROUTE = 256
TM_MOE = 256
MASK_NEG = -1e30


def _rmsnorm(x, g, eps):
    return (x * lax.rsqrt(jnp.mean(x * x, axis=-1, keepdims=True) + eps)) * g


def _inproj_kernel(x_ref, g_ref, w_ref, *out_refs, d):
    h = _rmsnorm(x_ref[...], g_ref[...], NORM_EPS).astype(BF16)
    for c, o_ref in enumerate(out_refs):
        z = jnp.dot(h, w_ref[:, c * d:(c + 1) * d], preferred_element_type=F32)
        if c == 0:
            z = z * Q_SCALE
        o_ref[...] = z.astype(o_ref.dtype)


def _inproj(x2, g_mix, w_in_bf):
    n, d = x2.shape
    tm = TM_INPROJ
    out_dtypes = (BF16, BF16, BF16, F32, F32, F32, F32)
    row_spec = pl.BlockSpec((tm, d), lambda i: (i, 0))
    return pl.pallas_call(
        functools.partial(_inproj_kernel, d=d),
        grid=(n // tm,),
        in_specs=[
            row_spec,
            pl.BlockSpec((1, d), lambda i: (0, 0)),
            pl.BlockSpec((d, N_IN_PARTS * d), lambda i: (0, 0), pipeline_mode=pl.Buffered(1)),
        ],
        out_specs=[row_spec] * N_IN_PARTS,
        out_shape=[jax.ShapeDtypeStruct((n, d), dt) for dt in out_dtypes],
        compiler_params=pltpu.CompilerParams(dimension_semantics=("arbitrary",)),
        name="inproj",
    )(x2, g_mix, w_in_bf)


def _attn_kernel(lq1_ref, lk1_ref, lq2_ref, lk2_ref, g_ref, q_ref, k_ref, v_ref, o_ref, *, tq, tk, lam_init):
    i = pl.program_id(2)
    q = q_ref[...]
    lane = lax.broadcasted_iota(I32, q.shape, 1)
    zero = jnp.zeros_like(q)
    qq = jnp.concatenate([jnp.where(lane < HEAD_DIM, q, zero), jnp.where(lane >= HEAD_DIM, q, zero)], axis=0)

    def step(j, carry, masked):
        m, l, acc = carry
        start = pl.multiple_of(j * tk, tk)
        kj = k_ref[pl.ds(start, tk), :]
        vj = v_ref[pl.ds(start, tk), :]
        s = lax.dot_general(qq, kj, (((1,), (1,)), ((), ())), preferred_element_type=F32)
        if masked:
            row = lax.broadcasted_iota(I32, s.shape, 0)
            col = lax.broadcasted_iota(I32, s.shape, 1)
            s = jnp.where(col <= (row & (tq - 1)), s, MASK_NEG)
        m_new = jnp.maximum(m, jnp.max(s, axis=1, keepdims=True))
        alpha = jnp.exp(m - m_new)
        p = jnp.exp(s - m_new)
        l = alpha * l + jnp.sum(p, axis=1, keepdims=True)
        acc = alpha * acc + jnp.dot(p.astype(BF16), vj, preferred_element_type=F32)
        return m_new, l, acc

    m0 = jnp.full((2 * tq, 1), MASK_NEG, F32)
    l0 = jnp.zeros((2 * tq, 1), F32)
    a0 = jnp.zeros((2 * tq, V_DIM), F32)
    carry = lax.fori_loop(0, i, functools.partial(step, masked=False), (m0, l0, a0))
    _, l, acc = step(i, carry, masked=True)

    o_all = acc / l
    lam = (jnp.exp(jnp.sum(lq1_ref[...] * lk1_ref[...], axis=1, keepdims=True))
           - jnp.exp(jnp.sum(lq2_ref[...] * lk2_ref[...], axis=1, keepdims=True)) + lam_init)
    o = o_all[:tq] - lam * o_all[tq:]
    o_ref[...] = (_rmsnorm(o, g_ref[...], SUBLN_EPS) * (1.0 - lam_init)).astype(o_ref.dtype)


def _attention(q, k, v, lam_q1, lam_k1, lam_q2, lam_k2, g_subln, lam_init):
    b, s, d = q.shape
    tq, tk = TQ_ATTN, TK_ATTN
    assert tq == tk and s % tq == 0 and d == N_HEADS * V_DIM
    small = lambda w: pl.BlockSpec((1, w), lambda bi, hi, qi: (0, 0))
    return pl.pallas_call(
        functools.partial(_attn_kernel, tq=tq, tk=tk, lam_init=lam_init),
        grid=(b, N_HEADS, s // tq),
        in_specs=[
            small(HEAD_DIM), small(HEAD_DIM), small(HEAD_DIM), small(HEAD_DIM), small(V_DIM),
            pl.BlockSpec((None, tq, V_DIM), lambda bi, hi, qi: (bi, qi, hi)),
            pl.BlockSpec((None, s, V_DIM), lambda bi, hi, qi: (bi, 0, hi)),
            pl.BlockSpec((None, s, V_DIM), lambda bi, hi, qi: (bi, 0, hi)),
        ],
        out_specs=pl.BlockSpec((None, tq, V_DIM), lambda bi, hi, qi: (bi, qi, hi)),
        out_shape=jax.ShapeDtypeStruct((b, s, d), BF16),
        compiler_params=pltpu.CompilerParams(dimension_semantics=("arbitrary", "arbitrary", "arbitrary")),
        name="attn",
    )(lam_q1, lam_k1, lam_q2, lam_k2, g_subln, q, k, v)


def _lru_kernel(xr_ref, gr_ref, wc_ref, bc_ref, wrg_ref, brg_ref, wig_ref, big_ref, lam_ref, o_ref,
                xprev_ref, hcar_ref, *, t_tile):
    @pl.when(pl.program_id(1) == 0)
    def _():
        xprev_ref[...] = jnp.zeros_like(xprev_ref)
        hcar_ref[...] = jnp.zeros_like(hcar_ref)

    x = xr_ref[...]
    w = x.shape[1]
    prev = xprev_ref[...]
    wc = wc_ref[...]
    row8 = lax.broadcasted_iota(I32, (SUBLANES, w), 0)
    xc = bc_ref[...]
    for j in range(CONV_WIDTH):
        shift = CONV_WIDTH - 1 - j
        if shift == 0:
            xs = x
        else:
            rolled = pltpu.roll(x, shift, 0)
            head = jnp.where(row8 < shift, pltpu.roll(prev, shift, 0), rolled[:SUBLANES])
            xs = jnp.concatenate([head, rolled[SUBLANES:]], axis=0)
        xc = xc + xs * wc[j:j + 1]
    xprev_ref[...] = x[t_tile - SUBLANES:]

    xcb = xc.astype(BF16)
    n_grp = w // MXU_DIM
    pre_r = jnp.concatenate(
        [jnp.dot(xcb[:, g * MXU_DIM:(g + 1) * MXU_DIM], wrg_ref[g], preferred_element_type=F32) for g in range(n_grp)],
        axis=1) + brg_ref[...]
    pre_i = jnp.concatenate(
        [jnp.dot(xcb[:, g * MXU_DIM:(g + 1) * MXU_DIM], wig_ref[g], preferred_element_type=F32) for g in range(n_grp)],
        axis=1) + big_ref[...]
    r = jax.nn.sigmoid(pre_r)
    ig = jax.nn.sigmoid(pre_i)
    neg_lam = -lam_ref[...]
    softplus = jnp.maximum(neg_lam, 0.0) + jnp.log1p(jnp.exp(-jnp.abs(neg_lam)))
    log_a = (-LRU_C) * r * softplus
    a = jnp.exp(log_a)
    th = jnp.tanh(log_a)
    u = jnp.sqrt((-2.0 * th) / (1.0 - th)) * (ig * xc)

    rows = lax.broadcasted_iota(I32, x.shape, 0)
    dist = 1
    while dist < t_tile:
        keep = rows >= dist
        u = jnp.where(keep, a * pltpu.roll(u, dist, 0) + u, u)
        a = jnp.where(keep, a * pltpu.roll(a, dist, 0), a)
        dist *= 2
    h = a * hcar_ref[...] + u
    hcar_ref[...] = h[t_tile - 1:]
    o_ref[...] = (h * jax.nn.gelu(gr_ref[...])).astype(o_ref.dtype)


def _block_diag_groups(wblk):
    nb, bw, _ = wblk.shape
    per = MXU_DIM // bw
    g = nb // per
    eye = jnp.eye(per, dtype=wblk.dtype)
    w5 = wblk.reshape(g, per, bw, bw)
    return jnp.einsum("gawv,ab->gawbv", w5, eye).reshape(g, MXU_DIM, MXU_DIM).astype(BF16)


def _lru(xr, gr, w_conv, b_conv, w_rg, b_rg, w_ig, b_ig, lru_lambda, b, s):
    n, w = xr.shape
    t = T_LRU
    n_t = s // t
    row_spec = pl.BlockSpec((t, w), lambda bi, ti: (bi * n_t + ti, 0))
    vec = lambda r: pl.BlockSpec((r, w), lambda bi, ti: (0, 0))
    blk = pl.BlockSpec((w // MXU_DIM, MXU_DIM, MXU_DIM), lambda bi, ti: (0, 0, 0))
    return pl.pallas_call(
        functools.partial(_lru_kernel, t_tile=t),
        grid=(b, n_t),
        in_specs=[row_spec, row_spec, vec(CONV_WIDTH), vec(1), blk, vec(1), blk, vec(1), vec(1)],
        out_specs=row_spec,
        out_shape=jax.ShapeDtypeStruct((n, w), BF16),
        scratch_shapes=[pltpu.VMEM((SUBLANES, w), F32), pltpu.VMEM((1, w), F32)],
        compiler_params=pltpu.CompilerParams(dimension_semantics=("arbitrary", "arbitrary")),
        name="lru",
    )(xr, gr, w_conv, b_conv.reshape(1, w), _block_diag_groups(w_rg), b_rg.reshape(1, w),
      _block_diag_groups(w_ig), b_ig.reshape(1, w), lru_lambda.reshape(1, w))


def _merge_kernel(o_ref, hg_ref, ga_ref, gb_ref, x_ref, wa_ref, wl_ref, wo_ref, gm_ref, wr_ref, br_ref,
                  x1_ref, h2_ref, lg_ref):
    attn_br = jnp.dot(o_ref[...], wa_ref[...], preferred_element_type=F32)
    lru_br = jnp.dot(hg_ref[...], wl_ref[...], preferred_element_type=F32)
    mixed = jax.nn.sigmoid(ga_ref[...]) * attn_br + jax.nn.sigmoid(gb_ref[...]) * lru_br
    x1 = x_ref[...] + jnp.dot(mixed.astype(BF16), wo_ref[...], preferred_element_type=F32)
    x1_ref[...] = x1
    h2 = _rmsnorm(x1, gm_ref[...], NORM_EPS)
    h2_ref[...] = h2
    lg_ref[...] = jnp.dot(h2, wr_ref[...], preferred_element_type=F32,
                          precision=lax.Precision.HIGHEST) + br_ref[...]


def _merge(o, hg, ga, gb, x2, wa, wl, wo, g_moe, w_router, b_router):
    n, d = x2.shape
    tm = TM_MERGE
    row = pl.BlockSpec((tm, d), lambda i: (i, 0))
    full = lambda r, c: pl.BlockSpec((r, c), lambda i: (0, 0))
    return pl.pallas_call(
        _merge_kernel,
        grid=(n // tm,),
        in_specs=[row, row, row, row, row, full(d, d), full(d, d), full(d, d), full(1, d),
                  full(d, LANES), full(1, LANES)],
        out_specs=[row, row, pl.BlockSpec((tm, LANES), lambda i: (i, 0))],
        out_shape=[jax.ShapeDtypeStruct((n, d), F32), jax.ShapeDtypeStruct((n, d), F32),
                   jax.ShapeDtypeStruct((n, LANES), F32)],
        compiler_params=pltpu.CompilerParams(dimension_semantics=("arbitrary",)),
        name="merge",
    )(o, hg, ga, gb, x2, wa, wl, wo, g_moe, w_router, b_router)


def _lane_pick(x, lane, idx):
    return jnp.sum(jnp.where(lane == idx, x, jnp.zeros_like(x)), axis=1, keepdims=True)


def _route_kernel(lg_ref, pos_ref, wts_ref, offs_ref, *, n, t_tile):
    shape = (t_tile, LANES)
    lane = lax.broadcasted_iota(I32, shape, 1)
    neg_inf = jnp.float32(-jnp.inf)
    tri = (lax.broadcasted_iota(I32, (t_tile, t_tile), 0) > lax.broadcasted_iota(I32, (t_tile, t_tile), 1)).astype(BF16)

    def phase1(t, cnt):
        rows = pl.ds(pl.multiple_of(t * t_tile, t_tile), t_tile)
        lg = lg_ref[rows, :]
        is_grp = lane < N_GROUPS
        gl = jnp.where(is_grp, lg, neg_inf)
        gmax = jnp.max(gl, axis=1, keepdims=True)
        g_idx = jnp.min(jnp.where(gl == gmax, lane, LANES), axis=1, keepdims=True)
        g_w = 1.0 / jnp.sum(jnp.where(is_grp, jnp.exp(lg - gmax), 0.0), axis=1, keepdims=True)
        lo = N_GROUPS + EXPERTS_PER_GROUP * g_idx
        in_grp = (lane >= lo) & (lane < lo + EXPERTS_PER_GROUP)
        fl = jnp.where(in_grp, lg, neg_inf)
        v1 = jnp.max(fl, axis=1, keepdims=True)
        i1 = jnp.min(jnp.where(in_grp & (fl == v1), lane, LANES), axis=1, keepdims=True)
        rest = in_grp & (lane != i1)
        fl2 = jnp.where(rest, lg, neg_inf)
        v2 = jnp.max(fl2, axis=1, keepdims=True)
        i2 = jnp.min(jnp.where(rest & (fl2 == v2), lane, LANES), axis=1, keepdims=True)
        t2 = jnp.exp(v2 - v1)
        den = 1.0 + t2
        w1 = g_w * (1.0 / den)
        w2 = g_w * (t2 / den)
        e1 = i1 - N_GROUPS
        e2 = i2 - N_GROUPS
        onehot = ((lane == e1) | (lane == e2)).astype(F32)
        before = jnp.dot(tri, onehot.astype(BF16), preferred_element_type=F32) + cnt
        rank1 = _lane_pick(before, lane, e1)
        rank2 = _lane_pick(before, lane, e2)
        pos_ref[rows, :] = jnp.where(lane == 0, e1, jnp.where(lane == 1, e2, jnp.where(
            lane == 2, rank1.astype(I32), jnp.where(lane == 3, rank2.astype(I32), 0))))
        wts_ref[rows, :] = jnp.where(lane == 0, w1, jnp.where(lane == 1, w2, 0.0))
        return cnt + jnp.sum(onehot, axis=0, keepdims=True)

    cnt = lax.fori_loop(0, n // t_tile, phase1, jnp.zeros((1, LANES), F32))

    lane8 = lax.broadcasted_iota(I32, (SUBLANES, LANES), 1)
    incl = jnp.broadcast_to(cnt, (SUBLANES, LANES))
    dist = 1
    while dist < LANES:
        incl = incl + jnp.where(lane8 >= dist, pltpu.roll(incl, dist, 1), 0.0)
        dist *= 2
    offs = incl - cnt
    offs_ref[...] = offs.astype(I32)
    offs_row = offs[0:1]

    def phase2(t, carry):
        rows = pl.ds(pl.multiple_of(t * t_tile, t_tile), t_tile)
        info = pos_ref[rows, :]
        e1 = _lane_pick(info, lane, 0)
        e2 = _lane_pick(info, lane, 1)
        r1 = _lane_pick(info, lane, 2)
        r2 = _lane_pick(info, lane, 3)
        offs_b = jnp.broadcast_to(offs_row, shape).astype(I32)
        p1 = r1 + _lane_pick(offs_b, lane, e1)
        p2 = r2 + _lane_pick(offs_b, lane, e2)
        pos_ref[rows, :] = jnp.where(lane == 0, p1, jnp.where(lane == 1, p2, 0))
        return carry

    lax.fori_loop(0, n // t_tile, phase2, 0)


def _route(logits):
    n = logits.shape[0]
    return pl.pallas_call(
        functools.partial(_route_kernel, n=n, t_tile=T_ROUTE),
        out_shape=[jax.ShapeDtypeStruct((n, LANES), I32), jax.ShapeDtypeStruct((n, LANES), F32),
                   jax.ShapeDtypeStruct((SUBLANES, LANES), I32)],
        name="route",
    )(logits)


def _dispatch_kernel(p1_ref, p2_ref, h_ref, hs_ref, sem, *, tm):
    def issue(r, carry):
        src = h_ref.at[pl.ds(r, 1)]
        pltpu.make_async_copy(src, hs_ref.at[pl.ds(p1_ref[r], 1)], sem).start()
        pltpu.make_async_copy(src, hs_ref.at[pl.ds(p2_ref[r], 1)], sem).start()
        return carry

    lax.fori_loop(0, tm, issue, 0)

    def drain(r, carry):
        src = h_ref.at[pl.ds(r, 1)]
        pltpu.make_async_copy(src, hs_ref.at[pl.ds(0, 1)], sem).wait()
        pltpu.make_async_copy(src, hs_ref.at[pl.ds(0, 1)], sem).wait()
        return carry

    lax.fori_loop(0, tm, drain, 0)


def _dispatch(pos1, pos2, h2):
    n, d = h2.shape
    tm = TM_MOE
    idx = pl.BlockSpec((tm,), lambda i: (i,), memory_space=pltpu.SMEM)
    return pl.pallas_call(
        functools.partial(_dispatch_kernel, tm=tm),
        grid=(n // tm,),
        in_specs=[idx, idx, pl.BlockSpec((tm, d), lambda i: (i, 0))],
        out_specs=pl.BlockSpec(memory_space=pl.ANY),
        out_shape=jax.ShapeDtypeStruct((2 * n, d), F32),
        scratch_shapes=[pltpu.SemaphoreType.DMA(())],
        compiler_params=pltpu.CompilerParams(dimension_semantics=("arbitrary",), has_side_effects=True),
        name="dispatch",
    )(pos1, pos2, h2)


def _expert_kernel(it_tile, it_exp, it_first, it_valid, it_new, offs, hs_ref, wg_ref, wu_ref, wd_ref, ys_ref,
                   wg_bf, wu_bf, wd_bf, *, tm):
    w = pl.program_id(0)

    @pl.when(it_valid[w] == 1)
    def _():
        @pl.when(it_new[w] == 1)
        def _():
            wg_bf[...] = wg_ref[...].astype(BF16)
            wu_bf[...] = wu_ref[...].astype(BF16)
            wd_bf[...] = wd_ref[...].astype(BF16)

        e = it_exp[w]
        base = it_tile[w] * tm
        lo = offs[e] - base
        hi = offs[e + 1] - base
        h = hs_ref[...].astype(BF16)
        gate = jnp.dot(h, wg_bf[...], preferred_element_type=F32)
        up = jnp.dot(h, wu_bf[...], preferred_element_type=F32)
        hid = (jax.nn.silu(gate) * up).astype(BF16)
        y = jnp.dot(hid, wd_bf[...], preferred_element_type=F32)
        row = lax.broadcasted_iota(I32, (tm, 1), 0)
        mine = (row >= lo) & (row < hi)

        @pl.when(it_first[w] == 1)
        def _():
            ys_ref[...] = jnp.where(mine, y, 0.0)

        @pl.when(it_first[w] == 0)
        def _():
            ys_ref[...] = jnp.where(mine, y, ys_ref[...])


def _work_items(offs, n_rows, tm):
    n_items = n_rows // tm + N_EXPERTS
    starts, ends = offs[:N_EXPERTS], offs[1:N_EXPERTS + 1]
    first_tile = starts // tm
    n_e = jnp.where(ends > starts, (ends - 1) // tm - first_tile + 1, 0)
    item_end = jnp.cumsum(n_e)
    item_start = item_end - n_e
    total = item_end[-1]
    w = jnp.arange(n_items, dtype=I32)
    wc = jnp.minimum(w, total - 1)
    exp = jnp.sum(item_end[None, :] <= wc[:, None], axis=1).astype(I32)
    tile = (first_tile[exp] + wc - item_start[exp]).astype(I32)
    valid = (w < total).astype(I32)
    prev_tile = jnp.concatenate([jnp.full((1,), -1, I32), tile[:-1]])
    prev_exp = jnp.concatenate([jnp.full((1,), -1, I32), exp[:-1]])
    first = (tile != prev_tile).astype(I32)
    new = (exp != prev_exp).astype(I32)
    return tile, exp, first, valid, new


def _experts(hs, offs, w_e_gate, w_e_up, w_e_down):
    n_rows, d = hs.shape
    ff = w_e_gate.shape[-1]
    tm = TM_MOE
    tile, exp, first, valid, new = _work_items(offs, n_rows, tm)
    grid_spec = pltpu.PrefetchScalarGridSpec(
        num_scalar_prefetch=6,
        grid=(tile.shape[0],),
        in_specs=[
            pl.BlockSpec((tm, d), lambda w, t, e, *_: (t[w], 0)),
            pl.BlockSpec((None, d, ff), lambda w, t, e, *_: (e[w], 0, 0)),
            pl.BlockSpec((None, d, ff), lambda w, t, e, *_: (e[w], 0, 0)),
            pl.BlockSpec((None, ff, d), lambda w, t, e, *_: (e[w], 0, 0)),
        ],
        out_specs=pl.BlockSpec((tm, d), lambda w, t, e, *_: (t[w], 0)),
        scratch_shapes=[pltpu.VMEM((d, ff), BF16), pltpu.VMEM((d, ff), BF16), pltpu.VMEM((ff, d), BF16)],
    )
    return pl.pallas_call(
        functools.partial(_expert_kernel, tm=tm),
        grid_spec=grid_spec,
        out_shape=jax.ShapeDtypeStruct((n_rows, d), F32),
        compiler_params=pltpu.CompilerParams(dimension_semantics=("arbitrary",)),
        name="experts",
    )(tile, exp, first, valid, new, offs, hs, w_e_gate, w_e_up, w_e_down)


def _combine_kernel(p1_ref, p2_ref, wts_ref, x1_ref, p_ref, gp_ref, wpg_ref, wpp_ref, gf_ref, ys_ref, out_ref,
                    ybuf, sem, *, tm):
    def issue(r, carry):
        pltpu.make_async_copy(ys_ref.at[pl.ds(p1_ref[r], 1)], ybuf.at[0, pl.ds(r, 1)], sem).start()
        pltpu.make_async_copy(ys_ref.at[pl.ds(p2_ref[r], 1)], ybuf.at[1, pl.ds(r, 1)], sem).start()
        return carry

    lax.fori_loop(0, tm, issue, 0)

    def drain(r, carry):
        pltpu.make_async_copy(ys_ref.at[pl.ds(0, 1)], ybuf.at[0, pl.ds(r, 1)], sem).wait()
        pltpu.make_async_copy(ys_ref.at[pl.ds(0, 1)], ybuf.at[1, pl.ds(r, 1)], sem).wait()
        return carry

    lax.fori_loop(0, tm, drain, 0)

    wts = wts_ref[...]
    x2 = x1_ref[...] + (wts[:, 0:1] * ybuf[0] + wts[:, 1:2] * ybuf[1])
    hp = _rmsnorm(x2, gp_ref[...], NORM_EPS).astype(BF16)
    gate = jax.nn.sigmoid(jnp.dot(hp, wpg_ref[...], preferred_element_type=F32))
    proj = jnp.dot(p_ref[...].astype(BF16), wpp_ref[...], preferred_element_type=F32)
    x3 = x2 + gate * proj
    out_ref[...] = _rmsnorm(x3, gf_ref[...], NORM_EPS)


def _combine(pos1, pos2, wts, x1, p2d, g_ple, wpg, wpp, g_final, ys):
    n, d = x1.shape
    pd = p2d.shape[1]
    tm = TM_MOE
    idx = pl.BlockSpec((tm,), lambda i: (i,), memory_space=pltpu.SMEM)
    row = pl.BlockSpec((tm, d), lambda i: (i, 0))
    full = lambda r, c: pl.BlockSpec((r, c), lambda i: (0, 0))
    return pl.pallas_call(
        functools.partial(_combine_kernel, tm=tm),
        grid=(n // tm,),
        in_specs=[idx, idx, pl.BlockSpec((tm, LANES), lambda i: (i, 0)), row,
                  pl.BlockSpec((tm, pd), lambda i: (i, 0)), full(1, d), full(d, d), full(pd, d), full(1, d),
                  pl.BlockSpec(memory_space=pl.ANY)],
        out_specs=row,
        out_shape=jax.ShapeDtypeStruct((n, d), F32),
        scratch_shapes=[pltpu.VMEM((2, tm, d), F32), pltpu.SemaphoreType.DMA(())],
        compiler_params=pltpu.CompilerParams(dimension_semantics=("arbitrary",)),
        name="combine",
    )(pos1, pos2, wts, x1, p2d, g_ple, wpg, wpp, g_final, ys)


def _layer(i, x2, p2d, b, s, g_mix, w_in, lam_q1, lam_k1, lam_q2, lam_k2, g_subln, w_conv, b_conv, w_rg, b_rg,
           w_ig, b_ig, lru_lambda, w_attn_br, w_lru_br, w_out, g_moe, w_rt_group, b_rt_group, w_rt_expert,
           b_rt_expert, w_e_gate, w_e_up, w_e_down, g_ple, w_ple_gate, w_ple_proj):
    n, d = x2.shape
    lam_init = 0.8 - 0.6 * math.exp(-0.3 * i)
    row = lambda a: a.reshape(1, -1)

    q, k, v, xr, gr, ga, gb = _inproj(x2, row(g_mix), w_in.astype(BF16))
    o = _attention(q.reshape(b, s, d), k.reshape(b, s, d), v.reshape(b, s, d),
                   row(lam_q1), row(lam_k1), row(lam_q2), row(lam_k2), row(g_subln), lam_init).reshape(n, d)
    hg = _lru(xr, gr, w_conv, b_conv, w_rg, b_rg, w_ig, b_ig, lru_lambda, b, s)

    pad = LANES - N_GROUPS - N_EXPERTS
    w_router = jnp.concatenate(
        [w_rt_group, w_rt_expert.transpose(1, 0, 2).reshape(d, N_EXPERTS), jnp.zeros((d, pad), F32)], axis=1)
    b_router = jnp.concatenate([b_rt_group, b_rt_expert.reshape(N_EXPERTS), jnp.zeros((pad,), F32)]).reshape(1, LANES)
    x1, h2, logits = _merge(o, hg, ga, gb, x2, w_attn_br.astype(BF16), w_lru_br.astype(BF16), w_out.astype(BF16),
                            row(g_moe), w_router, b_router)

    pos, wts, offs = _route(logits)
    pos1, pos2 = pos[:, 0], pos[:, 1]
    hs = _dispatch(pos1, pos2, h2)
    ys = _experts(hs, offs[0, :N_EXPERTS + 1], w_e_gate, w_e_up, w_e_down)
    return pos1, pos2, wts, x1, ys


def kernel(x, p, g_mix, w_in, lam_q1, lam_k1, lam_q2, lam_k2, g_subln, w_conv, b_conv, w_rg, b_rg, w_ig, b_ig, lru_lambda, w_attn_br, w_lru_br, w_out, g_moe, w_rt_group, b_rt_group, w_rt_expert, b_rt_expert, w_e_gate, w_e_up, w_e_down, g_ple, w_ple_gate, w_ple_proj, g_final):
    b, s, d = x.shape
    depth = p.shape[0]
    assert depth == 1, "the final RMSNorm is fused into the last layer's combine step; one layer supported"
    n = b * s
    x2 = x.reshape(n, d)
    i = 0
    p2d = p[i].reshape(n, -1)
    pos1, pos2, wts, x1, ys = _layer(
        i, x2, p2d, b, s, g_mix[i], w_in[i], lam_q1[i], lam_k1[i], lam_q2[i], lam_k2[i], g_subln[i], w_conv[i],
        b_conv[i], w_rg[i], b_rg[i], w_ig[i], b_ig[i], lru_lambda[i], w_attn_br[i], w_lru_br[i], w_out[i],
        g_moe[i], w_rt_group[i], b_rt_group[i], w_rt_expert[i], b_rt_expert[i], w_e_gate[i], w_e_up[i],
        w_e_down[i], g_ple[i], w_ple_gate[i], w_ple_proj[i])
    out = _combine(pos1, pos2, wts, x1, p2d, g_ple[i].reshape(1, d), w_ple_gate[i].astype(BF16),
                   w_ple_proj[i].astype(BF16), g_final.reshape(1, d), ys)
    return out.reshape(b, s, d)
```

```python
import functools
import math

import jax
import jax.numpy as jnp
from jax import lax
from jax.experimental import pallas as pl
from jax.experimental.pallas import tpu as pltpu

F32 = jnp.float32
BF16 = jnp.bfloat16
I32 = jnp.int32

N_HEADS = 8
HEAD_DIM = 64
V_DIM = 2 * HEAD_DIM
SUBLN_EPS = 1e-5
NORM_EPS = 1e-6
LRU_BLOCKS = 16
CONV_WIDTH = 4
LRU_C = 8.0
N_GROUPS = 4
EXPERTS_PER_GROUP = 8
N_EXPERTS = N_GROUPS * EXPERTS_PER_GROUP
N_IN_PARTS = 7
Q_SCALE = HEAD_DIM ** -0.5

LANES = 128
SUBLANES = 8
MXU_DIM = 256

TM_INPROJ = 256
T_ATTN = 256
T_LRU = 256
TM_MERGE = 256
T_ROUTE = 256
TM_MOE = 256
MASK_NEG = -1e30


def _rmsnorm(x, g, eps):
    return (x * lax.rsqrt(jnp.mean(x * x, axis=-1, keepdims=True) + eps)) * g


def _inproj_kernel(x_ref, g_ref, w_ref, *out_refs, d):
    h = _rmsnorm(x_ref[...], g_ref[...], NORM_EPS).astype(BF16)
    for c, o_ref in enumerate(out_refs):
        z = jnp.dot(h, w_ref[:, c * d:(c + 1) * d], preferred_element_type=F32)
        if c == 0:
            z = z * Q_SCALE
        o_ref[...] = z.astype(o_ref.dtype)


def _inproj(x2, g_mix, w_in_bf):
    n, d = x2.shape
    tm = TM_INPROJ
    out_dtypes = (BF16, BF16, BF16, F32, F32, F32, F32)
    row_spec = pl.BlockSpec((tm, d), lambda i: (i, 0))
    return pl.pallas_call(
        functools.partial(_inproj_kernel, d=d),
        grid=(n // tm,),
        in_specs=[
            row_spec,
            pl.BlockSpec((1, d), lambda i: (0, 0)),
            pl.BlockSpec((d, N_IN_PARTS * d), lambda i: (0, 0), pipeline_mode=pl.Buffered(1)),
        ],
        out_specs=[row_spec] * N_IN_PARTS,
        out_shape=[jax.ShapeDtypeStruct((n, d), dt) for dt in out_dtypes],
        compiler_params=pltpu.CompilerParams(dimension_semantics=("arbitrary",)),
        name="inproj",
    )(x2, g_mix, w_in_bf)


V_AUG = V_DIM + 16
PIPE_SLOTS = 4


def _attn_kernel(lq1_ref, lk1_ref, lq2_ref, lk2_ref, g_ref, q_ref, k_ref, v_ref, o_ref,
                 qt_scr, vt_scr, acc_scr, m_scr, bias_scr, s_scr, p_scr, *, t, lam_init):
    nq = q_ref.shape[0] // t
    w = 2 * t

    feat = lax.broadcasted_iota(I32, (t, V_DIM), 1)
    ones_pad = jnp.where(lax.broadcasted_iota(I32, (V_AUG - V_DIM, t), 0) == 0, 1.0, 0.0).astype(F32)

    def prep(i, carry):
        rows = pl.ds(pl.multiple_of(i * t, t), t)
        q = q_ref[rows, :].astype(F32)
        q1t = jnp.where(feat < HEAD_DIM, q, 0.0).T
        q2t = jnp.where(feat >= HEAD_DIM, q, 0.0).T
        qt_scr[i] = jnp.concatenate([q1t, q2t], axis=1).astype(BF16)
        vt = v_ref[rows, :].astype(F32).T
        vt_scr[i] = jnp.concatenate([vt, ones_pad], axis=0).astype(BF16)
        m_scr[i] = jnp.full((1, w), MASK_NEG, F32)
        acc_scr[i] = jnp.zeros((V_AUG, w), F32)
        return carry

    lax.fori_loop(0, nq, prep, 0)
    key = lax.broadcasted_iota(I32, (t, w), 0)
    qry = lax.broadcasted_iota(I32, (t, w), 1) & (t - 1)
    bias_scr[...] = jnp.where(key <= qry, 0.0, MASK_NEG)

    def stage_a(j, i, slot, masked):
        kj = k_ref[pl.ds(pl.multiple_of(j * t, t), t), :]
        s = jnp.dot(kj, qt_scr[i], preferred_element_type=F32)
        if masked:
            s = s + bias_scr[...]
        s_scr[slot] = s
        return jnp.max(s, axis=0, keepdims=True)

    def stage_b(i, slot, cmax):
        m_old = m_scr[i]
        m_new = jnp.maximum(m_old, cmax)
        m_scr[i] = m_new
        p_scr[slot] = jnp.exp(s_scr[slot] - m_new).astype(BF16)
        return jnp.exp(m_old - m_new)

    def stage_c(j, i, slot, alpha):
        acc_scr[i] = alpha * acc_scr[i] + jnp.dot(vt_scr[j], p_scr[slot], preferred_element_type=F32)

    def run(n_steps, first, nxt, masked):
        assert n_steps >= PIPE_SLOTS and n_steps % PIPE_SLOTS == 0
        a = functools.partial(stage_a, masked=masked)
        s0 = first
        s1 = nxt(*s0)
        s2 = nxt(*s1)
        s3 = nxt(*s2)
        cm0, cm1 = a(*s0, 0), a(*s1, 1)
        al0, al1 = stage_b(s0[1], 0, cm0), stage_b(s1[1], 1, cm1)
        cm2, cm3 = a(*s2, 2), a(*s3, 3)

        def half(state, sl):
            c0, c1, alc0, alc1, b0, b1, cmb0, cmb1 = state
            other = (sl + 2) % PIPE_SLOTS
            stage_c(*c0, sl, alc0)
            stage_c(*c1, sl + 1, alc1)
            alb0 = stage_b(b0[1], other, cmb0)
            alb1 = stage_b(b1[1], other + 1, cmb1)
            a0 = nxt(*b1)
            a1 = nxt(*a0)
            return b0, b1, alb0, alb1, a0, a1, a(*a0, sl), a(*a1, sl + 1)

        def body(_, state):
            return half(half(state, 0), 2)

        state = lax.fori_loop(0, (n_steps - PIPE_SLOTS) // PIPE_SLOTS, body, (s0, s1, al0, al1, s2, s3, cm2, cm3))
        c0, c1, alc0, alc1, b0, b1, cmb0, cmb1 = state
        stage_c(*c0, 0, alc0)
        stage_c(*c1, 1, alc1)
        alb0 = stage_b(b0[1], 2, cmb0)
        alb1 = stage_b(b1[1], 3, cmb1)
        stage_c(*b0, 2, alb0)
        stage_c(*b1, 3, alb1)

    zero = jnp.int32(0)
    run(nq, (zero, zero), lambda j, i: (j + 1, i + 1), masked=True)

    def next_below_diagonal(j, i):
        wrap = i + 1 >= nq
        return jnp.where(wrap, j + 1, j), jnp.where(wrap, j + 2, i + 1)

    run(nq * (nq - 1) // 2, (zero, zero + 1), next_below_diagonal, masked=False)

    lam = (jnp.exp(jnp.sum(lq1_ref[...] * lk1_ref[...], axis=1, keepdims=True))
           - jnp.exp(jnp.sum(lq2_ref[...] * lk2_ref[...], axis=1, keepdims=True)) + lam_init)
    gain = g_ref[...]

    def finish(i, carry):
        acc = acc_scr[i]
        o1 = acc[:V_DIM, :t] / acc[V_DIM:V_DIM + 1, :t]
        o2 = acc[:V_DIM, t:] / acc[V_DIM:V_DIM + 1, t:]
        o = o1 - lam * o2
        y = ((o * lax.rsqrt(jnp.mean(o * o, axis=0, keepdims=True) + SUBLN_EPS)) * gain) * (1.0 - lam_init)
        o_ref[pl.ds(pl.multiple_of(i * t, t), t), :] = y.T.astype(o_ref.dtype)
        return carry

    lax.fori_loop(0, nq, finish, 0)


def _attention(q, k, v, lam_q1, lam_k1, lam_q2, lam_k2, g_subln, lam_init):
    b, s, d = q.shape
    t = T_ATTN
    assert s % t == 0 and d == N_HEADS * V_DIM
    nq = s // t
    small = lambda w: pl.BlockSpec((1, w), lambda bi, hi: (0, 0))
    head = pl.BlockSpec((None, s, V_DIM), lambda bi, hi: (bi, 0, hi))
    return pl.pallas_call(
        functools.partial(_attn_kernel, t=t, lam_init=lam_init),
        grid=(b, N_HEADS),
        in_specs=[small(HEAD_DIM), small(HEAD_DIM), small(HEAD_DIM), small(HEAD_DIM),
                  pl.BlockSpec((V_DIM, 1), lambda bi, hi: (0, 0)), head, head, head],
        out_specs=head,
        out_shape=jax.ShapeDtypeStruct((b, s, d), BF16),
        scratch_shapes=[
            pltpu.VMEM((nq, V_DIM, 2 * t), BF16),
            pltpu.VMEM((nq, V_AUG, t), BF16),
            pltpu.VMEM((nq, V_AUG, 2 * t), F32),
            pltpu.VMEM((nq, 1, 2 * t), F32),
            pltpu.VMEM((t, 2 * t), F32),
            pltpu.VMEM((PIPE_SLOTS, t, 2 * t), F32),
            pltpu.VMEM((PIPE_SLOTS, t, 2 * t), BF16),
        ],
        compiler_params=pltpu.CompilerParams(dimension_semantics=("arbitrary", "arbitrary")),
        name="attn",
    )(lam_q1, lam_k1, lam_q2, lam_k2, g_subln.reshape(V_DIM, 1), q, k, v)


def _lru_kernel(xr_ref, gr_ref, wc_ref, bc_ref, wrg_ref, brg_ref, wig_ref, big_ref, lam_ref, o_ref,
                xprev_ref, hcar_ref, *, t_tile):
    @pl.when(pl.program_id(1) == 0)
    def _():
        xprev_ref[...] = jnp.zeros_like(xprev_ref)
        hcar_ref[...] = jnp.zeros_like(hcar_ref)

    x = xr_ref[...]
    w = x.shape[1]
    prev = xprev_ref[...]
    wc = wc_ref[...]
    row8 = lax.broadcasted_iota(I32, (SUBLANES, w), 0)
    xc = bc_ref[...]
    for j in range(CONV_WIDTH):
        shift = CONV_WIDTH - 1 - j
        if shift == 0:
            xs = x
        else:
            rolled = pltpu.roll(x, shift, 0)
            head = jnp.where(row8 < shift, pltpu.roll(prev, shift, 0), rolled[:SUBLANES])
            xs = jnp.concatenate([head, rolled[SUBLANES:]], axis=0)
        xc = xc + xs * wc[j:j + 1]
    xprev_ref[...] = x[t_tile - SUBLANES:]

    xcb = xc.astype(BF16)
    n_grp = w // MXU_DIM
    pre_r = jnp.concatenate(
        [jnp.dot(xcb[:, g * MXU_DIM:(g + 1) * MXU_DIM], wrg_ref[g], preferred_element_type=F32) for g in range(n_grp)],
        axis=1) + brg_ref[...]
    pre_i = jnp.concatenate(
        [jnp.dot(xcb[:, g * MXU_DIM:(g + 1) * MXU_DIM], wig_ref[g], preferred_element_type=F32) for g in range(n_grp)],
        axis=1) + big_ref[...]
    r = jax.nn.sigmoid(pre_r)
    ig = jax.nn.sigmoid(pre_i)
    neg_lam = -lam_ref[...]
    softplus = jnp.maximum(neg_lam, 0.0) + jnp.log1p(jnp.exp(-jnp.abs(neg_lam)))
    log_a = (-LRU_C) * r * softplus
    a = jnp.exp(log_a)
    th = jnp.tanh(log_a)
    u = jnp.sqrt((-2.0 * th) / (1.0 - th)) * (ig * xc)

    rows = lax.broadcasted_iota(I32, x.shape, 0)
    dist = 1
    while dist < t_tile:
        keep = rows >= dist
        u = jnp.where(keep, a * pltpu.roll(u, dist, 0) + u, u)
        a = jnp.where(keep, a * pltpu.roll(a, dist, 0), a)
        dist *= 2
    h = a * hcar_ref[...] + u
    hcar_ref[...] = h[t_tile - 1:]
    o_ref[...] = (h * jax.nn.gelu(gr_ref[...])).astype(o_ref.dtype)


def _block_diag_groups(wblk):
    nb, bw, _ = wblk.shape
    per = MXU_DIM // bw
    g = nb // per
    eye = jnp.eye(per, dtype=wblk.dtype)
    w5 = wblk.reshape(g, per, bw, bw)
    return jnp.einsum("gawv,ab->gawbv", w5, eye).reshape(g, MXU_DIM, MXU_DIM).astype(BF16)


def _lru(xr, gr, w_conv, b_conv, w_rg, b_rg, w_ig, b_ig, lru_lambda, b, s):
    n, w = xr.shape
    t = T_LRU
    n_t = s // t
    row_spec = pl.BlockSpec((t, w), lambda bi, ti: (bi * n_t + ti, 0))
    vec = lambda r: pl.BlockSpec((r, w), lambda bi, ti: (0, 0))
    blk = pl.BlockSpec((w // MXU_DIM, MXU_DIM, MXU_DIM), lambda bi, ti: (0, 0, 0))
    return pl.pallas_call(
        functools.partial(_lru_kernel, t_tile=t),
        grid=(b, n_t),
        in_specs=[row_spec, row_spec, vec(CONV_WIDTH), vec(1), blk, vec(1), blk, vec(1), vec(1)],
        out_specs=row_spec,
        out_shape=jax.ShapeDtypeStruct((n, w), BF16),
        scratch_shapes=[pltpu.VMEM((SUBLANES, w), F32), pltpu.VMEM((1, w), F32)],
        compiler_params=pltpu.CompilerParams(dimension_semantics=("arbitrary", "arbitrary")),
        name="lru",
    )(xr, gr, w_conv, b_conv.reshape(1, w), _block_diag_groups(w_rg), b_rg.reshape(1, w),
      _block_diag_groups(w_ig), b_ig.reshape(1, w), lru_lambda.reshape(1, w))


def _merge_kernel(o_ref, hg_ref, ga_ref, gb_ref, x_ref, wa_ref, wl_ref, wo_ref, gm_ref, wr_ref, br_ref,
                  x1_ref, h2_ref, lg_ref):
    attn_br = jnp.dot(o_ref[...], wa_ref[...], preferred_element_type=F32)
    lru_br = jnp.dot(hg_ref[...], wl_ref[...], preferred_element_type=F32)
    mixed = jax.nn.sigmoid(ga_ref[...]) * attn_br + jax.nn.sigmoid(gb_ref[...]) * lru_br
    x1 = x_ref[...] + jnp.dot(mixed.astype(BF16), wo_ref[...], preferred_element_type=F32)
    x1_ref[...] = x1
    h2 = _rmsnorm(x1, gm_ref[...], NORM_EPS)
    h2_ref[...] = h2
    lg_ref[...] = jnp.dot(h2, wr_ref[...], preferred_element_type=F32,
                          precision=lax.Precision.HIGHEST) + br_ref[...]


def _merge(o, hg, ga, gb, x2, wa, wl, wo, g_moe, w_router, b_router):
    n, d = x2.shape
    tm = TM_MERGE
    row = pl.BlockSpec((tm, d), lambda i: (i, 0))
    full = lambda r, c: pl.BlockSpec((r, c), lambda i: (0, 0))
    return pl.pallas_call(
        _merge_kernel,
        grid=(n // tm,),
        in_specs=[row, row, row, row, row, full(d, d), full(d, d), full(d, d), full(1, d),
                  full(d, LANES), full(1, LANES)],
        out_specs=[row, row, pl.BlockSpec((tm, LANES), lambda i: (i, 0))],
        out_shape=[jax.ShapeDtypeStruct((n, d), F32), jax.ShapeDtypeStruct((n, d), F32),
                   jax.ShapeDtypeStruct((n, LANES), F32)],
        compiler_params=pltpu.CompilerParams(dimension_semantics=("arbitrary",)),
        name="merge",
    )(o, hg, ga, gb, x2, wa, wl, wo, g_moe, w_router, b_router)


def _lane_pick(x, lane, idx):
    return jnp.sum(jnp.where(lane == idx, x, jnp.zeros_like(x)), axis=1, keepdims=True)


def _route_kernel(lg_ref, pos_ref, wts_ref, offs_ref, *, n, t_tile):
    shape = (t_tile, LANES)
    lane = lax.broadcasted_iota(I32, shape, 1)
    neg_inf = jnp.float32(-jnp.inf)
    tri = (lax.broadcasted_iota(I32, (t_tile, t_tile), 0) > lax.broadcasted_iota(I32, (t_tile, t_tile), 1)).astype(BF16)

    def phase1(t, cnt):
        rows = pl.ds(pl.multiple_of(t * t_tile, t_tile), t_tile)
        lg = lg_ref[rows, :]
        is_grp = lane < N_GROUPS
        gl = jnp.where(is_grp, lg, neg_inf)
        gmax = jnp.max(gl, axis=1, keepdims=True)
        g_idx = jnp.min(jnp.where(gl == gmax, lane, LANES), axis=1, keepdims=True)
        g_w = 1.0 / jnp.sum(jnp.where(is_grp, jnp.exp(lg - gmax), 0.0), axis=1, keepdims=True)
        lo = N_GROUPS + EXPERTS_PER_GROUP * g_idx
        in_grp = (lane >= lo) & (lane < lo + EXPERTS_PER_GROUP)
        fl = jnp.where(in_grp, lg, neg_inf)
        v1 = jnp.max(fl, axis=1, keepdims=True)
        i1 = jnp.min(jnp.where(in_grp & (fl == v1), lane, LANES), axis=1, keepdims=True)
        rest = in_grp & (lane != i1)
        fl2 = jnp.where(rest, lg, neg_inf)
        v2 = jnp.max(fl2, axis=1, keepdims=True)
        i2 = jnp.min(jnp.where(rest & (fl2 == v2), lane, LANES), axis=1, keepdims=True)
        t2 = jnp.exp(v2 - v1)
        den = 1.0 + t2
        w1 = g_w * (1.0 / den)
        w2 = g_w * (t2 / den)
        e1 = i1 - N_GROUPS
        e2 = i2 - N_GROUPS
        onehot = ((lane == e1) | (lane == e2)).astype(F32)
        before = jnp.dot(tri, onehot.astype(BF16), preferred_element_type=F32) + cnt
        rank1 = _lane_pick(before, lane, e1)
        rank2 = _lane_pick(before, lane, e2)
        pos_ref[rows, :] = jnp.where(lane == 0, e1, jnp.where(lane == 1, e2, jnp.where(
            lane == 2, rank1.astype(I32), jnp.where(lane == 3, rank2.astype(I32), 0))))
        wts_ref[rows, :] = jnp.where(lane == 0, w1, jnp.where(lane == 1, w2, 0.0))
        return cnt + jnp.sum(onehot, axis=0, keepdims=True)

    cnt = lax.fori_loop(0, n // t_tile, phase1, jnp.zeros((1, LANES), F32))

    lane8 = lax.broadcasted_iota(I32, (SUBLANES, LANES), 1)
    incl = jnp.broadcast_to(cnt, (SUBLANES, LANES))
    dist = 1
    while dist < LANES:
        incl = incl + jnp.where(lane8 >= dist, pltpu.roll(incl, dist, 1), 0.0)
        dist *= 2
    offs = incl - cnt
    offs_ref[...] = offs.astype(I32)
    offs_row = offs[0:1]

    def phase2(t, carry):
        rows = pl.ds(pl.multiple_of(t * t_tile, t_tile), t_tile)
        info = pos_ref[rows, :]
        e1 = _lane_pick(info, lane, 0)
        e2 = _lane_pick(info, lane, 1)
        r1 = _lane_pick(info, lane, 2)
        r2 = _lane_pick(info, lane, 3)
        offs_b = jnp.broadcast_to(offs_row, shape).astype(I32)
        p1 = r1 + _lane_pick(offs_b, lane, e1)
        p2 = r2 + _lane_pick(offs_b, lane, e2)
        pos_ref[rows, :] = jnp.where(lane == 0, p1, jnp.where(lane == 1, p2, 0))
        return carry

    lax.fori_loop(0, n // t_tile, phase2, 0)


def _route(logits):
    n = logits.shape[0]
    return pl.pallas_call(
        functools.partial(_route_kernel, n=n, t_tile=T_ROUTE),
        out_shape=[jax.ShapeDtypeStruct((n, LANES), I32), jax.ShapeDtypeStruct((n, LANES), F32),
                   jax.ShapeDtypeStruct((SUBLANES, LANES), I32)],
        name="route",
    )(logits)


def _dispatch_kernel(p1_ref, p2_ref, h_ref, hs_ref, sem, *, tm):
    def issue(r, carry):
        src = h_ref.at[pl.ds(r, 1)]
        pltpu.make_async_copy(src, hs_ref.at[pl.ds(p1_ref[r], 1)], sem).start()
        pltpu.make_async_copy(src, hs_ref.at[pl.ds(p2_ref[r], 1)], sem).start()
        return carry

    lax.fori_loop(0, tm, issue, 0)

    def drain(r, carry):
        src = h_ref.at[pl.ds(r, 1)]
        pltpu.make_async_copy(src, hs_ref.at[pl.ds(0, 1)], sem).wait()
        pltpu.make_async_copy(src, hs_ref.at[pl.ds(0, 1)], sem).wait()
        return carry

    lax.fori_loop(0, tm, drain, 0)


def _dispatch(pos1, pos2, h2):
    n, d = h2.shape
    tm = TM_MOE
    idx = pl.BlockSpec((tm,), lambda i: (i,), memory_space=pltpu.SMEM)
    return pl.pallas_call(
        functools.partial(_dispatch_kernel, tm=tm),
        grid=(n // tm,),
        in_specs=[idx, idx, pl.BlockSpec((tm, d), lambda i: (i, 0))],
        out_specs=pl.BlockSpec(memory_space=pl.ANY),
        out_shape=jax.ShapeDtypeStruct((2 * n, d), F32),
        scratch_shapes=[pltpu.SemaphoreType.DMA(())],
        compiler_params=pltpu.CompilerParams(dimension_semantics=("arbitrary",), has_side_effects=True),
        name="dispatch",
    )(pos1, pos2, h2)


def _expert_kernel(it_tile, it_exp, it_first, it_valid, it_new, offs, hs_ref, wg_ref, wu_ref, wd_ref, ys_ref,
                   wg_bf, wu_bf, wd_bf, *, tm):
    w = pl.program_id(0)

    @pl.when(it_valid[w] == 1)
    def _():
        @pl.when(it_new[w] == 1)
        def _():
            wg_bf[...] = wg_ref[...].astype(BF16)
            wu_bf[...] = wu_ref[...].astype(BF16)
            wd_bf[...] = wd_ref[...].astype(BF16)

        e = it_exp[w]
        base = it_tile[w] * tm
        lo = offs[e] - base
        hi = offs[e + 1] - base
        h = hs_ref[...].astype(BF16)
        gate = jnp.dot(h, wg_bf[...], preferred_element_type=F32)
        up = jnp.dot(h, wu_bf[...], preferred_element_type=F32)
        hid = (jax.nn.silu(gate) * up).astype(BF16)
        y = jnp.dot(hid, wd_bf[...], preferred_element_type=F32)
        row = lax.broadcasted_iota(I32, (tm, 1), 0)
        mine = (row >= lo) & (row < hi)

        @pl.when(it_first[w] == 1)
        def _():
            ys_ref[...] = jnp.where(mine, y, 0.0)

        @pl.when(it_first[w] == 0)
        def _():
            ys_ref[...] = jnp.where(mine, y, ys_ref[...])


def _work_items(offs, n_rows, tm):
    n_items = n_rows // tm + N_EXPERTS
    starts, ends = offs[:N_EXPERTS], offs[1:N_EXPERTS + 1]
    first_tile = starts // tm
    n_e = jnp.where(ends > starts, (ends - 1) // tm - first_tile + 1, 0)
    item_end = jnp.cumsum(n_e)
    item_start = item_end - n_e
    total = item_end[-1]
    w = jnp.arange(n_items, dtype=I32)
    wc = jnp.minimum(w, total - 1)
    exp = jnp.sum(item_end[None, :] <= wc[:, None], axis=1).astype(I32)
    tile = (first_tile[exp] + wc - item_start[exp]).astype(I32)
    valid = (w < total).astype(I32)
    prev_tile = jnp.concatenate([jnp.full((1,), -1, I32), tile[:-1]])
    prev_exp = jnp.concatenate([jnp.full((1,), -1, I32), exp[:-1]])
    first = (tile != prev_tile).astype(I32)
    new = (exp != prev_exp).astype(I32)
    return tile, exp, first, valid, new


def _experts(hs, offs, w_e_gate, w_e_up, w_e_down):
    n_rows, d = hs.shape
    ff = w_e_gate.shape[-1]
    tm = TM_MOE
    tile, exp, first, valid, new = _work_items(offs, n_rows, tm)
    grid_spec = pltpu.PrefetchScalarGridSpec(
        num_scalar_prefetch=6,
        grid=(tile.shape[0],),
        in_specs=[
            pl.BlockSpec((tm, d), lambda w, t, e, *_: (t[w], 0)),
            pl.BlockSpec((None, d, ff), lambda w, t, e, *_: (e[w], 0, 0)),
            pl.BlockSpec((None, d, ff), lambda w, t, e, *_: (e[w], 0, 0)),
            pl.BlockSpec((None, ff, d), lambda w, t, e, *_: (e[w], 0, 0)),
        ],
        out_specs=pl.BlockSpec((tm, d), lambda w, t, e, *_: (t[w], 0)),
        scratch_shapes=[pltpu.VMEM((d, ff), BF16), pltpu.VMEM((d, ff), BF16), pltpu.VMEM((ff, d), BF16)],
    )
    return pl.pallas_call(
        functools.partial(_expert_kernel, tm=tm),
        grid_spec=grid_spec,
        out_shape=jax.ShapeDtypeStruct((n_rows, d), F32),
        compiler_params=pltpu.CompilerParams(dimension_semantics=("arbitrary",)),
        name="experts",
    )(tile, exp, first, valid, new, offs, hs, w_e_gate, w_e_up, w_e_down)


def _combine_kernel(p1_ref, p2_ref, wts_ref, x1_ref, p_ref, gp_ref, wpg_ref, wpp_ref, gf_ref, ys_ref, out_ref,
                    ybuf, sem, *, tm):
    def issue(r, carry):
        pltpu.make_async_copy(ys_ref.at[pl.ds(p1_ref[r], 1)], ybuf.at[0, pl.ds(r, 1)], sem).start()
        pltpu.make_async_copy(ys_ref.at[pl.ds(p2_ref[r], 1)], ybuf.at[1, pl.ds(r, 1)], sem).start()
        return carry

    lax.fori_loop(0, tm, issue, 0)

    def drain(r, carry):
        pltpu.make_async_copy(ys_ref.at[pl.ds(0, 1)], ybuf.at[0, pl.ds(r, 1)], sem).wait()
        pltpu.make_async_copy(ys_ref.at[pl.ds(0, 1)], ybuf.at[1, pl.ds(r, 1)], sem).wait()
        return carry

    lax.fori_loop(0, tm, drain, 0)

    wts = wts_ref[...]
    x2 = x1_ref[...] + (wts[:, 0:1] * ybuf[0] + wts[:, 1:2] * ybuf[1])
    hp = _rmsnorm(x2, gp_ref[...], NORM_EPS).astype(BF16)
    gate = jax.nn.sigmoid(jnp.dot(hp, wpg_ref[...], preferred_element_type=F32))
    proj = jnp.dot(p_ref[...].astype(BF16), wpp_ref[...], preferred_element_type=F32)
    x3 = x2 + gate * proj
    out_ref[...] = _rmsnorm(x3, gf_ref[...], NORM_EPS)


def _combine(pos1, pos2, wts, x1, p2d, g_ple, wpg, wpp, g_final, ys):
    n, d = x1.shape
    pd = p2d.shape[1]
    tm = TM_MOE
    idx = pl.BlockSpec((tm,), lambda i: (i,), memory_space=pltpu.SMEM)
    row = pl.BlockSpec((tm, d), lambda i: (i, 0))
    full = lambda r, c: pl.BlockSpec((r, c), lambda i: (0, 0))
    return pl.pallas_call(
        functools.partial(_combine_kernel, tm=tm),
        grid=(n // tm,),
        in_specs=[idx, idx, pl.BlockSpec((tm, LANES), lambda i: (i, 0)), row,
                  pl.BlockSpec((tm, pd), lambda i: (i, 0)), full(1, d), full(d, d), full(pd, d), full(1, d),
                  pl.BlockSpec(memory_space=pl.ANY)],
        out_specs=row,
        out_shape=jax.ShapeDtypeStruct((n, d), F32),
        scratch_shapes=[pltpu.VMEM((2, tm, d), F32), pltpu.SemaphoreType.DMA(())],
        compiler_params=pltpu.CompilerParams(dimension_semantics=("arbitrary",)),
        name="combine",
    )(pos1, pos2, wts, x1, p2d, g_ple, wpg, wpp, g_final, ys)


def _layer(i, x2, p2d, b, s, g_mix, w_in, lam_q1, lam_k1, lam_q2, lam_k2, g_subln, w_conv, b_conv, w_rg, b_rg,
           w_ig, b_ig, lru_lambda, w_attn_br, w_lru_br, w_out, g_moe, w_rt_group, b_rt_group, w_rt_expert,
           b_rt_expert, w_e_gate, w_e_up, w_e_down, g_ple, w_ple_gate, w_ple_proj):
    n, d = x2.shape
    lam_init = 0.8 - 0.6 * math.exp(-0.3 * i)
    row = lambda a: a.reshape(1, -1)

    q, k, v, xr, gr, ga, gb = _inproj(x2, row(g_mix), w_in.astype(BF16))
    o = _attention(q.reshape(b, s, d), k.reshape(b, s, d), v.reshape(b, s, d),
                   row(lam_q1), row(lam_k1), row(lam_q2), row(lam_k2), row(g_subln), lam_init).reshape(n, d)
    hg = _lru(xr, gr, w_conv, b_conv, w_rg, b_rg, w_ig, b_ig, lru_lambda, b, s)

    pad = LANES - N_GROUPS - N_EXPERTS
    w_router = jnp.concatenate(
        [w_rt_group, w_rt_expert.transpose(1, 0, 2).reshape(d, N_EXPERTS), jnp.zeros((d, pad), F32)], axis=1)
    b_router = jnp.concatenate([b_rt_group, b_rt_expert.reshape(N_EXPERTS), jnp.zeros((pad,), F32)]).reshape(1, LANES)
    x1, h2, logits = _merge(o, hg, ga, gb, x2, w_attn_br.astype(BF16), w_lru_br.astype(BF16), w_out.astype(BF16),
                            row(g_moe), w_router, b_router)

    pos, wts, offs = _route(logits)
    pos1, pos2 = pos[:, 0], pos[:, 1]
    hs = _dispatch(pos1, pos2, h2)
    ys = _experts(hs, offs[0, :N_EXPERTS + 1], w_e_gate, w_e_up, w_e_down)
    return pos1, pos2, wts, x1, ys


def kernel(x, p, g_mix, w_in, lam_q1, lam_k1, lam_q2, lam_k2, g_subln, w_conv, b_conv, w_rg, b_rg, w_ig, b_ig, lru_lambda, w_attn_br, w_lru_br, w_out, g_moe, w_rt_group, b_rt_group, w_rt_expert, b_rt_expert, w_e_gate, w_e_up, w_e_down, g_ple, w_ple_gate, w_ple_proj, g_final):
    b, s, d = x.shape
    depth = p.shape[0]
    assert depth == 1, "the final RMSNorm is fused into the last layer's combine step; one layer supported"
    n = b * s
    x2 = x.reshape(n, d)
    i = 0
    p2d = p[i].reshape(n, -1)
    pos1, pos2, wts, x1, ys = _layer(
        i, x2, p2d, b, s, g_mix[i], w_in[i], lam_q1[i], lam_k1[i], lam_q2[i], lam_k2[i], g_subln[i], w_conv[i],
        b_conv[i], w_rg[i], b_rg[i], w_ig[i], b_ig[i], lru_lambda[i], w_attn_br[i], w_lru_br[i], w_out[i],
        g_moe[i], w_rt_group[i], b_rt_group[i], w_rt_expert[i], b_rt_expert[i], w_e_gate[i], w_e_up[i],
        w_e_down[i], g_ple[i], w_ple_gate[i], w_ple_proj[i])
    out = _combine(pos1, pos2, wts, x1, p2d, g_ple[i].reshape(1, d), w_ple_gate[i].astype(BF16),
                   w_ple_proj[i].astype(BF16), g_final.reshape(1, d), ys)
    return out.reshape(b, s, d)
```

```python
import functools
import math

import jax
import jax.numpy as jnp
from jax import lax
from jax.experimental import pallas as pl
from jax.experimental.pallas import tpu as pltpu

F32 = jnp.float32
BF16 = jnp.bfloat16
I32 = jnp.int32

N_HEADS = 8
HEAD_DIM = 64
V_DIM = 2 * HEAD_DIM
SUBLN_EPS = 1e-5
NORM_EPS = 1e-6
LRU_BLOCKS = 16
CONV_WIDTH = 4
LRU_C = 8.0
N_GROUPS = 4
EXPERTS_PER_GROUP = 8
N_EXPERTS = N_GROUPS * EXPERTS_PER_GROUP
N_IN_PARTS = 7
LOG2_E = 1.4426950408889634
Q_SCALE = HEAD_DIM ** -0.5 * LOG2_E

LANES = 128
SUBLANES = 8
MXU_DIM = 256

TM_INPROJ = 256
T_ATTN = 256
T_LRU = 256
TM_MERGE = 512
T_ROUTE = 512
TM_MOE = 256
DMA_ISSUE_UNROLL = 8
MASK_NEG = -1e30


def _rmsnorm(x, g, eps):
    return (x * lax.rsqrt(jnp.mean(x * x, axis=-1, keepdims=True) + eps)) * g


def _inproj_kernel(x_ref, g_ref, w_ref, *out_refs, d):
    h = _rmsnorm(x_ref[...], g_ref[...], NORM_EPS).astype(BF16)
    for c, o_ref in enumerate(out_refs):
        z = jnp.dot(h, w_ref[:, c * d:(c + 1) * d], preferred_element_type=F32)
        if c == 0:
            z = z * Q_SCALE
        o_ref[...] = z.astype(o_ref.dtype)


def _inproj(x2, g_mix, w_in_bf):
    n, d = x2.shape
    tm = TM_INPROJ
    out_dtypes = (BF16, BF16, BF16, F32, F32, F32, F32)
    row_spec = pl.BlockSpec((tm, d), lambda i: (i, 0))
    return pl.pallas_call(
        functools.partial(_inproj_kernel, d=d),
        grid=(n // tm,),
        in_specs=[
            row_spec,
            pl.BlockSpec((1, d), lambda i: (0, 0)),
            pl.BlockSpec((d, N_IN_PARTS * d), lambda i: (0, 0), pipeline_mode=pl.Buffered(1)),
        ],
        out_specs=[row_spec] * N_IN_PARTS,
        out_shape=[jax.ShapeDtypeStruct((n, d), dt) for dt in out_dtypes],
        compiler_params=pltpu.CompilerParams(dimension_semantics=("arbitrary",)),
        name="inproj",
    )(x2, g_mix, w_in_bf)


V_AUG = V_DIM + 16
PIPE_GROUP = 4
PIPE_RING = 4
PIPE_SLOTS = PIPE_GROUP * PIPE_RING


def _attn_kernel(lq1_ref, lk1_ref, lq2_ref, lk2_ref, g_ref, q_ref, k_ref, v_ref, o_ref,
                 qt_scr, vt_scr, acc_scr, m_scr, bias_scr, s_scr, p_scr, cmax_scr, alpha_scr, *, t, lam_init):
    nq = q_ref.shape[0] // t
    w = 2 * t

    feat = lax.broadcasted_iota(I32, (t, V_DIM), 1)
    ones_pad = jnp.where(lax.broadcasted_iota(I32, (V_AUG - V_DIM, t), 0) == 0, 1.0, 0.0).astype(F32)

    def prep(i, carry):
        rows = pl.ds(pl.multiple_of(i * t, t), t)
        q = q_ref[rows, :].astype(F32)
        q1t = jnp.where(feat < HEAD_DIM, q, 0.0).T
        q2t = jnp.where(feat >= HEAD_DIM, q, 0.0).T
        qt_scr[i] = jnp.concatenate([q1t, q2t], axis=1).astype(BF16)
        vt = v_ref[rows, :].astype(F32).T
        vt_scr[i] = jnp.concatenate([vt, ones_pad], axis=0).astype(BF16)
        m_scr[i] = jnp.full((1, w), MASK_NEG, F32)
        acc_scr[i] = jnp.zeros((V_AUG, w), F32)
        return carry

    lax.fori_loop(0, nq, prep, 0)
    key = lax.broadcasted_iota(I32, (t, w), 0)
    qry = lax.broadcasted_iota(I32, (t, w), 1) & (t - 1)
    bias_scr[...] = jnp.where(key <= qry, 0.0, MASK_NEG)

    def stage_a(j, i, slot, masked):
        kj = k_ref[pl.ds(pl.multiple_of(j * t, t), t), :]
        s = jnp.dot(kj, qt_scr[i], preferred_element_type=F32)
        if masked:
            s = s + bias_scr[...]
        s_scr[slot] = s
        cmax_scr[slot] = jnp.max(s, axis=0, keepdims=True)

    def stage_b(i, slot):
        m_old = m_scr[i]
        m_new = jnp.maximum(m_old, cmax_scr[slot])
        m_scr[i] = m_new
        p_scr[slot] = jnp.exp2(s_scr[slot] - m_new).astype(BF16)
        alpha_scr[slot] = jnp.exp2(m_old - m_new)

    def stage_c(j, i, slot):
        acc_scr[i] = alpha_scr[slot] * acc_scr[i] + jnp.dot(vt_scr[j], p_scr[slot], preferred_element_type=F32)

    def run(n_steps, first, nxt, masked):
        assert n_steps % (PIPE_RING * PIPE_GROUP) == 0
        n_groups = n_steps // PIPE_GROUP

        def steps_of(st):
            out = []
            for _ in range(PIPE_GROUP):
                out.append(st)
                st = nxt(*st)
            return out, st

        def turn(t_mod, a=None, b=None, c=None):
            base_a = (t_mod % PIPE_RING) * PIPE_GROUP
            base_b = ((t_mod - 2) % PIPE_RING) * PIPE_GROUP
            a_steps, following = steps_of(a) if a is not None else (None, None)
            b_steps = steps_of(b)[0] if b is not None else None
            c_steps = steps_of(c)[0] if c is not None else None
            for kk in range(PIPE_GROUP):
                if a_steps is not None:
                    stage_a(*a_steps[kk], base_a + kk, masked)
                if b_steps is not None:
                    stage_b(b_steps[kk][1], base_b + kk)
                if c_steps is not None:
                    stage_c(*c_steps[kk], base_a + kk)
            return following

        f0 = first
        f1 = turn(0, a=f0)
        f2 = turn(1, a=f1)
        f3 = turn(2, a=f2, b=f0)
        f4 = turn(3, a=f3, b=f1)

        def ring(_, f):
            f = list(f)
            for r in range(PIPE_RING):
                f.append(turn(r, a=f[-1], b=f[-3], c=f[-5]))
            return tuple(f[-5:])

        f = lax.fori_loop(0, (n_groups - PIPE_RING) // PIPE_RING, ring, (f0, f1, f2, f3, f4))
        turn(0, b=f[2], c=f[0])
        turn(1, b=f[3], c=f[1])
        turn(2, c=f[2])
        turn(3, c=f[3])

    zero = jnp.int32(0)
    run(nq, (zero, zero), lambda j, i: (j + 1, i + 1), masked=True)

    def next_below_diagonal(j, i):
        wrap = i + 1 >= nq
        return jnp.where(wrap, j + 1, j), jnp.where(wrap, j + 2, i + 1)

    run(nq * (nq - 1) // 2, (zero, zero + 1), next_below_diagonal, masked=False)

    lam = (jnp.exp(jnp.sum(lq1_ref[...] * lk1_ref[...], axis=1, keepdims=True))
           - jnp.exp(jnp.sum(lq2_ref[...] * lk2_ref[...], axis=1, keepdims=True)) + lam_init)
    gain = g_ref[...]

    def finish(i, carry):
        acc = acc_scr[i]
        o1 = acc[:V_DIM, :t] / acc[V_DIM:V_DIM + 1, :t]
        o2 = acc[:V_DIM, t:] / acc[V_DIM:V_DIM + 1, t:]
        o = o1 - lam * o2
        y = ((o * lax.rsqrt(jnp.mean(o * o, axis=0, keepdims=True) + SUBLN_EPS)) * gain) * (1.0 - lam_init)
        o_ref[pl.ds(pl.multiple_of(i * t, t), t), :] = y.T.astype(o_ref.dtype)
        return carry

    lax.fori_loop(0, nq, finish, 0)


def _attention(q, k, v, lam_q1, lam_k1, lam_q2, lam_k2, g_subln, lam_init):
    b, s, d = q.shape
    t = T_ATTN
    assert s % t == 0 and d == N_HEADS * V_DIM
    nq = s // t
    small = lambda w: pl.BlockSpec((1, w), lambda bi, hi: (0, 0))
    head = pl.BlockSpec((None, s, V_DIM), lambda bi, hi: (bi, 0, hi))
    return pl.pallas_call(
        functools.partial(_attn_kernel, t=t, lam_init=lam_init),
        grid=(b, N_HEADS),
        in_specs=[small(HEAD_DIM), small(HEAD_DIM), small(HEAD_DIM), small(HEAD_DIM),
                  pl.BlockSpec((V_DIM, 1), lambda bi, hi: (0, 0)), head, head, head],
        out_specs=head,
        out_shape=jax.ShapeDtypeStruct((b, s, d), BF16),
        scratch_shapes=[
            pltpu.VMEM((nq, V_DIM, 2 * t), BF16),
            pltpu.VMEM((nq, V_AUG, t), BF16),
            pltpu.VMEM((nq, V_AUG, 2 * t), F32),
            pltpu.VMEM((nq, 1, 2 * t), F32),
            pltpu.VMEM((t, 2 * t), F32),
            pltpu.VMEM((PIPE_SLOTS,t, 2 * t), F32),
            pltpu.VMEM((PIPE_SLOTS,t, 2 * t), BF16),
            pltpu.VMEM((PIPE_SLOTS,1, 2 * t), F32),
            pltpu.VMEM((PIPE_SLOTS,1, 2 * t), F32),
        ],
        compiler_params=pltpu.CompilerParams(dimension_semantics=("arbitrary", "arbitrary")),
        name="attn",
    )(lam_q1, lam_k1, lam_q2, lam_k2, g_subln.reshape(V_DIM, 1), q, k, v)


def _lru_kernel(xr_ref, gr_ref, wc_ref, bc_ref, wrg_ref, brg_ref, wig_ref, big_ref, lam_ref, o_ref,
                xprev_ref, hcar_ref, *, t_tile):
    @pl.when(pl.program_id(1) == 0)
    def _():
        xprev_ref[...] = jnp.zeros_like(xprev_ref)
        hcar_ref[...] = jnp.zeros_like(hcar_ref)

    x = xr_ref[...]
    w = x.shape[1]
    prev = xprev_ref[...]
    wc = wc_ref[...]
    row8 = lax.broadcasted_iota(I32, (SUBLANES, w), 0)
    xc = bc_ref[...]
    for j in range(CONV_WIDTH):
        shift = CONV_WIDTH - 1 - j
        if shift == 0:
            xs = x
        else:
            rolled = pltpu.roll(x, shift, 0)
            head = jnp.where(row8 < shift, pltpu.roll(prev, shift, 0), rolled[:SUBLANES])
            xs = jnp.concatenate([head, rolled[SUBLANES:]], axis=0)
        xc = xc + xs * wc[j:j + 1]
    xprev_ref[...] = x[t_tile - SUBLANES:]

    xcb = xc.astype(BF16)
    n_grp = w // MXU_DIM
    pre_r = jnp.concatenate(
        [jnp.dot(xcb[:, g * MXU_DIM:(g + 1) * MXU_DIM], wrg_ref[g], preferred_element_type=F32) for g in range(n_grp)],
        axis=1) + brg_ref[...]
    pre_i = jnp.concatenate(
        [jnp.dot(xcb[:, g * MXU_DIM:(g + 1) * MXU_DIM], wig_ref[g], preferred_element_type=F32) for g in range(n_grp)],
        axis=1) + big_ref[...]
    r = jax.nn.sigmoid(pre_r)
    ig = jax.nn.sigmoid(pre_i)
    neg_lam = -lam_ref[...]
    softplus = jnp.maximum(neg_lam, 0.0) + jnp.log1p(jnp.exp(-jnp.abs(neg_lam)))
    log_a = (-LRU_C) * r * softplus
    a = jnp.exp(log_a)
    th = jnp.tanh(log_a)
    u = jnp.sqrt((-2.0 * th) / (1.0 - th)) * (ig * xc)

    rows = lax.broadcasted_iota(I32, x.shape, 0)
    dist = 1
    while dist < t_tile:
        keep = rows >= dist
        u = jnp.where(keep, a * pltpu.roll(u, dist, 0) + u, u)
        a = jnp.where(keep, a * pltpu.roll(a, dist, 0), a)
        dist *= 2
    h = a * hcar_ref[...] + u
    hcar_ref[...] = h[t_tile - 1:]
    o_ref[...] = (h * jax.nn.gelu(gr_ref[...])).astype(o_ref.dtype)


def _block_diag_groups(wblk):
    nb, bw, _ = wblk.shape
    per = MXU_DIM // bw
    g = nb // per
    eye = jnp.eye(per, dtype=wblk.dtype)
    w5 = wblk.reshape(g, per, bw, bw)
    return jnp.einsum("gawv,ab->gawbv", w5, eye).reshape(g, MXU_DIM, MXU_DIM).astype(BF16)


def _lru(xr, gr, w_conv, b_conv, w_rg, b_rg, w_ig, b_ig, lru_lambda, b, s):
    n, w = xr.shape
    t = T_LRU
    n_t = s // t
    row_spec = pl.BlockSpec((t, w), lambda bi, ti: (bi * n_t + ti, 0))
    vec = lambda r: pl.BlockSpec((r, w), lambda bi, ti: (0, 0))
    blk = pl.BlockSpec((w // MXU_DIM, MXU_DIM, MXU_DIM), lambda bi, ti: (0, 0, 0))
    return pl.pallas_call(
        functools.partial(_lru_kernel, t_tile=t),
        grid=(b, n_t),
        in_specs=[row_spec, row_spec, vec(CONV_WIDTH), vec(1), blk, vec(1), blk, vec(1), vec(1)],
        out_specs=row_spec,
        out_shape=jax.ShapeDtypeStruct((n, w), BF16),
        scratch_shapes=[pltpu.VMEM((SUBLANES, w), F32), pltpu.VMEM((1, w), F32)],
        compiler_params=pltpu.CompilerParams(dimension_semantics=("arbitrary", "arbitrary")),
        name="lru",
    )(xr, gr, w_conv, b_conv.reshape(1, w), _block_diag_groups(w_rg), b_rg.reshape(1, w),
      _block_diag_groups(w_ig), b_ig.reshape(1, w), lru_lambda.reshape(1, w))


def _split_bf16(x):
    hi = x.astype(BF16)
    return hi, (x - hi.astype(F32)).astype(BF16)


def _to_token_tiles(ref, x):
    tm, d = x.shape
    rpt = d // LANES
    for c in range(rpt):
        ref[pl.ds(c, tm, stride=rpt), :] = x[:, c * LANES:(c + 1) * LANES]


def _from_token_tiles(ref, tm, rpt):
    return jnp.concatenate([ref[pl.ds(c, tm, stride=rpt), :] for c in range(rpt)], axis=1)


def _merge_kernel(o_ref, hg_ref, ga_ref, gb_ref, x_ref, wa_ref, wl_ref, wo_ref, gm_ref, wrh_ref, wrl_ref, br_ref,
                  x1_ref, h2_ref, lg_ref):
    attn_br = jnp.dot(o_ref[...], wa_ref[...], preferred_element_type=F32)
    lru_br = jnp.dot(hg_ref[...], wl_ref[...], preferred_element_type=F32)
    mixed = jax.nn.sigmoid(ga_ref[...]) * attn_br + jax.nn.sigmoid(gb_ref[...]) * lru_br
    x1 = x_ref[...] + jnp.dot(mixed.astype(BF16), wo_ref[...], preferred_element_type=F32)
    x1_ref[...] = x1
    h2 = _rmsnorm(x1, gm_ref[...], NORM_EPS)
    _to_token_tiles(h2_ref, h2)
    hi, lo = _split_bf16(h2)
    wrh = wrh_ref[...]
    lg_ref[...] = (jnp.dot(hi, wrh, preferred_element_type=F32) + jnp.dot(lo, wrh, preferred_element_type=F32)
                   + jnp.dot(hi, wrl_ref[...], preferred_element_type=F32)) + br_ref[...]


def _merge(o, hg, ga, gb, x2, wa, wl, wo, g_moe, w_router, b_router):
    n, d = x2.shape
    tm = TM_MERGE
    rpt = d // LANES
    row = pl.BlockSpec((tm, d), lambda i: (i, 0))
    full = lambda r, c: pl.BlockSpec((r, c), lambda i: (0, 0))
    wr_hi, wr_lo = _split_bf16(w_router)
    return pl.pallas_call(
        _merge_kernel,
        grid=(n // tm,),
        in_specs=[row, row, row, row, row, full(d, d), full(d, d), full(d, d), full(1, d),
                  full(d, LANES), full(d, LANES), full(1, LANES)],
        out_specs=[row, pl.BlockSpec((tm * rpt, LANES), lambda i: (i, 0)), pl.BlockSpec((tm, LANES), lambda i: (i, 0))],
        out_shape=[jax.ShapeDtypeStruct((n, d), F32), jax.ShapeDtypeStruct((n * rpt, LANES), F32),
                   jax.ShapeDtypeStruct((n, LANES), F32)],
        compiler_params=pltpu.CompilerParams(dimension_semantics=("arbitrary",)),
        name="merge",
    )(o, hg, ga, gb, x2, wa, wl, wo, g_moe, wr_hi, wr_lo, b_router)


def _lane_pick(x, lane, idx):
    return jnp.sum(jnp.where(lane == idx, x, jnp.zeros_like(x)), axis=1, keepdims=True)


def _route_kernel(lg_ref, pos_ref, wts_ref, offs_ref, *, n, t_tile):
    shape = (t_tile, LANES)
    lane = lax.broadcasted_iota(I32, shape, 1)
    neg_inf = jnp.float32(-jnp.inf)
    tri = (lax.broadcasted_iota(I32, (t_tile, t_tile), 0) > lax.broadcasted_iota(I32, (t_tile, t_tile), 1)).astype(BF16)

    def phase1(t, cnt):
        rows = pl.ds(pl.multiple_of(t * t_tile, t_tile), t_tile)
        lg = lg_ref[rows, :]
        is_grp = lane < N_GROUPS
        gl = jnp.where(is_grp, lg, neg_inf)
        gmax = jnp.max(gl, axis=1, keepdims=True)
        g_idx = jnp.min(jnp.where(gl == gmax, lane, LANES), axis=1, keepdims=True)
        g_w = 1.0 / jnp.sum(jnp.where(is_grp, jnp.exp(lg - gmax), 0.0), axis=1, keepdims=True)
        lo = N_GROUPS + EXPERTS_PER_GROUP * g_idx
        in_grp = (lane >= lo) & (lane < lo + EXPERTS_PER_GROUP)
        fl = jnp.where(in_grp, lg, neg_inf)
        v1 = jnp.max(fl, axis=1, keepdims=True)
        i1 = jnp.min(jnp.where(in_grp & (fl == v1), lane, LANES), axis=1, keepdims=True)
        rest = in_grp & (lane != i1)
        fl2 = jnp.where(rest, lg, neg_inf)
        v2 = jnp.max(fl2, axis=1, keepdims=True)
        i2 = jnp.min(jnp.where(rest & (fl2 == v2), lane, LANES), axis=1, keepdims=True)
        t2 = jnp.exp(v2 - v1)
        den = 1.0 + t2
        w1 = g_w * (1.0 / den)
        w2 = g_w * (t2 / den)
        e1 = i1 - N_GROUPS
        e2 = i2 - N_GROUPS
        onehot = ((lane == e1) | (lane == e2)).astype(F32)
        before = jnp.dot(tri, onehot.astype(BF16), preferred_element_type=F32) + cnt
        rank1 = _lane_pick(before, lane, e1)
        rank2 = _lane_pick(before, lane, e2)
        pos_ref[rows, :] = jnp.where(lane == 0, e1, jnp.where(lane == 1, e2, jnp.where(
            lane == 2, rank1.astype(I32), jnp.where(lane == 3, rank2.astype(I32), 0))))
        wts_ref[rows, :] = jnp.where(lane == 0, w1, jnp.where(lane == 1, w2, 0.0))
        return cnt + jnp.sum(onehot, axis=0, keepdims=True)

    cnt = lax.fori_loop(0, n // t_tile, phase1, jnp.zeros((1, LANES), F32))

    lane8 = lax.broadcasted_iota(I32, (SUBLANES, LANES), 1)
    incl = jnp.broadcast_to(cnt, (SUBLANES, LANES))
    dist = 1
    while dist < LANES:
        incl = incl + jnp.where(lane8 >= dist, pltpu.roll(incl, dist, 1), 0.0)
        dist *= 2
    offs = incl - cnt
    offs_ref[...] = offs.astype(I32)
    offs_row = offs[0:1]

    def phase2(t, carry):
        rows = pl.ds(pl.multiple_of(t * t_tile, t_tile), t_tile)
        info = pos_ref[rows, :]
        e1 = _lane_pick(info, lane, 0)
        e2 = _lane_pick(info, lane, 1)
        r1 = _lane_pick(info, lane, 2)
        r2 = _lane_pick(info, lane, 3)
        offs_b = jnp.broadcast_to(offs_row, shape).astype(I32)
        p1 = r1 + _lane_pick(offs_b, lane, e1)
        p2 = r2 + _lane_pick(offs_b, lane, e2)
        pos_ref[rows, :] = jnp.where(lane == 0, p1, jnp.where(lane == 1, p2, 0))
        return carry

    lax.fori_loop(0, n // t_tile, phase2, 0)


def _route(logits):
    n = logits.shape[0]
    return pl.pallas_call(
        functools.partial(_route_kernel, n=n, t_tile=T_ROUTE),
        out_shape=[jax.ShapeDtypeStruct((n, LANES), I32), jax.ShapeDtypeStruct((n, LANES), F32),
                   jax.ShapeDtypeStruct((SUBLANES, LANES), I32)],
        name="route",
    )(logits)


def _token_rows(idx, rpt):
    return pl.ds(pl.multiple_of(idx * rpt, rpt), rpt)


def _dispatch_kernel(p1_ref, p2_ref, h_ref, hs_ref, sem, *, tm, rpt):
    def issue(r, carry):
        src = h_ref.at[_token_rows(r, rpt)]
        pltpu.make_async_copy(src, hs_ref.at[_token_rows(p1_ref[r], rpt)], sem).start()
        pltpu.make_async_copy(src, hs_ref.at[_token_rows(p2_ref[r], rpt)], sem).start()
        return carry

    lax.fori_loop(0, tm, issue, 0, unroll=DMA_ISSUE_UNROLL)
    whole = pltpu.make_async_copy(h_ref, hs_ref.at[pl.ds(0, tm * rpt)], sem)
    whole.wait()
    whole.wait()


def _dispatch(pos1, pos2, h2t, rpt):
    n = h2t.shape[0] // rpt
    tm = TM_MOE
    idx = pl.BlockSpec((tm,), lambda i: (i,), memory_space=pltpu.SMEM)
    return pl.pallas_call(
        functools.partial(_dispatch_kernel, tm=tm, rpt=rpt),
        grid=(n // tm,),
        in_specs=[idx, idx, pl.BlockSpec((tm * rpt, LANES), lambda i: (i, 0))],
        out_specs=pl.BlockSpec(memory_space=pl.ANY),
        out_shape=jax.ShapeDtypeStruct((2 * n * rpt, LANES), F32),
        scratch_shapes=[pltpu.SemaphoreType.DMA(())],
        compiler_params=pltpu.CompilerParams(dimension_semantics=("arbitrary",), has_side_effects=True),
        name="dispatch",
    )(pos1, pos2, h2t)


def _expert_kernel(it_tile, it_exp, it_first, it_valid, it_new, offs, hs_ref, wg_ref, wu_ref, wd_ref, ys_ref,
                   wg_bf, wu_bf, wd_bf, *, tm, rpt):
    w = pl.program_id(0)

    @pl.when(it_valid[w] == 1)
    def _():
        @pl.when(it_new[w] == 1)
        def _():
            wg_bf[...] = wg_ref[...].astype(BF16)
            wu_bf[...] = wu_ref[...].astype(BF16)
            wd_bf[...] = wd_ref[...].astype(BF16)

        e = it_exp[w]
        base = it_tile[w] * tm
        lo = offs[e] - base
        hi = offs[e + 1] - base
        h = _from_token_tiles(hs_ref, tm, rpt).astype(BF16)
        gate = jnp.dot(h, wg_bf[...], preferred_element_type=F32)
        up = jnp.dot(h, wu_bf[...], preferred_element_type=F32)
        hid = (jax.nn.silu(gate) * up).astype(BF16)
        y = jnp.dot(hid, wd_bf[...], preferred_element_type=F32)
        row = lax.broadcasted_iota(I32, (tm, 1), 0)
        mine = (row >= lo) & (row < hi)

        @pl.when(it_first[w] == 1)
        def _():
            _to_token_tiles(ys_ref, jnp.where(mine, y, 0.0))

        @pl.when(it_first[w] == 0)
        def _():
            _to_token_tiles(ys_ref, jnp.where(mine, y, _from_token_tiles(ys_ref, tm, rpt)))


def _work_items(offs, n_rows, tm):
    n_items = n_rows // tm + N_EXPERTS
    starts, ends = offs[:N_EXPERTS], offs[1:N_EXPERTS + 1]
    first_tile = starts // tm
    n_e = jnp.where(ends > starts, (ends - 1) // tm - first_tile + 1, 0)
    item_end = jnp.cumsum(n_e)
    item_start = item_end - n_e
    total = item_end[-1]
    w = jnp.arange(n_items, dtype=I32)
    wc = jnp.minimum(w, total - 1)
    exp = jnp.sum(item_end[None, :] <= wc[:, None], axis=1).astype(I32)
    tile = (first_tile[exp] + wc - item_start[exp]).astype(I32)
    valid = (w < total).astype(I32)
    prev_tile = jnp.concatenate([jnp.full((1,), -1, I32), tile[:-1]])
    prev_exp = jnp.concatenate([jnp.full((1,), -1, I32), exp[:-1]])
    first = (tile != prev_tile).astype(I32)
    new = (exp != prev_exp).astype(I32)
    return tile, exp, first, valid, new


def _experts(hs, offs, w_e_gate, w_e_up, w_e_down, rpt):
    n_rows = hs.shape[0] // rpt
    d, ff = w_e_gate.shape[-2:]
    tm = TM_MOE
    tile, exp, first, valid, new = _work_items(offs, n_rows, tm)
    rows = pl.BlockSpec((tm * rpt, LANES), lambda w, t, e, *_: (t[w], 0))
    grid_spec = pltpu.PrefetchScalarGridSpec(
        num_scalar_prefetch=6,
        grid=(tile.shape[0],),
        in_specs=[
            rows,
            pl.BlockSpec((None, d, ff), lambda w, t, e, *_: (e[w], 0, 0)),
            pl.BlockSpec((None, d, ff), lambda w, t, e, *_: (e[w], 0, 0)),
            pl.BlockSpec((None, ff, d), lambda w, t, e, *_: (e[w], 0, 0)),
        ],
        out_specs=rows,
        scratch_shapes=[pltpu.VMEM((d, ff), BF16), pltpu.VMEM((d, ff), BF16), pltpu.VMEM((ff, d), BF16)],
    )
    return pl.pallas_call(
        functools.partial(_expert_kernel, tm=tm, rpt=rpt),
        grid_spec=grid_spec,
        out_shape=jax.ShapeDtypeStruct(hs.shape, F32),
        compiler_params=pltpu.CompilerParams(dimension_semantics=("arbitrary",)),
        name="experts",
    )(tile, exp, first, valid, new, offs, hs, w_e_gate, w_e_up, w_e_down)


def _combine_kernel(p1_ref, p2_ref, p1n_ref, p2n_ref, wts_ref, x1_ref, p_ref, gp_ref, wpg_ref, wpp_ref, gf_ref,
                    ys_ref, out_ref, ybuf, sems, *, tm, rpt, n_tiles):
    i = pl.program_id(0)
    slot = i % 2

    def gather(a_ref, b_ref, dst):
        def issue(r, carry):
            rows = _token_rows(r, rpt)
            pltpu.make_async_copy(ys_ref.at[_token_rows(a_ref[r], rpt)], ybuf.at[dst, 0, rows], sems.at[dst]).start()
            pltpu.make_async_copy(ys_ref.at[_token_rows(b_ref[r], rpt)], ybuf.at[dst, 1, rows], sems.at[dst]).start()
            return carry

        lax.fori_loop(0, tm, issue, 0, unroll=DMA_ISSUE_UNROLL)

    @pl.when(i == 0)
    def _():
        gather(p1_ref, p2_ref, 0)

    @pl.when(i + 1 < n_tiles)
    def _():
        gather(p1n_ref, p2n_ref, 1 - slot)

    for half in range(2):
        pltpu.make_async_copy(ys_ref.at[pl.ds(0, tm * rpt)], ybuf.at[slot, half], sems.at[slot]).wait()

    wts = wts_ref[...]
    y1 = _from_token_tiles(ybuf.at[slot, 0], tm, rpt)
    y2 = _from_token_tiles(ybuf.at[slot, 1], tm, rpt)
    x2 = x1_ref[...] + (wts[:, 0:1] * y1 + wts[:, 1:2] * y2)
    hp = _rmsnorm(x2, gp_ref[...], NORM_EPS).astype(BF16)
    gate = jax.nn.sigmoid(jnp.dot(hp, wpg_ref[...], preferred_element_type=F32))
    proj = jnp.dot(p_ref[...].astype(BF16), wpp_ref[...], preferred_element_type=F32)
    x3 = x2 + gate * proj
    out_ref[...] = _rmsnorm(x3, gf_ref[...], NORM_EPS)


def _combine(pos1, pos2, wts, x1, p2d, g_ple, wpg, wpp, g_final, ys, rpt):
    n, d = x1.shape
    pd = p2d.shape[1]
    tm = TM_MOE
    n_tiles = n // tm
    idx = pl.BlockSpec((tm,), lambda i: (i,), memory_space=pltpu.SMEM)
    idx_next = pl.BlockSpec((tm,), lambda i: (jnp.minimum(i + 1, n_tiles - 1),), memory_space=pltpu.SMEM)
    row = pl.BlockSpec((tm, d), lambda i: (i, 0))
    full = lambda r, c: pl.BlockSpec((r, c), lambda i: (0, 0))
    return pl.pallas_call(
        functools.partial(_combine_kernel, tm=tm, rpt=rpt, n_tiles=n_tiles),
        grid=(n_tiles,),
        in_specs=[idx, idx, idx_next, idx_next, pl.BlockSpec((tm, LANES), lambda i: (i, 0)), row,
                  pl.BlockSpec((tm, pd), lambda i: (i, 0)), full(1, d), full(d, d), full(pd, d), full(1, d),
                  pl.BlockSpec(memory_space=pl.ANY)],
        out_specs=row,
        out_shape=jax.ShapeDtypeStruct((n, d), F32),
        scratch_shapes=[pltpu.VMEM((2, 2, tm * rpt, LANES), F32), pltpu.SemaphoreType.DMA((2,))],
        compiler_params=pltpu.CompilerParams(dimension_semantics=("arbitrary",)),
        name="combine",
    )(pos1, pos2, pos1, pos2, wts, x1, p2d, g_ple, wpg, wpp, g_final, ys)


def _layer(i, x2, p2d, b, s, g_mix, w_in, lam_q1, lam_k1, lam_q2, lam_k2, g_subln, w_conv, b_conv, w_rg, b_rg,
           w_ig, b_ig, lru_lambda, w_attn_br, w_lru_br, w_out, g_moe, w_rt_group, b_rt_group, w_rt_expert,
           b_rt_expert, w_e_gate, w_e_up, w_e_down, g_ple, w_ple_gate, w_ple_proj):
    n, d = x2.shape
    lam_init = 0.8 - 0.6 * math.exp(-0.3 * i)
    row = lambda a: a.reshape(1, -1)

    q, k, v, xr, gr, ga, gb = _inproj(x2, row(g_mix), w_in.astype(BF16))
    o = _attention(q.reshape(b, s, d), k.reshape(b, s, d), v.reshape(b, s, d),
                   row(lam_q1), row(lam_k1), row(lam_q2), row(lam_k2), row(g_subln), lam_init).reshape(n, d)
    hg = _lru(xr, gr, w_conv, b_conv, w_rg, b_rg, w_ig, b_ig, lru_lambda, b, s)

    pad = LANES - N_GROUPS - N_EXPERTS
    w_router = jnp.concatenate(
        [w_rt_group, w_rt_expert.transpose(1, 0, 2).reshape(d, N_EXPERTS), jnp.zeros((d, pad), F32)], axis=1)
    b_router = jnp.concatenate([b_rt_group, b_rt_expert.reshape(N_EXPERTS), jnp.zeros((pad,), F32)]).reshape(1, LANES)
    x1, h2, logits = _merge(o, hg, ga, gb, x2, w_attn_br.astype(BF16), w_lru_br.astype(BF16), w_out.astype(BF16),
                            row(g_moe), w_router, b_router)

    pos, wts, offs = _route(logits)
    pos1, pos2 = pos[:, 0], pos[:, 1]
    rpt = d // LANES
    hs = _dispatch(pos1, pos2, h2, rpt)
    ys = _experts(hs, offs[0, :N_EXPERTS + 1], w_e_gate, w_e_up, w_e_down, rpt)
    return pos1, pos2, wts, x1, ys


def kernel(x, p, g_mix, w_in, lam_q1, lam_k1, lam_q2, lam_k2, g_subln, w_conv, b_conv, w_rg, b_rg, w_ig, b_ig, lru_lambda, w_attn_br, w_lru_br, w_out, g_moe, w_rt_group, b_rt_group, w_rt_expert, b_rt_expert, w_e_gate, w_e_up, w_e_down, g_ple, w_ple_gate, w_ple_proj, g_final):
    b, s, d = x.shape
    depth = p.shape[0]
    assert depth == 1, "the final RMSNorm is fused into the last layer's combine step; one layer supported"
    n = b * s
    x2 = x.reshape(n, d)
    i = 0
    p2d = p[i].reshape(n, -1)
    pos1, pos2, wts, x1, ys = _layer(
        i, x2, p2d, b, s, g_mix[i], w_in[i], lam_q1[i], lam_k1[i], lam_q2[i], lam_k2[i], g_subln[i], w_conv[i],
        b_conv[i], w_rg[i], b_rg[i], w_ig[i], b_ig[i], lru_lambda[i], w_attn_br[i], w_lru_br[i], w_out[i],
        g_moe[i], w_rt_group[i], b_rt_group[i], w_rt_expert[i], b_rt_expert[i], w_e_gate[i], w_e_up[i],
        w_e_down[i], g_ple[i], w_ple_gate[i], w_ple_proj[i])
    out = _combine(pos1, pos2, wts, x1, p2d, g_ple[i].reshape(1, d), w_ple_gate[i].astype(BF16),
                   w_ple_proj[i].astype(BF16), g_final.reshape(1, d), ys, d // LANES)
    return out.reshape(b, s, d)
```

```python
import functools
import math

import jax
import jax.numpy as jnp
from jax import lax
from jax.experimental import pallas as pl
from jax.experimental.pallas import tpu as pltpu

F32 = jnp.float32
BF16 = jnp.bfloat16
I32 = jnp.int32

N_HEADS = 8
HEAD_DIM = 64
V_DIM = 2 * HEAD_DIM
SUBLN_EPS = 1e-5
NORM_EPS = 1e-6
LRU_BLOCKS = 16
CONV_WIDTH = 4
LRU_C = 8.0
N_GROUPS = 4
EXPERTS_PER_GROUP = 8
N_EXPERTS = N_GROUPS * EXPERTS_PER_GROUP
N_IN_PARTS = 7
LOG2_E = 1.4426950408889634
Q_SCALE = HEAD_DIM ** -0.5 * LOG2_E

LANES = 128
SUBLANES = 8
MXU_DIM = 256

TM_INPROJ = 256
T_ATTN = 256
T_LRU = 256
TM_MERGE = 512
T_ROUTE = 512
TM_MOE = 256
TM_EXPERT = 256
DMA_ISSUE_UNROLL = 8
MASK_NEG = -1e30


def _rmsnorm(x, g, eps):
    return (x * lax.rsqrt(jnp.mean(x * x, axis=-1, keepdims=True) + eps)) * g


def _inproj_kernel(x_ref, g_ref, w_ref, *out_refs, d):
    h = _rmsnorm(x_ref[...], g_ref[...], NORM_EPS).astype(BF16)
    for c, o_ref in enumerate(out_refs):
        z = jnp.dot(h, w_ref[:, c * d:(c + 1) * d], preferred_element_type=F32)
        if c == 0:
            z = z * Q_SCALE
        o_ref[...] = z.astype(o_ref.dtype)


def _inproj(x2, g_mix, w_in_bf):
    n, d = x2.shape
    tm = TM_INPROJ
    out_dtypes = (BF16, BF16, BF16, F32, F32, F32, F32)
    row_spec = pl.BlockSpec((tm, d), lambda i: (i, 0))
    return pl.pallas_call(
        functools.partial(_inproj_kernel, d=d),
        grid=(n // tm,),
        in_specs=[
            row_spec,
            pl.BlockSpec((1, d), lambda i: (0, 0)),
            pl.BlockSpec((d, N_IN_PARTS * d), lambda i: (0, 0), pipeline_mode=pl.Buffered(1)),
        ],
        out_specs=[row_spec] * N_IN_PARTS,
        out_shape=[jax.ShapeDtypeStruct((n, d), dt) for dt in out_dtypes],
        compiler_params=pltpu.CompilerParams(dimension_semantics=("arbitrary",)),
        name="inproj",
    )(x2, g_mix, w_in_bf)


V_AUG = V_DIM + 16
PIPE_GROUP = 4
PIPE_RING = 4
PIPE_SLOTS = PIPE_GROUP * PIPE_RING


def _attn_kernel(lq1_ref, lk1_ref, lq2_ref, lk2_ref, g_ref, q_ref, k_ref, v_ref, o_ref,
                 qt_scr, vt_scr, acc_scr, m_scr, bias_scr, s_scr, p_scr, cmax_scr, alpha_scr, *, t, lam_init):
    nq = q_ref.shape[0] // t
    w = 2 * t

    feat = lax.broadcasted_iota(I32, (t, V_DIM), 1)
    ones_pad = jnp.where(lax.broadcasted_iota(I32, (V_AUG - V_DIM, t), 0) == 0, 1.0, 0.0).astype(F32)

    def prep(i, carry):
        rows = pl.ds(pl.multiple_of(i * t, t), t)
        q = q_ref[rows, :].astype(F32)
        q1t = jnp.where(feat < HEAD_DIM, q, 0.0).T
        q2t = jnp.where(feat >= HEAD_DIM, q, 0.0).T
        qt_scr[i] = jnp.concatenate([q1t, q2t], axis=1).astype(BF16)
        vt = v_ref[rows, :].astype(F32).T
        vt_scr[i] = jnp.concatenate([vt, ones_pad], axis=0).astype(BF16)
        m_scr[i] = jnp.full((1, w), MASK_NEG, F32)
        acc_scr[i] = jnp.zeros((V_AUG, w), F32)
        return carry

    lax.fori_loop(0, nq, prep, 0)
    key = lax.broadcasted_iota(I32, (t, w), 0)
    qry = lax.broadcasted_iota(I32, (t, w), 1) & (t - 1)
    bias_scr[...] = jnp.where(key <= qry, 0.0, MASK_NEG)

    def stage_a(j, i, slot, masked):
        kj = k_ref[pl.ds(pl.multiple_of(j * t, t), t), :]
        s = jnp.dot(kj, qt_scr[i], preferred_element_type=F32)
        if masked:
            s = s + bias_scr[...]
        s_scr[slot] = s
        cmax_scr[slot] = jnp.max(s, axis=0, keepdims=True)

    def stage_b(i, slot):
        m_old = m_scr[i]
        m_new = jnp.maximum(m_old, cmax_scr[slot])
        m_scr[i] = m_new
        p_scr[slot] = jnp.exp2(s_scr[slot] - m_new).astype(BF16)
        alpha_scr[slot] = jnp.exp2(m_old - m_new)

    def stage_c(j, i, slot):
        acc_scr[i] = alpha_scr[slot] * acc_scr[i] + jnp.dot(vt_scr[j], p_scr[slot], preferred_element_type=F32)

    def run(n_steps, first, nxt, masked):
        assert n_steps % (PIPE_RING * PIPE_GROUP) == 0
        n_groups = n_steps // PIPE_GROUP

        def steps_of(st):
            out = []
            for _ in range(PIPE_GROUP):
                out.append(st)
                st = nxt(*st)
            return out, st

        def turn(t_mod, a=None, b=None, c=None):
            base_a = (t_mod % PIPE_RING) * PIPE_GROUP
            base_b = ((t_mod - 2) % PIPE_RING) * PIPE_GROUP
            a_steps, following = steps_of(a) if a is not None else (None, None)
            b_steps = steps_of(b)[0] if b is not None else None
            c_steps = steps_of(c)[0] if c is not None else None
            for kk in range(PIPE_GROUP):
                if a_steps is not None:
                    stage_a(*a_steps[kk], base_a + kk, masked)
                if b_steps is not None:
                    stage_b(b_steps[kk][1], base_b + kk)
                if c_steps is not None:
                    stage_c(*c_steps[kk], base_a + kk)
            return following

        f0 = first
        f1 = turn(0, a=f0)
        f2 = turn(1, a=f1)
        f3 = turn(2, a=f2, b=f0)
        f4 = turn(3, a=f3, b=f1)

        def ring(_, f):
            f = list(f)
            for r in range(PIPE_RING):
                f.append(turn(r, a=f[-1], b=f[-3], c=f[-5]))
            return tuple(f[-5:])

        f = lax.fori_loop(0, (n_groups - PIPE_RING) // PIPE_RING, ring, (f0, f1, f2, f3, f4))
        turn(0, b=f[2], c=f[0])
        turn(1, b=f[3], c=f[1])
        turn(2, c=f[2])
        turn(3, c=f[3])

    zero = jnp.int32(0)
    run(nq, (zero, zero), lambda j, i: (j + 1, i + 1), masked=True)

    def next_below_diagonal(j, i):
        wrap = i + 1 >= nq
        return jnp.where(wrap, j + 1, j), jnp.where(wrap, j + 2, i + 1)

    run(nq * (nq - 1) // 2, (zero, zero + 1), next_below_diagonal, masked=False)

    lam = (jnp.exp(jnp.sum(lq1_ref[...] * lk1_ref[...], axis=1, keepdims=True))
           - jnp.exp(jnp.sum(lq2_ref[...] * lk2_ref[...], axis=1, keepdims=True)) + lam_init)
    gain = g_ref[...]

    def finish(i, carry):
        acc = acc_scr[i]
        o1 = acc[:V_DIM, :t] / acc[V_DIM:V_DIM + 1, :t]
        o2 = acc[:V_DIM, t:] / acc[V_DIM:V_DIM + 1, t:]
        o = o1 - lam * o2
        y = ((o * lax.rsqrt(jnp.mean(o * o, axis=0, keepdims=True) + SUBLN_EPS)) * gain) * (1.0 - lam_init)
        o_ref[pl.ds(pl.multiple_of(i * t, t), t), :] = y.T.astype(o_ref.dtype)
        return carry

    lax.fori_loop(0, nq, finish, 0)


def _attention(q, k, v, lam_q1, lam_k1, lam_q2, lam_k2, g_subln, lam_init):
    b, s, d = q.shape
    t = T_ATTN
    assert s % t == 0 and d == N_HEADS * V_DIM
    nq = s // t
    small = lambda w: pl.BlockSpec((1, w), lambda bi, hi: (0, 0))
    head = pl.BlockSpec((None, s, V_DIM), lambda bi, hi: (bi, 0, hi))
    return pl.pallas_call(
        functools.partial(_attn_kernel, t=t, lam_init=lam_init),
        grid=(b, N_HEADS),
        in_specs=[small(HEAD_DIM), small(HEAD_DIM), small(HEAD_DIM), small(HEAD_DIM),
                  pl.BlockSpec((V_DIM, 1), lambda bi, hi: (0, 0)), head, head, head],
        out_specs=head,
        out_shape=jax.ShapeDtypeStruct((b, s, d), BF16),
        scratch_shapes=[
            pltpu.VMEM((nq, V_DIM, 2 * t), BF16),
            pltpu.VMEM((nq, V_AUG, t), BF16),
            pltpu.VMEM((nq, V_AUG, 2 * t), F32),
            pltpu.VMEM((nq, 1, 2 * t), F32),
            pltpu.VMEM((t, 2 * t), F32),
            pltpu.VMEM((PIPE_SLOTS,t, 2 * t), F32),
            pltpu.VMEM((PIPE_SLOTS,t, 2 * t), BF16),
            pltpu.VMEM((PIPE_SLOTS,1, 2 * t), F32),
            pltpu.VMEM((PIPE_SLOTS,1, 2 * t), F32),
        ],
        compiler_params=pltpu.CompilerParams(dimension_semantics=("arbitrary", "arbitrary")),
        name="attn",
    )(lam_q1, lam_k1, lam_q2, lam_k2, g_subln.reshape(V_DIM, 1), q, k, v)


def _lru_kernel(xr_ref, gr_ref, wc_ref, bc_ref, wrg_ref, brg_ref, wig_ref, big_ref, lam_ref, o_ref,
                xprev_ref, hcar_ref, *, t_tile):
    @pl.when(pl.program_id(1) == 0)
    def _():
        xprev_ref[...] = jnp.zeros_like(xprev_ref)
        hcar_ref[...] = jnp.zeros_like(hcar_ref)

    x = xr_ref[...]
    w = x.shape[1]
    prev = xprev_ref[...]
    wc = wc_ref[...]
    row8 = lax.broadcasted_iota(I32, (SUBLANES, w), 0)
    xc = bc_ref[...]
    for j in range(CONV_WIDTH):
        shift = CONV_WIDTH - 1 - j
        if shift == 0:
            xs = x
        else:
            rolled = pltpu.roll(x, shift, 0)
            head = jnp.where(row8 < shift, pltpu.roll(prev, shift, 0), rolled[:SUBLANES])
            xs = jnp.concatenate([head, rolled[SUBLANES:]], axis=0)
        xc = xc + xs * wc[j:j + 1]
    xprev_ref[...] = x[t_tile - SUBLANES:]

    xcb = xc.astype(BF16)
    n_grp = w // MXU_DIM
    pre_r = jnp.concatenate(
        [jnp.dot(xcb[:, g * MXU_DIM:(g + 1) * MXU_DIM], wrg_ref[g], preferred_element_type=F32) for g in range(n_grp)],
        axis=1) + brg_ref[...]
    pre_i = jnp.concatenate(
        [jnp.dot(xcb[:, g * MXU_DIM:(g + 1) * MXU_DIM], wig_ref[g], preferred_element_type=F32) for g in range(n_grp)],
        axis=1) + big_ref[...]
    r = jax.nn.sigmoid(pre_r)
    ig = jax.nn.sigmoid(pre_i)
    neg_lam = -lam_ref[...]
    softplus = jnp.maximum(neg_lam, 0.0) + jnp.log1p(jnp.exp(-jnp.abs(neg_lam)))
    log_a = (-LRU_C) * r * softplus
    a = jnp.exp(log_a)
    th = jnp.tanh(log_a)
    u = jnp.sqrt((-2.0 * th) / (1.0 - th)) * (ig * xc)

    rows = lax.broadcasted_iota(I32, x.shape, 0)
    dist = 1
    while dist < t_tile:
        keep = rows >= dist
        u = jnp.where(keep, a * pltpu.roll(u, dist, 0) + u, u)
        a = jnp.where(keep, a * pltpu.roll(a, dist, 0), a)
        dist *= 2
    h = a * hcar_ref[...] + u
    hcar_ref[...] = h[t_tile - 1:]
    o_ref[...] = (h * jax.nn.gelu(gr_ref[...])).astype(o_ref.dtype)


def _block_diag_groups(wblk):
    nb, bw, _ = wblk.shape
    per = MXU_DIM // bw
    g = nb // per
    eye = jnp.eye(per, dtype=wblk.dtype)
    w5 = wblk.reshape(g, per, bw, bw)
    return jnp.einsum("gawv,ab->gawbv", w5, eye).reshape(g, MXU_DIM, MXU_DIM).astype(BF16)


def _lru(xr, gr, w_conv, b_conv, w_rg, b_rg, w_ig, b_ig, lru_lambda, b, s):
    n, w = xr.shape
    t = T_LRU
    n_t = s // t
    row_spec = pl.BlockSpec((t, w), lambda bi, ti: (bi * n_t + ti, 0))
    vec = lambda r: pl.BlockSpec((r, w), lambda bi, ti: (0, 0))
    blk = pl.BlockSpec((w // MXU_DIM, MXU_DIM, MXU_DIM), lambda bi, ti: (0, 0, 0))
    return pl.pallas_call(
        functools.partial(_lru_kernel, t_tile=t),
        grid=(b, n_t),
        in_specs=[row_spec, row_spec, vec(CONV_WIDTH), vec(1), blk, vec(1), blk, vec(1), vec(1)],
        out_specs=row_spec,
        out_shape=jax.ShapeDtypeStruct((n, w), BF16),
        scratch_shapes=[pltpu.VMEM((SUBLANES, w), F32), pltpu.VMEM((1, w), F32)],
        compiler_params=pltpu.CompilerParams(dimension_semantics=("arbitrary", "arbitrary")),
        name="lru",
    )(xr, gr, w_conv, b_conv.reshape(1, w), _block_diag_groups(w_rg), b_rg.reshape(1, w),
      _block_diag_groups(w_ig), b_ig.reshape(1, w), lru_lambda.reshape(1, w))


def _split_bf16(x):
    hi = x.astype(BF16)
    return hi, (x - hi.astype(F32)).astype(BF16)


def _to_token_tiles(ref, x):
    tm, d = x.shape
    rpt = d // LANES
    for c in range(rpt):
        ref[pl.ds(c, tm, stride=rpt), :] = x[:, c * LANES:(c + 1) * LANES]


def _from_token_tiles(ref, tm, rpt):
    return jnp.concatenate([ref[pl.ds(c, tm, stride=rpt), :] for c in range(rpt)], axis=1)


def _merge_kernel(o_ref, hg_ref, ga_ref, gb_ref, x_ref, wa_ref, wl_ref, wo_ref, gm_ref, wrh_ref, wrl_ref, br_ref,
                  x1_ref, h2_ref, lg_ref):
    attn_br = jnp.dot(o_ref[...], wa_ref[...], preferred_element_type=F32)
    lru_br = jnp.dot(hg_ref[...], wl_ref[...], preferred_element_type=F32)
    mixed = jax.nn.sigmoid(ga_ref[...]) * attn_br + jax.nn.sigmoid(gb_ref[...]) * lru_br
    x1 = x_ref[...] + jnp.dot(mixed.astype(BF16), wo_ref[...], preferred_element_type=F32)
    x1_ref[...] = x1
    h2 = _rmsnorm(x1, gm_ref[...], NORM_EPS)
    _to_token_tiles(h2_ref, h2)
    hi, lo = _split_bf16(h2)
    wrh = wrh_ref[...]
    lg_ref[...] = (jnp.dot(hi, wrh, preferred_element_type=F32) + jnp.dot(lo, wrh, preferred_element_type=F32)
                   + jnp.dot(hi, wrl_ref[...], preferred_element_type=F32)) + br_ref[...]


def _merge(o, hg, ga, gb, x2, wa, wl, wo, g_moe, w_router, b_router):
    n, d = x2.shape
    tm = TM_MERGE
    rpt = d // LANES
    row = pl.BlockSpec((tm, d), lambda i: (i, 0))
    full = lambda r, c: pl.BlockSpec((r, c), lambda i: (0, 0))
    wr_hi, wr_lo = _split_bf16(w_router)
    return pl.pallas_call(
        _merge_kernel,
        grid=(n // tm,),
        in_specs=[row, row, row, row, row, full(d, d), full(d, d), full(d, d), full(1, d),
                  full(d, LANES), full(d, LANES), full(1, LANES)],
        out_specs=[row, pl.BlockSpec((tm * rpt, LANES), lambda i: (i, 0)), pl.BlockSpec((tm, LANES), lambda i: (i, 0))],
        out_shape=[jax.ShapeDtypeStruct((n, d), F32), jax.ShapeDtypeStruct((n * rpt, LANES), F32),
                   jax.ShapeDtypeStruct((n, LANES), F32)],
        compiler_params=pltpu.CompilerParams(dimension_semantics=("arbitrary",)),
        name="merge",
    )(o, hg, ga, gb, x2, wa, wl, wo, g_moe, wr_hi, wr_lo, b_router)


def _lane_pick(x, lane, idx):
    return jnp.sum(jnp.where(lane == idx, x, jnp.zeros_like(x)), axis=1, keepdims=True)


def _route_kernel(lg_ref, pos_ref, wts_ref, offs_ref, *, n, t_tile, seg_tile):
    shape = (t_tile, LANES)
    lane = lax.broadcasted_iota(I32, shape, 1)
    neg_inf = jnp.float32(-jnp.inf)
    tri = (lax.broadcasted_iota(I32, (t_tile, t_tile), 0) > lax.broadcasted_iota(I32, (t_tile, t_tile), 1)).astype(BF16)

    def phase1(t, cnt):
        rows = pl.ds(pl.multiple_of(t * t_tile, t_tile), t_tile)
        lg = lg_ref[rows, :]
        is_grp = lane < N_GROUPS
        gl = jnp.where(is_grp, lg, neg_inf)
        gmax = jnp.max(gl, axis=1, keepdims=True)
        g_idx = jnp.min(jnp.where(gl == gmax, lane, LANES), axis=1, keepdims=True)
        g_w = 1.0 / jnp.sum(jnp.where(is_grp, jnp.exp(lg - gmax), 0.0), axis=1, keepdims=True)
        lo = N_GROUPS + EXPERTS_PER_GROUP * g_idx
        in_grp = (lane >= lo) & (lane < lo + EXPERTS_PER_GROUP)
        fl = jnp.where(in_grp, lg, neg_inf)
        v1 = jnp.max(fl, axis=1, keepdims=True)
        i1 = jnp.min(jnp.where(in_grp & (fl == v1), lane, LANES), axis=1, keepdims=True)
        rest = in_grp & (lane != i1)
        fl2 = jnp.where(rest, lg, neg_inf)
        v2 = jnp.max(fl2, axis=1, keepdims=True)
        i2 = jnp.min(jnp.where(rest & (fl2 == v2), lane, LANES), axis=1, keepdims=True)
        t2 = jnp.exp(v2 - v1)
        den = 1.0 + t2
        w1 = g_w * (1.0 / den)
        w2 = g_w * (t2 / den)
        e1 = i1 - N_GROUPS
        e2 = i2 - N_GROUPS
        onehot = ((lane == e1) | (lane == e2)).astype(F32)
        before = jnp.dot(tri, onehot.astype(BF16), preferred_element_type=F32) + cnt
        rank1 = _lane_pick(before, lane, e1)
        rank2 = _lane_pick(before, lane, e2)
        pos_ref[rows, :] = jnp.where(lane == 0, e1, jnp.where(lane == 1, e2, jnp.where(
            lane == 2, rank1.astype(I32), jnp.where(lane == 3, rank2.astype(I32), 0))))
        wts_ref[rows, :] = jnp.where(lane == 0, w1, jnp.where(lane == 1, w2, 0.0))
        return cnt + jnp.sum(onehot, axis=0, keepdims=True)

    cnt = lax.fori_loop(0, n // t_tile, phase1, jnp.zeros((1, LANES), F32))

    lane8 = lax.broadcasted_iota(I32, (SUBLANES, LANES), 1)
    row8 = lax.broadcasted_iota(I32, (SUBLANES, LANES), 0)
    cnt8 = jnp.broadcast_to(cnt, (SUBLANES, LANES))
    padded = jnp.ceil(cnt8 * (1.0 / seg_tile)) * seg_tile
    incl = padded
    dist = 1
    while dist < LANES:
        incl = incl + jnp.where(lane8 >= dist, pltpu.roll(incl, dist, 1), 0.0)
        dist *= 2
    offs = incl - padded
    offs_ref[...] = jnp.where(row8 == 0, offs, jnp.where(row8 == 1, cnt8, 0.0)).astype(I32)
    offs_row = offs[0:1]

    def phase2(t, carry):
        rows = pl.ds(pl.multiple_of(t * t_tile, t_tile), t_tile)
        info = pos_ref[rows, :]
        e1 = _lane_pick(info, lane, 0)
        e2 = _lane_pick(info, lane, 1)
        r1 = _lane_pick(info, lane, 2)
        r2 = _lane_pick(info, lane, 3)
        offs_b = jnp.broadcast_to(offs_row, shape).astype(I32)
        p1 = r1 + _lane_pick(offs_b, lane, e1)
        p2 = r2 + _lane_pick(offs_b, lane, e2)
        pos_ref[rows, :] = jnp.where(lane == 0, p1, jnp.where(lane == 1, p2, 0))
        return carry

    lax.fori_loop(0, n // t_tile, phase2, 0)


def _route(logits):
    n = logits.shape[0]
    return pl.pallas_call(
        functools.partial(_route_kernel, n=n, t_tile=T_ROUTE, seg_tile=TM_EXPERT),
        out_shape=[jax.ShapeDtypeStruct((n, LANES), I32), jax.ShapeDtypeStruct((n, LANES), F32),
                   jax.ShapeDtypeStruct((SUBLANES, LANES), I32)],
        name="route",
    )(logits)


def _token_rows(idx, rpt):
    return pl.ds(pl.multiple_of(idx * rpt, rpt), rpt)


def _dispatch_kernel(seg_ref, cnt_ref, p1_ref, p2_ref, h_ref, hs_ref, zero_scr, sem, *, tm, rpt, n_tiles,
                     seg_tile, cap_tiles):
    def issue(r, carry):
        src = h_ref.at[_token_rows(r, rpt)]
        pltpu.make_async_copy(src, hs_ref.at[_token_rows(p1_ref[r], rpt)], sem).start(priority=0)
        pltpu.make_async_copy(src, hs_ref.at[_token_rows(p2_ref[r], rpt)], sem).start(priority=1)
        return carry

    lax.fori_loop(0, tm, issue, 0, unroll=DMA_ISSUE_UNROLL)
    whole = pltpu.make_async_copy(h_ref, hs_ref.at[pl.ds(0, tm * rpt)], sem)
    whole.wait()
    whole.wait()

    @pl.when(pl.program_id(0) == n_tiles - 1)
    def _():
        zero_scr[...] = jnp.zeros_like(zero_scr)
        zero_row = zero_scr.at[pl.ds(0, rpt)]

        def per_expert(e, carry):
            first_pad = seg_ref[e] + cnt_ref[e]
            n_pad = seg_ref[e + 1] - first_pad

            def fill(r, c):
                pltpu.make_async_copy(zero_row, hs_ref.at[_token_rows(first_pad + r, rpt)], sem).start()
                return c

            def drain(r, c):
                pltpu.make_async_copy(zero_row, hs_ref.at[_token_rows(first_pad + r, rpt)], sem).wait()
                return c

            lax.fori_loop(0, n_pad, fill, 0)
            lax.fori_loop(0, n_pad, drain, 0)
            return carry

        lax.fori_loop(0, N_EXPERTS, per_expert, 0)

        used_tiles = seg_ref[N_EXPERTS] // seg_tile

        def tail_tile(c):
            return hs_ref.at[pl.ds(pl.multiple_of((used_tiles + c) * (seg_tile * rpt), seg_tile * rpt), seg_tile * rpt)]

        def fill_tail(c, carry):
            pltpu.make_async_copy(zero_scr, tail_tile(c), sem).start()
            return carry

        def drain_tail(c, carry):
            pltpu.make_async_copy(zero_scr, tail_tile(c), sem).wait()
            return carry

        lax.fori_loop(0, cap_tiles - used_tiles, fill_tail, 0)
        lax.fori_loop(0, cap_tiles - used_tiles, drain_tail, 0)


def _dispatch(seg_start, counts, pos1, pos2, h2t, rpt):
    n = h2t.shape[0] // rpt
    tm = TM_MOE
    n_tiles = n // tm
    idx = pl.BlockSpec((tm,), lambda i, *_: (i,), memory_space=pltpu.SMEM)
    grid_spec = pltpu.PrefetchScalarGridSpec(
        num_scalar_prefetch=2,
        grid=(n_tiles,),
        in_specs=[idx, idx, pl.BlockSpec((tm * rpt, LANES), lambda i, *_: (i, 0))],
        out_specs=pl.BlockSpec(memory_space=pl.ANY),
        scratch_shapes=[pltpu.VMEM((TM_EXPERT * rpt, LANES), F32), pltpu.SemaphoreType.DMA(())],
    )
    cap_tiles = 2 * n // TM_EXPERT + N_EXPERTS
    return pl.pallas_call(
        functools.partial(_dispatch_kernel, tm=tm, rpt=rpt, n_tiles=n_tiles, seg_tile=TM_EXPERT, cap_tiles=cap_tiles),
        grid_spec=grid_spec,
        out_shape=jax.ShapeDtypeStruct((cap_tiles * TM_EXPERT * rpt, LANES), F32),
        compiler_params=pltpu.CompilerParams(dimension_semantics=("arbitrary",), has_side_effects=True),
        name="dispatch",
    )(seg_start, counts, pos1, pos2, h2t)


def _expert_kernel(t_blk, t_exp, t_valid, t_new, hs_ref, wg_ref, wu_ref, wd_ref, ys_ref,
                   wg_bf, wu_bf, wd_bf, *, tm, rpt):
    t = pl.program_id(0)

    @pl.when(t_valid[t] == 1)
    def _():
        @pl.when(t_new[t] == 1)
        def _():
            wg_bf[...] = wg_ref[...].astype(BF16)
            wu_bf[...] = wu_ref[...].astype(BF16)
            wd_bf[...] = wd_ref[...].astype(BF16)

        h = _from_token_tiles(hs_ref, tm, rpt).astype(BF16)
        gate = jnp.dot(h, wg_bf[...], preferred_element_type=F32)
        up = jnp.dot(h, wu_bf[...], preferred_element_type=F32)
        hid = (jax.nn.silu(gate) * up).astype(BF16)
        _to_token_tiles(ys_ref, jnp.dot(hid, wd_bf[...], preferred_element_type=F32))

    @pl.when(t_valid[t] == 0)
    def _():
        ys_ref[...] = jnp.zeros_like(ys_ref)


def _tile_table(seg_start, n_tiles, tm):
    total = seg_start[N_EXPERTS] // tm
    t = jnp.arange(n_tiles, dtype=I32)
    blk = jnp.minimum(t, total - 1)
    exp = jnp.sum(seg_start[None, 1:N_EXPERTS + 1] <= (blk * tm)[:, None], axis=1).astype(I32)
    valid = (t < total).astype(I32)
    new = (exp != jnp.concatenate([jnp.full((1,), -1, I32), exp[:-1]])).astype(I32)
    return blk, exp, valid, new


def _experts(hs, seg_start, w_e_gate, w_e_up, w_e_down, rpt):
    d, ff = w_e_gate.shape[-2:]
    tm = TM_EXPERT
    n_tiles = hs.shape[0] // (rpt * tm)
    blk, exp, valid, new = _tile_table(seg_start, n_tiles, tm)
    rows = pl.BlockSpec((tm * rpt, LANES), lambda t, b, e, *_: (b[t], 0))
    grid_spec = pltpu.PrefetchScalarGridSpec(
        num_scalar_prefetch=4,
        grid=(n_tiles,),
        in_specs=[
            rows,
            pl.BlockSpec((None, d, ff), lambda t, b, e, *_: (e[t], 0, 0)),
            pl.BlockSpec((None, d, ff), lambda t, b, e, *_: (e[t], 0, 0)),
            pl.BlockSpec((None, ff, d), lambda t, b, e, *_: (e[t], 0, 0)),
        ],
        out_specs=pl.BlockSpec((tm * rpt, LANES), lambda t, *_: (t, 0)),
        scratch_shapes=[pltpu.VMEM((d, ff), BF16), pltpu.VMEM((d, ff), BF16), pltpu.VMEM((ff, d), BF16)],
    )
    return pl.pallas_call(
        functools.partial(_expert_kernel, tm=tm, rpt=rpt),
        grid_spec=grid_spec,
        out_shape=jax.ShapeDtypeStruct(hs.shape, F32),
        compiler_params=pltpu.CompilerParams(dimension_semantics=("arbitrary",)),
        name="experts",
    )(blk, exp, valid, new, hs, w_e_gate, w_e_up, w_e_down)


def _combine_kernel(p1_ref, p2_ref, p1n_ref, p2n_ref, wts_ref, x1_ref, p_ref, gp_ref, wpg_ref, wpp_ref, gf_ref,
                    ys_ref, out_ref, ybuf, sems, *, tm, rpt, n_tiles):
    i = pl.program_id(0)
    slot = i % 2

    def gather(a_ref, b_ref, dst):
        def issue(r, carry):
            rows = _token_rows(r, rpt)
            pltpu.make_async_copy(ys_ref.at[_token_rows(a_ref[r], rpt)], ybuf.at[dst, 0, rows],
                                  sems.at[dst]).start(priority=0)
            pltpu.make_async_copy(ys_ref.at[_token_rows(b_ref[r], rpt)], ybuf.at[dst, 1, rows],
                                  sems.at[dst]).start(priority=1)
            return carry

        lax.fori_loop(0, tm, issue, 0, unroll=DMA_ISSUE_UNROLL)

    @pl.when(i == 0)
    def _():
        gather(p1_ref, p2_ref, 0)

    @pl.when(i + 1 < n_tiles)
    def _():
        gather(p1n_ref, p2n_ref, 1 - slot)

    for half in range(2):
        pltpu.make_async_copy(ys_ref.at[pl.ds(0, tm * rpt)], ybuf.at[slot, half], sems.at[slot]).wait()

    wts = wts_ref[...]
    y1 = _from_token_tiles(ybuf.at[slot, 0], tm, rpt)
    y2 = _from_token_tiles(ybuf.at[slot, 1], tm, rpt)
    x2 = x1_ref[...] + (wts[:, 0:1] * y1 + wts[:, 1:2] * y2)
    hp = _rmsnorm(x2, gp_ref[...], NORM_EPS).astype(BF16)
    gate = jax.nn.sigmoid(jnp.dot(hp, wpg_ref[...], preferred_element_type=F32))
    proj = jnp.dot(p_ref[...].astype(BF16), wpp_ref[...], preferred_element_type=F32)
    x3 = x2 + gate * proj
    out_ref[...] = _rmsnorm(x3, gf_ref[...], NORM_EPS)


def _combine(pos1, pos2, wts, x1, p2d, g_ple, wpg, wpp, g_final, ys, rpt):
    n, d = x1.shape
    pd = p2d.shape[1]
    tm = TM_MOE
    n_tiles = n // tm
    idx = pl.BlockSpec((tm,), lambda i: (i,), memory_space=pltpu.SMEM)
    idx_next = pl.BlockSpec((tm,), lambda i: (jnp.minimum(i + 1, n_tiles - 1),), memory_space=pltpu.SMEM)
    row = pl.BlockSpec((tm, d), lambda i: (i, 0))
    full = lambda r, c: pl.BlockSpec((r, c), lambda i: (0, 0))
    return pl.pallas_call(
        functools.partial(_combine_kernel, tm=tm, rpt=rpt, n_tiles=n_tiles),
        grid=(n_tiles,),
        in_specs=[idx, idx, idx_next, idx_next, pl.BlockSpec((tm, LANES), lambda i: (i, 0)), row,
                  pl.BlockSpec((tm, pd), lambda i: (i, 0)), full(1, d), full(d, d), full(pd, d), full(1, d),
                  pl.BlockSpec(memory_space=pl.ANY)],
        out_specs=row,
        out_shape=jax.ShapeDtypeStruct((n, d), F32),
        scratch_shapes=[pltpu.VMEM((2, 2, tm * rpt, LANES), F32), pltpu.SemaphoreType.DMA((2,))],
        compiler_params=pltpu.CompilerParams(dimension_semantics=("arbitrary",)),
        name="combine",
    )(pos1, pos2, pos1, pos2, wts, x1, p2d, g_ple, wpg, wpp, g_final, ys)


def _layer(i, x2, p2d, b, s, g_mix, w_in, lam_q1, lam_k1, lam_q2, lam_k2, g_subln, w_conv, b_conv, w_rg, b_rg,
           w_ig, b_ig, lru_lambda, w_attn_br, w_lru_br, w_out, g_moe, w_rt_group, b_rt_group, w_rt_expert,
           b_rt_expert, w_e_gate, w_e_up, w_e_down, g_ple, w_ple_gate, w_ple_proj):
    n, d = x2.shape
    lam_init = 0.8 - 0.6 * math.exp(-0.3 * i)
    row = lambda a: a.reshape(1, -1)

    q, k, v, xr, gr, ga, gb = _inproj(x2, row(g_mix), w_in.astype(BF16))
    o = _attention(q.reshape(b, s, d), k.reshape(b, s, d), v.reshape(b, s, d),
                   row(lam_q1), row(lam_k1), row(lam_q2), row(lam_k2), row(g_subln), lam_init).reshape(n, d)
    hg = _lru(xr, gr, w_conv, b_conv, w_rg, b_rg, w_ig, b_ig, lru_lambda, b, s)

    pad = LANES - N_GROUPS - N_EXPERTS
    w_router = jnp.concatenate(
        [w_rt_group, w_rt_expert.transpose(1, 0, 2).reshape(d, N_EXPERTS), jnp.zeros((d, pad), F32)], axis=1)
    b_router = jnp.concatenate([b_rt_group, b_rt_expert.reshape(N_EXPERTS), jnp.zeros((pad,), F32)]).reshape(1, LANES)
    x1, h2, logits = _merge(o, hg, ga, gb, x2, w_attn_br.astype(BF16), w_lru_br.astype(BF16), w_out.astype(BF16),
                            row(g_moe), w_router, b_router)

    pos, wts, offs = _route(logits)
    pos1, pos2 = pos[:, 0], pos[:, 1]
    rpt = d // LANES
    seg_start, counts = offs[0, :N_EXPERTS + 1], offs[1, :N_EXPERTS]
    hs = _dispatch(seg_start, counts, pos1, pos2, h2, rpt)
    ys = _experts(hs, seg_start, w_e_gate, w_e_up, w_e_down, rpt)
    return pos1, pos2, wts, x1, ys


def kernel(x, p, g_mix, w_in, lam_q1, lam_k1, lam_q2, lam_k2, g_subln, w_conv, b_conv, w_rg, b_rg, w_ig, b_ig, lru_lambda, w_attn_br, w_lru_br, w_out, g_moe, w_rt_group, b_rt_group, w_rt_expert, b_rt_expert, w_e_gate, w_e_up, w_e_down, g_ple, w_ple_gate, w_ple_proj, g_final):
    b, s, d = x.shape
    depth = p.shape[0]
    assert depth == 1, "the final RMSNorm is fused into the last layer's combine step; one layer supported"
    n = b * s
    x2 = x.reshape(n, d)
    i = 0
    p2d = p[i].reshape(n, -1)
    pos1, pos2, wts, x1, ys = _layer(
        i, x2, p2d, b, s, g_mix[i], w_in[i], lam_q1[i], lam_k1[i], lam_q2[i], lam_k2[i], g_subln[i], w_conv[i],
        b_conv[i], w_rg[i], b_rg[i], w_ig[i], b_ig[i], lru_lambda[i], w_attn_br[i], w_lru_br[i], w_out[i],
        g_moe[i], w_rt_group[i], b_rt_group[i], w_rt_expert[i], b_rt_expert[i], w_e_gate[i], w_e_up[i],
        w_e_down[i], g_ple[i], w_ple_gate[i], w_ple_proj[i])
    out = _combine(pos1, pos2, wts, x1, p2d, g_ple[i].reshape(1, d), w_ple_gate[i].astype(BF16),
                   w_ple_proj[i].astype(BF16), g_final.reshape(1, d), ys, d // LANES)
    return out.reshape(b, s, d)
```

```python
import functools
import math

import jax
import jax.numpy as jnp
from jax import lax
from jax.experimental import pallas as pl
from jax.experimental.pallas import tpu as pltpu

F32 = jnp.float32
BF16 = jnp.bfloat16
I32 = jnp.int32

N_HEADS = 8
HEAD_DIM = 64
V_DIM = 2 * HEAD_DIM
SUBLN_EPS = 1e-5
NORM_EPS = 1e-6
LRU_BLOCKS = 16
CONV_WIDTH = 4
LRU_C = 8.0
N_GROUPS = 4
EXPERTS_PER_GROUP = 8
N_EXPERTS = N_GROUPS * EXPERTS_PER_GROUP
N_IN_PARTS = 7
LOG2_E = 1.4426950408889634
Q_SCALE = HEAD_DIM ** -0.5 * LOG2_E

LANES = 128
SUBLANES = 8
MXU_DIM = 256

TM_INPROJ = 256
T_ATTN = 256
T_LRU = 256
TM_MERGE = 512
T_ROUTE = 512
TM_MOE = 256
TM_EXPERT = 256
DMA_ISSUE_UNROLL = 8
MASK_NEG = -1e30


def _rmsnorm(x, g, eps):
    return (x * lax.rsqrt(jnp.mean(x * x, axis=-1, keepdims=True) + eps)) * g


def _inproj_kernel(x_ref, g_ref, w_ref, *out_refs, d):
    h = _rmsnorm(x_ref[...], g_ref[...], NORM_EPS).astype(BF16)
    for c, o_ref in enumerate(out_refs):
        z = jnp.dot(h, w_ref[:, c * d:(c + 1) * d], preferred_element_type=F32)
        if c == 0:
            z = z * Q_SCALE
        o_ref[...] = z.astype(o_ref.dtype)


def _inproj(x2, g_mix, w_in_bf):
    n, d = x2.shape
    tm = TM_INPROJ
    out_dtypes = (BF16, BF16, BF16, F32, F32, F32, F32)
    row_spec = pl.BlockSpec((tm, d), lambda i: (i, 0))
    return pl.pallas_call(
        functools.partial(_inproj_kernel, d=d),
        grid=(n // tm,),
        in_specs=[
            row_spec,
            pl.BlockSpec((1, d), lambda i: (0, 0)),
            pl.BlockSpec((d, N_IN_PARTS * d), lambda i: (0, 0), pipeline_mode=pl.Buffered(1)),
        ],
        out_specs=[row_spec] * N_IN_PARTS,
        out_shape=[jax.ShapeDtypeStruct((n, d), dt) for dt in out_dtypes],
        compiler_params=pltpu.CompilerParams(dimension_semantics=("arbitrary",)),
        name="inproj",
    )(x2, g_mix, w_in_bf)


V_AUG = V_DIM + 16
PIPE_GROUP = 4
PIPE_RING = 4
PIPE_SLOTS = PIPE_GROUP * PIPE_RING


def _attn_kernel(lq1_ref, lk1_ref, lq2_ref, lk2_ref, g_ref, q_ref, k_ref, v_ref, o_ref,
                 qt_scr, vt_scr, acc_scr, m_scr, bias_scr, s_scr, p_scr, cmax_scr, alpha_scr, *, t, lam_init):
    nq = q_ref.shape[0] // t
    w = 2 * t

    feat = lax.broadcasted_iota(I32, (t, V_DIM), 1)
    ones_pad = jnp.where(lax.broadcasted_iota(I32, (V_AUG - V_DIM, t), 0) == 0, 1.0, 0.0).astype(F32)

    def prep(i, carry):
        rows = pl.ds(pl.multiple_of(i * t, t), t)
        q = q_ref[rows, :].astype(F32)
        q1t = jnp.where(feat < HEAD_DIM, q, 0.0).T
        q2t = jnp.where(feat >= HEAD_DIM, q, 0.0).T
        qt_scr[i] = jnp.concatenate([q1t, q2t], axis=1).astype(BF16)
        vt = v_ref[rows, :].astype(F32).T
        vt_scr[i] = jnp.concatenate([vt, ones_pad], axis=0).astype(BF16)
        m_scr[i] = jnp.full((1, w), MASK_NEG, F32)
        acc_scr[i] = jnp.zeros((V_AUG, w), F32)
        return carry

    lax.fori_loop(0, nq, prep, 0, unroll=2)
    key = lax.broadcasted_iota(I32, (t, w), 0)
    qry = lax.broadcasted_iota(I32, (t, w), 1) & (t - 1)
    bias_scr[...] = jnp.where(key <= qry, 0.0, MASK_NEG)

    def stage_a(j, i, slot, masked):
        kj = k_ref[pl.ds(pl.multiple_of(j * t, t), t), :]
        s = jnp.dot(kj, qt_scr[i], preferred_element_type=F32)
        if masked:
            s = s + bias_scr[...]
        s_scr[slot] = s
        cmax_scr[slot] = jnp.max(s, axis=0, keepdims=True)

    def stage_b(i, slot):
        m_old = m_scr[i]
        m_new = jnp.maximum(m_old, cmax_scr[slot])
        m_scr[i] = m_new
        p_scr[slot] = jnp.exp2(s_scr[slot] - m_new).astype(BF16)
        alpha_scr[slot] = jnp.exp2(m_old - m_new)

    def stage_c(j, i, slot):
        acc_scr[i] = alpha_scr[slot] * acc_scr[i] + jnp.dot(vt_scr[j], p_scr[slot], preferred_element_type=F32)

    def run(n_steps, first, nxt, masked):
        assert n_steps % (PIPE_RING * PIPE_GROUP) == 0
        n_groups = n_steps // PIPE_GROUP

        def steps_of(st):
            out = []
            for _ in range(PIPE_GROUP):
                out.append(st)
                st = nxt(*st)
            return out, st

        def turn(t_mod, a=None, b=None, c=None):
            base_a = (t_mod % PIPE_RING) * PIPE_GROUP
            base_b = ((t_mod - 2) % PIPE_RING) * PIPE_GROUP
            a_steps, following = steps_of(a) if a is not None else (None, None)
            b_steps = steps_of(b)[0] if b is not None else None
            c_steps = steps_of(c)[0] if c is not None else None
            for kk in range(PIPE_GROUP):
                if a_steps is not None:
                    stage_a(*a_steps[kk], base_a + kk, masked)
                if b_steps is not None:
                    stage_b(b_steps[kk][1], base_b + kk)
                if c_steps is not None:
                    stage_c(*c_steps[kk], base_a + kk)
            return following

        f0 = first
        f1 = turn(0, a=f0)
        f2 = turn(1, a=f1)
        f3 = turn(2, a=f2, b=f0)
        f4 = turn(3, a=f3, b=f1)

        def ring(_, f):
            f = list(f)
            for r in range(PIPE_RING):
                f.append(turn(r, a=f[-1], b=f[-3], c=f[-5]))
            return tuple(f[-5:])

        f = lax.fori_loop(0, (n_groups - PIPE_RING) // PIPE_RING, ring, (f0, f1, f2, f3, f4))
        turn(0, b=f[2], c=f[0])
        turn(1, b=f[3], c=f[1])
        turn(2, c=f[2])
        turn(3, c=f[3])

    zero = jnp.int32(0)
    run(nq, (zero, zero), lambda j, i: (j + 1, i + 1), masked=True)

    def next_below_diagonal(j, i):
        wrap = i + 1 >= nq
        return jnp.where(wrap, j + 1, j), jnp.where(wrap, j + 2, i + 1)

    run(nq * (nq - 1) // 2, (zero, zero + 1), next_below_diagonal, masked=False)

    lam = (jnp.exp(jnp.sum(lq1_ref[...] * lk1_ref[...], axis=1, keepdims=True))
           - jnp.exp(jnp.sum(lq2_ref[...] * lk2_ref[...], axis=1, keepdims=True)) + lam_init)
    gain = g_ref[...]

    def finish(i, carry):
        acc = acc_scr[i]
        o1 = acc[:V_DIM, :t] / acc[V_DIM:V_DIM + 1, :t]
        o2 = acc[:V_DIM, t:] / acc[V_DIM:V_DIM + 1, t:]
        o = o1 - lam * o2
        y = ((o * lax.rsqrt(jnp.mean(o * o, axis=0, keepdims=True) + SUBLN_EPS)) * gain) * (1.0 - lam_init)
        o_ref[pl.ds(pl.multiple_of(i * t, t), t), :] = y.T.astype(o_ref.dtype)
        return carry

    lax.fori_loop(0, nq, finish, 0, unroll=2)


def _attention(q, k, v, lam_q1, lam_k1, lam_q2, lam_k2, g_subln, lam_init):
    b, s, d = q.shape
    t = T_ATTN
    assert s % t == 0 and d == N_HEADS * V_DIM
    nq = s // t
    small = lambda w: pl.BlockSpec((1, w), lambda bi, hi: (0, 0))
    head = pl.BlockSpec((None, s, V_DIM), lambda bi, hi: (bi, 0, hi))
    return pl.pallas_call(
        functools.partial(_attn_kernel, t=t, lam_init=lam_init),
        grid=(b, N_HEADS),
        in_specs=[small(HEAD_DIM), small(HEAD_DIM), small(HEAD_DIM), small(HEAD_DIM),
                  pl.BlockSpec((V_DIM, 1), lambda bi, hi: (0, 0)), head, head, head],
        out_specs=head,
        out_shape=jax.ShapeDtypeStruct((b, s, d), BF16),
        scratch_shapes=[
            pltpu.VMEM((nq, V_DIM, 2 * t), BF16),
            pltpu.VMEM((nq, V_AUG, t), BF16),
            pltpu.VMEM((nq, V_AUG, 2 * t), F32),
            pltpu.VMEM((nq, 1, 2 * t), F32),
            pltpu.VMEM((t, 2 * t), F32),
            pltpu.VMEM((PIPE_SLOTS,t, 2 * t), F32),
            pltpu.VMEM((PIPE_SLOTS,t, 2 * t), BF16),
            pltpu.VMEM((PIPE_SLOTS,1, 2 * t), F32),
            pltpu.VMEM((PIPE_SLOTS,1, 2 * t), F32),
        ],
        compiler_params=pltpu.CompilerParams(dimension_semantics=("arbitrary", "arbitrary")),
        name="attn",
    )(lam_q1, lam_k1, lam_q2, lam_k2, g_subln.reshape(V_DIM, 1), q, k, v)


def _lru_kernel(xr_ref, gr_ref, wc_ref, bc_ref, wrg_ref, brg_ref, wig_ref, big_ref, lam_ref, o_ref,
                xprev_ref, hcar_ref, a_scr, u_scr, *, t_tile):
    @pl.when(pl.program_id(1) == 0)
    def _():
        xprev_ref[...] = jnp.zeros_like(xprev_ref)
        hcar_ref[...] = jnp.zeros_like(hcar_ref)

    x = xr_ref[...]
    w = x.shape[1]
    prev = xprev_ref[...]
    wc = wc_ref[...]
    row8 = lax.broadcasted_iota(I32, (SUBLANES, w), 0)
    xc = bc_ref[...]
    for j in range(CONV_WIDTH):
        shift = CONV_WIDTH - 1 - j
        if shift == 0:
            xs = x
        else:
            rolled = pltpu.roll(x, shift, 0)
            head = jnp.where(row8 < shift, pltpu.roll(prev, shift, 0), rolled[:SUBLANES])
            xs = jnp.concatenate([head, rolled[SUBLANES:]], axis=0)
        xc = xc + xs * wc[j:j + 1]
    xprev_ref[...] = x[t_tile - SUBLANES:]

    xcb = xc.astype(BF16)
    n_grp = w // MXU_DIM
    pre_r = jnp.concatenate(
        [jnp.dot(xcb[:, g * MXU_DIM:(g + 1) * MXU_DIM], wrg_ref[g], preferred_element_type=F32) for g in range(n_grp)],
        axis=1) + brg_ref[...]
    pre_i = jnp.concatenate(
        [jnp.dot(xcb[:, g * MXU_DIM:(g + 1) * MXU_DIM], wig_ref[g], preferred_element_type=F32) for g in range(n_grp)],
        axis=1) + big_ref[...]
    r = jax.nn.sigmoid(pre_r)
    ig = jax.nn.sigmoid(pre_i)
    neg_lam = -lam_ref[...]
    softplus = jnp.maximum(neg_lam, 0.0) + jnp.log1p(jnp.exp(-jnp.abs(neg_lam)))
    log_a = (-LRU_C) * r * softplus
    a = jnp.exp(log_a)
    th = jnp.tanh(log_a)
    num = -2.0 * th
    mult = jnp.where(num > 0.0, num * lax.rsqrt(num), 0.0) * lax.rsqrt(1.0 - th)
    u = mult * (ig * xc)

    def scan_steps(a, u, pos, length, axis):
        dist = 1
        while dist < length:
            keep = pos >= dist
            u = jnp.where(keep, a * pltpu.roll(u, dist, axis) + u, u)
            a = jnp.where(keep, a * pltpu.roll(a, dist, axis), a)
            dist *= 2
        return a, u

    n_grp_t = t_tile // SUBLANES
    grouped = (n_grp_t, SUBLANES, w)
    a, u = scan_steps(a.reshape(grouped), u.reshape(grouped), lax.broadcasted_iota(I32, grouped, 1), SUBLANES, 1)
    a, u = a.reshape(x.shape), u.reshape(x.shape)
    n_lt = w // LANES
    for c in range(n_lt):
        a_scr[c] = a[:, c * LANES:(c + 1) * LANES]
        u_scr[c] = u[:, c * LANES:(c + 1) * LANES]
    last = pl.ds(SUBLANES - 1, n_grp_t, stride=SUBLANES)
    grow = lax.broadcasted_iota(I32, (n_grp_t, w), 0)
    ag, ug = scan_steps(jnp.concatenate([a_scr[c, last, :] for c in range(n_lt)], axis=1),
                        jnp.concatenate([u_scr[c, last, :] for c in range(n_lt)], axis=1), grow, n_grp_t, 0)
    h_end = ag * hcar_ref[...] + ug
    h_in = jnp.where(grow == 0, hcar_ref[...], pltpu.roll(h_end, 1, 0))
    hcar_ref[...] = h_end[n_grp_t - 1:]
    for c in range(n_lt):
        for k in range(SUBLANES):
            a_scr[c, pl.ds(k, n_grp_t, stride=SUBLANES), :] = h_in[:, c * LANES:(c + 1) * LANES]
    h = a * jnp.concatenate([a_scr[c] for c in range(n_lt)], axis=1) + u
    o_ref[...] = (h * jax.nn.gelu(gr_ref[...])).astype(o_ref.dtype)


def _block_diag_groups(wblk):
    nb, bw, _ = wblk.shape
    per = MXU_DIM // bw
    g = nb // per
    eye = jnp.eye(per, dtype=wblk.dtype)
    w5 = wblk.reshape(g, per, bw, bw)
    return jnp.einsum("gawv,ab->gawbv", w5, eye).reshape(g, MXU_DIM, MXU_DIM).astype(BF16)


def _lru(xr, gr, w_conv, b_conv, w_rg, b_rg, w_ig, b_ig, lru_lambda, b, s):
    n, w = xr.shape
    t = T_LRU
    n_t = s // t
    row_spec = pl.BlockSpec((t, w), lambda bi, ti: (bi * n_t + ti, 0))
    vec = lambda r: pl.BlockSpec((r, w), lambda bi, ti: (0, 0))
    blk = pl.BlockSpec((w // MXU_DIM, MXU_DIM, MXU_DIM), lambda bi, ti: (0, 0, 0))
    return pl.pallas_call(
        functools.partial(_lru_kernel, t_tile=t),
        grid=(b, n_t),
        in_specs=[row_spec, row_spec, vec(CONV_WIDTH), vec(1), blk, vec(1), blk, vec(1), vec(1)],
        out_specs=row_spec,
        out_shape=jax.ShapeDtypeStruct((n, w), BF16),
        scratch_shapes=[pltpu.VMEM((SUBLANES, w), F32), pltpu.VMEM((1, w), F32),
                        pltpu.VMEM((w // LANES, t, LANES), F32), pltpu.VMEM((w // LANES, t, LANES), F32)],
        compiler_params=pltpu.CompilerParams(dimension_semantics=("arbitrary", "arbitrary")),
        name="lru",
    )(xr, gr, w_conv, b_conv.reshape(1, w), _block_diag_groups(w_rg), b_rg.reshape(1, w),
      _block_diag_groups(w_ig), b_ig.reshape(1, w), lru_lambda.reshape(1, w))


def _split_bf16(x):
    hi = x.astype(BF16)
    return hi, (x - hi.astype(F32)).astype(BF16)


def _to_token_tiles(ref, x):
    tm, d = x.shape
    rpt = d // LANES
    for c in range(rpt):
        ref[pl.ds(c, tm, stride=rpt), :] = x[:, c * LANES:(c + 1) * LANES]


def _from_token_tiles(ref, tm, rpt):
    return jnp.concatenate([ref[pl.ds(c, tm, stride=rpt), :] for c in range(rpt)], axis=1)


def _merge_kernel(o_ref, hg_ref, ga_ref, gb_ref, x_ref, wa_ref, wl_ref, wo_ref, gm_ref, wrh_ref, wrl_ref, br_ref,
                  x1_ref, h2_ref, lg_ref):
    attn_br = jnp.dot(o_ref[...], wa_ref[...], preferred_element_type=F32)
    lru_br = jnp.dot(hg_ref[...], wl_ref[...], preferred_element_type=F32)
    mixed = jax.nn.sigmoid(ga_ref[...]) * attn_br + jax.nn.sigmoid(gb_ref[...]) * lru_br
    x1 = x_ref[...] + jnp.dot(mixed.astype(BF16), wo_ref[...], preferred_element_type=F32)
    x1_ref[...] = x1
    h2 = _rmsnorm(x1, gm_ref[...], NORM_EPS)
    _to_token_tiles(h2_ref, h2)
    hi, lo = _split_bf16(h2)
    wrh = wrh_ref[...]
    lg_ref[...] = (jnp.dot(hi, wrh, preferred_element_type=F32) + jnp.dot(lo, wrh, preferred_element_type=F32)
                   + jnp.dot(hi, wrl_ref[...], preferred_element_type=F32)) + br_ref[...]


def _merge(o, hg, ga, gb, x2, wa, wl, wo, g_moe, w_router, b_router):
    n, d = x2.shape
    tm = TM_MERGE
    rpt = d // LANES
    row = pl.BlockSpec((tm, d), lambda i: (i, 0))
    full = lambda r, c: pl.BlockSpec((r, c), lambda i: (0, 0))
    wr_hi, wr_lo = _split_bf16(w_router)
    return pl.pallas_call(
        _merge_kernel,
        grid=(n // tm,),
        in_specs=[row, row, row, row, row, full(d, d), full(d, d), full(d, d), full(1, d),
                  full(d, LANES), full(d, LANES), full(1, LANES)],
        out_specs=[row, pl.BlockSpec((tm * rpt, LANES), lambda i: (i, 0)), pl.BlockSpec((tm, LANES), lambda i: (i, 0))],
        out_shape=[jax.ShapeDtypeStruct((n, d), F32), jax.ShapeDtypeStruct((n * rpt, LANES), F32),
                   jax.ShapeDtypeStruct((n, LANES), F32)],
        compiler_params=pltpu.CompilerParams(dimension_semantics=("arbitrary",)),
        name="merge",
    )(o, hg, ga, gb, x2, wa, wl, wo, g_moe, wr_hi, wr_lo, b_router)


def _lane_pick(x, lane, idx):
    return jnp.sum(jnp.where(lane == idx, x, jnp.zeros_like(x)), axis=1, keepdims=True)


def _route_kernel(lg_ref, pos_ref, wts_ref, offs_ref, *, n, t_tile, seg_tile):
    shape = (t_tile, LANES)
    lane = lax.broadcasted_iota(I32, shape, 1)
    neg_inf = jnp.float32(-jnp.inf)
    tri = (lax.broadcasted_iota(I32, (t_tile, t_tile), 0) > lax.broadcasted_iota(I32, (t_tile, t_tile), 1)).astype(BF16)

    def phase1(t, cnt):
        rows = pl.ds(pl.multiple_of(t * t_tile, t_tile), t_tile)
        lg = lg_ref[rows, :]
        is_grp = lane < N_GROUPS
        gl = jnp.where(is_grp, lg, neg_inf)
        gmax = jnp.max(gl, axis=1, keepdims=True)
        g_idx = jnp.min(jnp.where(gl == gmax, lane, LANES), axis=1, keepdims=True)
        g_w = 1.0 / jnp.sum(jnp.where(is_grp, jnp.exp(lg - gmax), 0.0), axis=1, keepdims=True)
        lo = N_GROUPS + EXPERTS_PER_GROUP * g_idx
        in_grp = (lane >= lo) & (lane < lo + EXPERTS_PER_GROUP)
        fl = jnp.where(in_grp, lg, neg_inf)
        v1 = jnp.max(fl, axis=1, keepdims=True)
        i1 = jnp.min(jnp.where(in_grp & (fl == v1), lane, LANES), axis=1, keepdims=True)
        rest = in_grp & (lane != i1)
        fl2 = jnp.where(rest, lg, neg_inf)
        v2 = jnp.max(fl2, axis=1, keepdims=True)
        i2 = jnp.min(jnp.where(rest & (fl2 == v2), lane, LANES), axis=1, keepdims=True)
        t2 = jnp.exp(v2 - v1)
        den = 1.0 + t2
        w1 = g_w * (1.0 / den)
        w2 = g_w * (t2 / den)
        e1 = i1 - N_GROUPS
        e2 = i2 - N_GROUPS
        onehot = ((lane == e1) | (lane == e2)).astype(F32)
        before = jnp.dot(tri, onehot.astype(BF16), preferred_element_type=F32) + cnt
        rank1 = _lane_pick(before, lane, e1)
        rank2 = _lane_pick(before, lane, e2)
        pos_ref[rows, :] = jnp.where(lane == 0, e1, jnp.where(lane == 1, e2, jnp.where(
            lane == 2, rank1.astype(I32), jnp.where(lane == 3, rank2.astype(I32), 0))))
        wts_ref[rows, :] = jnp.where(lane == 0, w1, jnp.where(lane == 1, w2, 0.0))
        return cnt + jnp.sum(onehot, axis=0, keepdims=True)

    cnt = lax.fori_loop(0, n // t_tile, phase1, jnp.zeros((1, LANES), F32))

    lane8 = lax.broadcasted_iota(I32, (SUBLANES, LANES), 1)
    row8 = lax.broadcasted_iota(I32, (SUBLANES, LANES), 0)
    cnt8 = jnp.broadcast_to(cnt, (SUBLANES, LANES))
    padded = jnp.ceil(cnt8 * (1.0 / seg_tile)) * seg_tile
    incl = padded
    dist = 1
    while dist < LANES:
        incl = incl + jnp.where(lane8 >= dist, pltpu.roll(incl, dist, 1), 0.0)
        dist *= 2
    offs = incl - padded
    offs_ref[...] = jnp.where(row8 == 0, offs, jnp.where(row8 == 1, cnt8, 0.0)).astype(I32)
    offs_row = offs[0:1]

    def phase2(t, carry):
        rows = pl.ds(pl.multiple_of(t * t_tile, t_tile), t_tile)
        info = pos_ref[rows, :]
        e1 = _lane_pick(info, lane, 0)
        e2 = _lane_pick(info, lane, 1)
        r1 = _lane_pick(info, lane, 2)
        r2 = _lane_pick(info, lane, 3)
        offs_b = jnp.broadcast_to(offs_row, shape).astype(I32)
        p1 = r1 + _lane_pick(offs_b, lane, e1)
        p2 = r2 + _lane_pick(offs_b, lane, e2)
        pos_ref[rows, :] = jnp.where(lane == 0, p1, jnp.where(lane == 1, p2, 0))
        return carry

    lax.fori_loop(0, n // t_tile, phase2, 0)


def _route(logits):
    n = logits.shape[0]
    return pl.pallas_call(
        functools.partial(_route_kernel, n=n, t_tile=T_ROUTE, seg_tile=TM_EXPERT),
        out_shape=[jax.ShapeDtypeStruct((n, LANES), I32), jax.ShapeDtypeStruct((n, LANES), F32),
                   jax.ShapeDtypeStruct((SUBLANES, LANES), I32)],
        name="route",
    )(logits)


def _token_rows(idx, rpt):
    return pl.ds(pl.multiple_of(idx * rpt, rpt), rpt)


def _dispatch_kernel(seg_ref, cnt_ref, p1_ref, p2_ref, h_ref, hs_ref, zero_scr, sem, *, tm, rpt, n_tiles,
                     seg_tile, cap_tiles):
    def issue(r, carry):
        src = h_ref.at[_token_rows(r, rpt)]
        pltpu.make_async_copy(src, hs_ref.at[_token_rows(p1_ref[r], rpt)], sem).start(priority=0)
        pltpu.make_async_copy(src, hs_ref.at[_token_rows(p2_ref[r], rpt)], sem).start(priority=1)
        return carry

    lax.fori_loop(0, tm, issue, 0, unroll=DMA_ISSUE_UNROLL)
    whole = pltpu.make_async_copy(h_ref, hs_ref.at[pl.ds(0, tm * rpt)], sem)
    whole.wait()
    whole.wait()

    @pl.when(pl.program_id(0) == n_tiles - 1)
    def _():
        zero_scr[...] = jnp.zeros_like(zero_scr)
        zero_row = zero_scr.at[pl.ds(0, rpt)]

        def per_expert(e, carry):
            first_pad = seg_ref[e] + cnt_ref[e]
            n_pad = seg_ref[e + 1] - first_pad

            def fill(r, c):
                pltpu.make_async_copy(zero_row, hs_ref.at[_token_rows(first_pad + r, rpt)], sem).start()
                return c

            def drain(r, c):
                pltpu.make_async_copy(zero_row, hs_ref.at[_token_rows(first_pad + r, rpt)], sem).wait()
                return c

            lax.fori_loop(0, n_pad, fill, 0)
            lax.fori_loop(0, n_pad, drain, 0)
            return carry

        lax.fori_loop(0, N_EXPERTS, per_expert, 0)

        used_tiles = seg_ref[N_EXPERTS] // seg_tile

        def tail_tile(c):
            return hs_ref.at[pl.ds(pl.multiple_of((used_tiles + c) * (seg_tile * rpt), seg_tile * rpt), seg_tile * rpt)]

        def fill_tail(c, carry):
            pltpu.make_async_copy(zero_scr, tail_tile(c), sem).start()
            return carry

        def drain_tail(c, carry):
            pltpu.make_async_copy(zero_scr, tail_tile(c), sem).wait()
            return carry

        lax.fori_loop(0, cap_tiles - used_tiles, fill_tail, 0)
        lax.fori_loop(0, cap_tiles - used_tiles, drain_tail, 0)


def _dispatch(seg_start, counts, pos1, pos2, h2t, rpt):
    n = h2t.shape[0] // rpt
    tm = TM_MOE
    n_tiles = n // tm
    idx = pl.BlockSpec((tm,), lambda i, *_: (i,), memory_space=pltpu.SMEM)
    grid_spec = pltpu.PrefetchScalarGridSpec(
        num_scalar_prefetch=2,
        grid=(n_tiles,),
        in_specs=[idx, idx, pl.BlockSpec((tm * rpt, LANES), lambda i, *_: (i, 0))],
        out_specs=pl.BlockSpec(memory_space=pl.ANY),
        scratch_shapes=[pltpu.VMEM((TM_EXPERT * rpt, LANES), F32), pltpu.SemaphoreType.DMA(())],
    )
    cap_tiles = 2 * n // TM_EXPERT + N_EXPERTS
    return pl.pallas_call(
        functools.partial(_dispatch_kernel, tm=tm, rpt=rpt, n_tiles=n_tiles, seg_tile=TM_EXPERT, cap_tiles=cap_tiles),
        grid_spec=grid_spec,
        out_shape=jax.ShapeDtypeStruct((cap_tiles * TM_EXPERT * rpt, LANES), F32),
        compiler_params=pltpu.CompilerParams(dimension_semantics=("arbitrary",), has_side_effects=True),
        name="dispatch",
    )(seg_start, counts, pos1, pos2, h2t)


def _expert_kernel(t_blk, t_exp, t_valid, t_new, t_slot, t_next, hs_ref, wg_hbm, wu_hbm, wd_hbm, ys_ref,
                   wg_f32, wu_f32, wd_f32, wg_bf, wu_bf, wd_bf, sems, *, tm, rpt):
    t = pl.program_id(0)

    def weight_copies(e, slot):
        return [pltpu.make_async_copy(src.at[e], dst.at[slot], sems.at[slot])
                for src, dst in ((wg_hbm, wg_f32), (wu_hbm, wu_f32), (wd_hbm, wd_f32))]

    @pl.when(t_valid[t] == 1)
    def _():
        @pl.when(t_new[t] == 1)
        def _():
            slot = t_slot[t]

            @pl.when(t == 0)
            def _():
                for c in weight_copies(t_exp[t], slot):
                    c.start()

            for c in weight_copies(t_exp[t], slot):
                c.wait()
            wg_bf[...] = wg_f32[slot].astype(BF16)
            wu_bf[...] = wu_f32[slot].astype(BF16)
            wd_bf[...] = wd_f32[slot].astype(BF16)

            @pl.when(t_next[t] >= 0)
            def _():
                for c in weight_copies(t_next[t], 1 - slot):
                    c.start()

        h = _from_token_tiles(hs_ref, tm, rpt).astype(BF16)
        gate = jnp.dot(h, wg_bf[...], preferred_element_type=F32)
        up = jnp.dot(h, wu_bf[...], preferred_element_type=F32)
        hid = (jax.nn.silu(gate) * up).astype(BF16)
        _to_token_tiles(ys_ref, jnp.dot(hid, wd_bf[...], preferred_element_type=F32))

    @pl.when(t_valid[t] == 0)
    def _():
        ys_ref[...] = jnp.zeros_like(ys_ref)


def _tile_table(seg_start, n_tiles, tm):
    total = seg_start[N_EXPERTS] // tm
    t = jnp.arange(n_tiles, dtype=I32)
    blk = jnp.minimum(t, total - 1)
    exp = jnp.sum(seg_start[None, 1:N_EXPERTS + 1] <= (blk * tm)[:, None], axis=1).astype(I32)
    valid = (t < total).astype(I32)
    new = (exp != jnp.concatenate([jnp.full((1,), -1, I32), exp[:-1]])).astype(I32)
    slot = (jnp.cumsum(new) - 1) % 2
    later = jnp.where(exp[None, :] > exp[:, None], exp[None, :], N_EXPERTS)
    nxt = jnp.min(later, axis=1)
    nxt = jnp.where(nxt < N_EXPERTS, nxt, -1).astype(I32)
    return blk, exp, valid, new, slot.astype(I32), nxt


def _experts(hs, seg_start, w_e_gate, w_e_up, w_e_down, rpt):
    d, ff = w_e_gate.shape[-2:]
    tm = TM_EXPERT
    n_tiles = hs.shape[0] // (rpt * tm)
    table = _tile_table(seg_start, n_tiles, tm)
    hbm = pl.BlockSpec(memory_space=pl.ANY)
    grid_spec = pltpu.PrefetchScalarGridSpec(
        num_scalar_prefetch=len(table),
        grid=(n_tiles,),
        in_specs=[pl.BlockSpec((tm * rpt, LANES), lambda t, b, *_: (b[t], 0)), hbm, hbm, hbm],
        out_specs=pl.BlockSpec((tm * rpt, LANES), lambda t, *_: (t, 0)),
        scratch_shapes=[pltpu.VMEM((2, d, ff), F32), pltpu.VMEM((2, d, ff), F32), pltpu.VMEM((2, ff, d), F32),
                        pltpu.VMEM((d, ff), BF16), pltpu.VMEM((d, ff), BF16), pltpu.VMEM((ff, d), BF16),
                        pltpu.SemaphoreType.DMA((2,))],
    )
    return pl.pallas_call(
        functools.partial(_expert_kernel, tm=tm, rpt=rpt),
        grid_spec=grid_spec,
        out_shape=jax.ShapeDtypeStruct(hs.shape, F32),
        compiler_params=pltpu.CompilerParams(dimension_semantics=("arbitrary",)),
        name="experts",
    )(*table, hs, w_e_gate, w_e_up, w_e_down)


def _combine_kernel(p1_ref, p2_ref, p1n_ref, p2n_ref, wts_ref, x1_ref, p_ref, gp_ref, wpg_ref, wpp_ref, gf_ref,
                    ys_ref, out_ref, ybuf, sems, *, tm, rpt, n_tiles):
    i = pl.program_id(0)
    slot = i % 2

    def gather(a_ref, b_ref, dst):
        def issue(r, carry):
            rows = _token_rows(r, rpt)
            pltpu.make_async_copy(ys_ref.at[_token_rows(a_ref[r], rpt)], ybuf.at[dst, 0, rows],
                                  sems.at[dst]).start(priority=0)
            pltpu.make_async_copy(ys_ref.at[_token_rows(b_ref[r], rpt)], ybuf.at[dst, 1, rows],
                                  sems.at[dst]).start(priority=1)
            return carry

        lax.fori_loop(0, tm, issue, 0, unroll=DMA_ISSUE_UNROLL)

    @pl.when(i == 0)
    def _():
        gather(p1_ref, p2_ref, 0)

    @pl.when(i + 1 < n_tiles)
    def _():
        gather(p1n_ref, p2n_ref, 1 - slot)

    for half in range(2):
        pltpu.make_async_copy(ys_ref.at[pl.ds(0, tm * rpt)], ybuf.at[slot, half], sems.at[slot]).wait()

    wts = wts_ref[...]
    y1 = _from_token_tiles(ybuf.at[slot, 0], tm, rpt)
    y2 = _from_token_tiles(ybuf.at[slot, 1], tm, rpt)
    x2 = x1_ref[...] + (wts[:, 0:1] * y1 + wts[:, 1:2] * y2)
    hp = _rmsnorm(x2, gp_ref[...], NORM_EPS).astype(BF16)
    gate = jax.nn.sigmoid(jnp.dot(hp, wpg_ref[...], preferred_element_type=F32))
    proj = jnp.dot(p_ref[...].astype(BF16), wpp_ref[...], preferred_element_type=F32)
    x3 = x2 + gate * proj
    out_ref[...] = _rmsnorm(x3, gf_ref[...], NORM_EPS)


def _combine(pos1, pos2, wts, x1, p2d, g_ple, wpg, wpp, g_final, ys, rpt):
    n, d = x1.shape
    pd = p2d.shape[1]
    tm = TM_MOE
    n_tiles = n // tm
    idx = pl.BlockSpec((tm,), lambda i: (i,), memory_space=pltpu.SMEM)
    idx_next = pl.BlockSpec((tm,), lambda i: (jnp.minimum(i + 1, n_tiles - 1),), memory_space=pltpu.SMEM)
    row = pl.BlockSpec((tm, d), lambda i: (i, 0))
    full = lambda r, c: pl.BlockSpec((r, c), lambda i: (0, 0))
    return pl.pallas_call(
        functools.partial(_combine_kernel, tm=tm, rpt=rpt, n_tiles=n_tiles),
        grid=(n_tiles,),
        in_specs=[idx, idx, idx_next, idx_next, pl.BlockSpec((tm, LANES), lambda i: (i, 0)), row,
                  pl.BlockSpec((tm, pd), lambda i: (i, 0)), full(1, d), full(d, d), full(pd, d), full(1, d),
                  pl.BlockSpec(memory_space=pl.ANY)],
        out_specs=row,
        out_shape=jax.ShapeDtypeStruct((n, d), F32),
        scratch_shapes=[pltpu.VMEM((2, 2, tm * rpt, LANES), F32), pltpu.SemaphoreType.DMA((2,))],
        compiler_params=pltpu.CompilerParams(dimension_semantics=("arbitrary",)),
        name="combine",
    )(pos1, pos2, pos1, pos2, wts, x1, p2d, g_ple, wpg, wpp, g_final, ys)


def _layer(i, x2, p2d, b, s, g_mix, w_in, lam_q1, lam_k1, lam_q2, lam_k2, g_subln, w_conv, b_conv, w_rg, b_rg,
           w_ig, b_ig, lru_lambda, w_attn_br, w_lru_br, w_out, g_moe, w_rt_group, b_rt_group, w_rt_expert,
           b_rt_expert, w_e_gate, w_e_up, w_e_down, g_ple, w_ple_gate, w_ple_proj):
    n, d = x2.shape
    lam_init = 0.8 - 0.6 * math.exp(-0.3 * i)
    row = lambda a: a.reshape(1, -1)

    q, k, v, xr, gr, ga, gb = _inproj(x2, row(g_mix), w_in.astype(BF16))
    o = _attention(q.reshape(b, s, d), k.reshape(b, s, d), v.reshape(b, s, d),
                   row(lam_q1), row(lam_k1), row(lam_q2), row(lam_k2), row(g_subln), lam_init).reshape(n, d)
    hg = _lru(xr, gr, w_conv, b_conv, w_rg, b_rg, w_ig, b_ig, lru_lambda, b, s)

    pad = LANES - N_GROUPS - N_EXPERTS
    w_router = jnp.concatenate(
        [w_rt_group, w_rt_expert.transpose(1, 0, 2).reshape(d, N_EXPERTS), jnp.zeros((d, pad), F32)], axis=1)
    b_router = jnp.concatenate([b_rt_group, b_rt_expert.reshape(N_EXPERTS), jnp.zeros((pad,), F32)]).reshape(1, LANES)
    x1, h2, logits = _merge(o, hg, ga, gb, x2, w_attn_br.astype(BF16), w_lru_br.astype(BF16), w_out.astype(BF16),
                            row(g_moe), w_router, b_router)

    pos, wts, offs = _route(logits)
    pos1, pos2 = pos[:, 0], pos[:, 1]
    rpt = d // LANES
    seg_start, counts = offs[0, :N_EXPERTS + 1], offs[1, :N_EXPERTS]
    hs = _dispatch(seg_start, counts, pos1, pos2, h2, rpt)
    ys = _experts(hs, seg_start, w_e_gate, w_e_up, w_e_down, rpt)
    return pos1, pos2, wts, x1, ys


def kernel(x, p, g_mix, w_in, lam_q1, lam_k1, lam_q2, lam_k2, g_subln, w_conv, b_conv, w_rg, b_rg, w_ig, b_ig, lru_lambda, w_attn_br, w_lru_br, w_out, g_moe, w_rt_group, b_rt_group, w_rt_expert, b_rt_expert, w_e_gate, w_e_up, w_e_down, g_ple, w_ple_gate, w_ple_proj, g_final):
    b, s, d = x.shape
    depth = p.shape[0]
    assert depth == 1, "the final RMSNorm is fused into the last layer's combine step; one layer supported"
    n = b * s
    x2 = x.reshape(n, d)
    i = 0
    p2d = p[i].reshape(n, -1)
    pos1, pos2, wts, x1, ys = _layer(
        i, x2, p2d, b, s, g_mix[i], w_in[i], lam_q1[i], lam_k1[i], lam_q2[i], lam_k2[i], g_subln[i], w_conv[i],
        b_conv[i], w_rg[i], b_rg[i], w_ig[i], b_ig[i], lru_lambda[i], w_attn_br[i], w_lru_br[i], w_out[i],
        g_moe[i], w_rt_group[i], b_rt_group[i], w_rt_expert[i], b_rt_expert[i], w_e_gate[i], w_e_up[i],
        w_e_down[i], g_ple[i], w_ple_gate[i], w_ple_proj[i])
    out = _combine(pos1, pos2, wts, x1, p2d, g_ple[i].reshape(1, d), w_ple_gate[i].astype(BF16),
                   w_ple_proj[i].astype(BF16), g_final.reshape(1, d), ys, d // LANES)
    return out.reshape(b, s, d)
```

```python
import functools
import math

import jax
import jax.numpy as jnp
from jax import lax
from jax.experimental import pallas as pl
from jax.experimental.pallas import tpu as pltpu

F32 = jnp.float32
BF16 = jnp.bfloat16
I32 = jnp.int32

N_HEADS = 8
HEAD_DIM = 64
V_DIM = 2 * HEAD_DIM
SUBLN_EPS = 1e-5
NORM_EPS = 1e-6
LRU_BLOCKS = 16
CONV_WIDTH = 4
LRU_C = 8.0
N_GROUPS = 4
EXPERTS_PER_GROUP = 8
N_EXPERTS = N_GROUPS * EXPERTS_PER_GROUP
N_IN_PARTS = 7
LOG2_E = 1.4426950408889634
Q_SCALE = HEAD_DIM ** -0.5 * LOG2_E

LANES = 128
SUBLANES = 8
MXU_DIM = 256

TM_INPROJ = 512
T_ATTN = 256
T_LRU = 512
TM_MERGE = 512
T_ROUTE = 512
TM_MOE = 256
TM_EXPERT = 256
DMA_ISSUE_UNROLL = 8
MASK_NEG = -1e30


def _sigmoid(x):
    return 0.5 * jnp.tanh(0.5 * x) + 0.5


def _rmsnorm(x, g, eps):
    return (x * lax.rsqrt(jnp.mean(x * x, axis=-1, keepdims=True) + eps)) * g


def _inproj_kernel(x_ref, g_ref, w_ref, *out_refs, d):
    h = _rmsnorm(x_ref[...], g_ref[...], NORM_EPS).astype(BF16)
    for c, o_ref in enumerate(out_refs):
        z = jnp.dot(h, w_ref[:, c * d:(c + 1) * d], preferred_element_type=F32)
        if c == 0:
            z = z * Q_SCALE
        o_ref[...] = z.astype(o_ref.dtype)


def _inproj(x2, g_mix, w_in_bf):
    n, d = x2.shape
    tm = TM_INPROJ
    out_dtypes = (BF16, BF16, BF16, F32, F32, F32, F32)
    row_spec = pl.BlockSpec((tm, d), lambda i: (i, 0))
    return pl.pallas_call(
        functools.partial(_inproj_kernel, d=d),
        grid=(n // tm,),
        in_specs=[
            row_spec,
            pl.BlockSpec((1, d), lambda i: (0, 0)),
            pl.BlockSpec((d, N_IN_PARTS * d), lambda i: (0, 0), pipeline_mode=pl.Buffered(1)),
        ],
        out_specs=[row_spec] * N_IN_PARTS,
        out_shape=[jax.ShapeDtypeStruct((n, d), dt) for dt in out_dtypes],
        compiler_params=pltpu.CompilerParams(dimension_semantics=("arbitrary",)),
        name="inproj",
    )(x2, g_mix, w_in_bf)


V_AUG = V_DIM + 16
PIPE_GROUP = 4
PIPE_LAG = 2
PIPE_RING = 2 * PIPE_LAG
PIPE_SLOTS = PIPE_GROUP * PIPE_RING


def _attn_kernel(lq1_ref, lk1_ref, lq2_ref, lk2_ref, g_ref, q_ref, k_ref, v_ref, o_ref,
                 qt_scr, vt_scr, acc_scr, m_scr, bias_scr, s_scr, p_scr, cmax_scr, alpha_scr, *, t, lam_init):
    nq = q_ref.shape[0] // t
    w = 2 * t

    feat = lax.broadcasted_iota(I32, (t, V_DIM), 1)
    ones_pad = jnp.where(lax.broadcasted_iota(I32, (V_AUG - V_DIM, t), 0) == 0, 1.0, 0.0).astype(F32)

    def prep(i, carry):
        rows = pl.ds(pl.multiple_of(i * t, t), t)
        q = q_ref[rows, :].astype(F32)
        q1t = jnp.where(feat < HEAD_DIM, q, 0.0).T
        q2t = jnp.where(feat >= HEAD_DIM, q, 0.0).T
        qt_scr[i] = jnp.concatenate([q1t, q2t], axis=1).astype(BF16)
        vt = v_ref[rows, :].astype(F32).T
        vt_scr[i] = jnp.concatenate([vt, ones_pad], axis=0).astype(BF16)
        m_scr[i] = jnp.full((1, w), MASK_NEG, F32)
        acc_scr[i] = jnp.zeros((V_AUG, w), F32)
        return carry

    lax.fori_loop(0, nq, prep, 0, unroll=8)
    key = lax.broadcasted_iota(I32, (t, w), 0)
    qry = lax.broadcasted_iota(I32, (t, w), 1) & (t - 1)
    bias_scr[...] = jnp.where(key <= qry, 0.0, MASK_NEG)

    def stage_a(j, i, slot, masked):
        kj = k_ref[pl.ds(pl.multiple_of(j * t, t), t), :]
        s = jnp.dot(kj, qt_scr[i], preferred_element_type=F32)
        if masked:
            s = s + bias_scr[...]
        s_scr[slot] = s
        cmax_scr[slot] = jnp.max(s, axis=0, keepdims=True)

    def stage_b(i, slot):
        m_old = m_scr[i]
        m_new = jnp.maximum(m_old, cmax_scr[slot])
        m_scr[i] = m_new
        p_scr[slot] = jnp.exp2(s_scr[slot] - m_new).astype(BF16)
        alpha_scr[slot] = jnp.exp2(m_old - m_new)

    def stage_c(j, i, slot):
        acc_scr[i] = alpha_scr[slot] * acc_scr[i] + jnp.dot(vt_scr[j], p_scr[slot], preferred_element_type=F32)

    def run(n_steps, n_masked, first, nxt):
        body_steps = PIPE_RING * PIPE_GROUP
        assert n_steps % body_steps == 0 and n_masked % body_steps == 0 and n_masked >= body_steps
        n_groups = n_steps // PIPE_GROUP
        n_masked_groups = n_masked // PIPE_GROUP

        def steps_of(st):
            out = []
            for _ in range(PIPE_GROUP):
                out.append(st)
                st = nxt(*st)
            return out, st

        def turn(t_mod, a=None, b=None, c=None, masked=False):
            base_a = (t_mod % PIPE_RING) * PIPE_GROUP
            base_b = ((t_mod - PIPE_LAG) % PIPE_RING) * PIPE_GROUP
            a_steps, following = steps_of(a) if a is not None else (None, None)
            b_steps = steps_of(b)[0] if b is not None else None
            c_steps = steps_of(c)[0] if c is not None else None
            for kk in range(PIPE_GROUP):
                if a_steps is not None:
                    stage_a(*a_steps[kk], base_a + kk, masked)
            for kk in range(PIPE_GROUP):
                if b_steps is not None:
                    stage_b(b_steps[kk][1], base_b + kk)
            for kk in range(PIPE_GROUP):
                if c_steps is not None:
                    stage_c(*c_steps[kk], base_a + kk)
            return following

        f = [first]
        for tt in range(PIPE_RING):
            f.append(turn(tt, a=f[tt], b=f[tt - PIPE_LAG] if tt >= PIPE_LAG else None, masked=True))

        def ring(_, f, masked):
            f = list(f)
            for r in range(PIPE_RING):
                f.append(turn(r, a=f[-1], b=f[-1 - PIPE_LAG], c=f[-1 - PIPE_RING], masked=masked))
            return tuple(f[-(PIPE_RING + 1):])

        f = lax.fori_loop(0, (n_masked_groups - PIPE_RING) // PIPE_RING, functools.partial(ring, masked=True), tuple(f))
        f = lax.fori_loop(0, (n_groups - n_masked_groups) // PIPE_RING, functools.partial(ring, masked=False), f)
        for r in range(PIPE_RING):
            turn(r, b=f[r + PIPE_LAG] if r < PIPE_LAG else None, c=f[r])

    def next_step(j, i):
        wrap = i + 1 >= nq
        return jnp.where(wrap, 0, j + 1), jnp.where(wrap, i - j + 1, i + 1)

    zero = jnp.int32(0)
    run(nq * (nq + 1) // 2, nq, (zero, zero), next_step)

    lam = (jnp.exp(jnp.sum(lq1_ref[...] * lk1_ref[...], axis=1, keepdims=True))
           - jnp.exp(jnp.sum(lq2_ref[...] * lk2_ref[...], axis=1, keepdims=True)) + lam_init)
    gain = g_ref[...]

    def finish(i, carry):
        acc = acc_scr[i]
        o1 = acc[:V_DIM, :t] / acc[V_DIM:V_DIM + 1, :t]
        o2 = acc[:V_DIM, t:] / acc[V_DIM:V_DIM + 1, t:]
        o = o1 - lam * o2
        y = ((o * lax.rsqrt(jnp.mean(o * o, axis=0, keepdims=True) + SUBLN_EPS)) * gain) * (1.0 - lam_init)
        o_ref[pl.ds(pl.multiple_of(i * t, t), t), :] = y.T.astype(o_ref.dtype)
        return carry

    lax.fori_loop(0, nq, finish, 0, unroll=8)


def _attention(q, k, v, lam_q1, lam_k1, lam_q2, lam_k2, g_subln, lam_init):
    b, s, d = q.shape
    t = T_ATTN
    assert s % t == 0 and d == N_HEADS * V_DIM
    nq = s // t
    small = lambda w: pl.BlockSpec((1, w), lambda bi, hi: (0, 0))
    head = pl.BlockSpec((None, s, V_DIM), lambda bi, hi: (bi, 0, hi))
    return pl.pallas_call(
        functools.partial(_attn_kernel, t=t, lam_init=lam_init),
        grid=(b, N_HEADS),
        in_specs=[small(HEAD_DIM), small(HEAD_DIM), small(HEAD_DIM), small(HEAD_DIM),
                  pl.BlockSpec((V_DIM, 1), lambda bi, hi: (0, 0)), head, head, head],
        out_specs=head,
        out_shape=jax.ShapeDtypeStruct((b, s, d), BF16),
        scratch_shapes=[
            pltpu.VMEM((nq, V_DIM, 2 * t), BF16),
            pltpu.VMEM((nq, V_AUG, t), BF16),
            pltpu.VMEM((nq, V_AUG, 2 * t), F32),
            pltpu.VMEM((nq, 1, 2 * t), F32),
            pltpu.VMEM((t, 2 * t), F32),
            pltpu.VMEM((PIPE_SLOTS,t, 2 * t), F32),
            pltpu.VMEM((PIPE_SLOTS,t, 2 * t), BF16),
            pltpu.VMEM((PIPE_SLOTS,1, 2 * t), F32),
            pltpu.VMEM((PIPE_SLOTS,1, 2 * t), F32),
        ],
        compiler_params=pltpu.CompilerParams(dimension_semantics=("arbitrary", "arbitrary")),
        name="attn",
    )(lam_q1, lam_k1, lam_q2, lam_k2, g_subln.reshape(V_DIM, 1), q, k, v)


def _lru_kernel(xr_ref, gr_ref, wc_ref, bc_ref, wrg_ref, brg_ref, wig_ref, big_ref, lam_ref, o_ref,
                xprev_ref, hcar_ref, a_scr, u_scr, *, t_tile):
    @pl.when(pl.program_id(1) == 0)
    def _():
        xprev_ref[...] = jnp.zeros_like(xprev_ref)
        hcar_ref[...] = jnp.zeros_like(hcar_ref)

    x = xr_ref[...]
    w = x.shape[1]
    prev = xprev_ref[...]
    wc = wc_ref[...]
    row8 = lax.broadcasted_iota(I32, (SUBLANES, w), 0)
    xc = bc_ref[...]
    for j in range(CONV_WIDTH):
        shift = CONV_WIDTH - 1 - j
        if shift == 0:
            xs = x
        else:
            rolled = pltpu.roll(x, shift, 0)
            head = jnp.where(row8 < shift, pltpu.roll(prev, shift, 0), rolled[:SUBLANES])
            xs = jnp.concatenate([head, rolled[SUBLANES:]], axis=0)
        xc = xc + xs * wc[j:j + 1]
    xprev_ref[...] = x[t_tile - SUBLANES:]

    xcb = xc.astype(BF16)
    n_grp = w // MXU_DIM
    pre_r = jnp.concatenate(
        [jnp.dot(xcb[:, g * MXU_DIM:(g + 1) * MXU_DIM], wrg_ref[g], preferred_element_type=F32) for g in range(n_grp)],
        axis=1) + brg_ref[...]
    pre_i = jnp.concatenate(
        [jnp.dot(xcb[:, g * MXU_DIM:(g + 1) * MXU_DIM], wig_ref[g], preferred_element_type=F32) for g in range(n_grp)],
        axis=1) + big_ref[...]
    r = _sigmoid(pre_r)
    ig = _sigmoid(pre_i)
    neg_lam = -lam_ref[...]
    softplus = jnp.maximum(neg_lam, 0.0) + jnp.log1p(jnp.exp(-jnp.abs(neg_lam)))
    log_a = (-LRU_C) * r * softplus
    a = jnp.exp(log_a)
    th = jnp.tanh(log_a)
    num = -2.0 * th
    mult = jnp.where(num > 0.0, num * lax.rsqrt(num), 0.0) * lax.rsqrt(1.0 - th)
    u = mult * (ig * xc)

    def scan_steps(a, u, pos, length, axis):
        dist = 1
        while dist < length:
            keep = pos >= dist
            u = jnp.where(keep, a * pltpu.roll(u, dist, axis) + u, u)
            a = jnp.where(keep, a * pltpu.roll(a, dist, axis), a)
            dist *= 2
        return a, u

    n_grp_t = t_tile // SUBLANES
    grouped = (n_grp_t, SUBLANES, w)
    a, u = scan_steps(a.reshape(grouped), u.reshape(grouped), lax.broadcasted_iota(I32, grouped, 1), SUBLANES, 1)
    a, u = a.reshape(x.shape), u.reshape(x.shape)
    n_lt = w // LANES
    for c in range(n_lt):
        a_scr[c] = a[:, c * LANES:(c + 1) * LANES]
        u_scr[c] = u[:, c * LANES:(c + 1) * LANES]
    last = pl.ds(SUBLANES - 1, n_grp_t, stride=SUBLANES)
    grow = lax.broadcasted_iota(I32, (n_grp_t, w), 0)
    ag, ug = scan_steps(jnp.concatenate([a_scr[c, last, :] for c in range(n_lt)], axis=1),
                        jnp.concatenate([u_scr[c, last, :] for c in range(n_lt)], axis=1), grow, n_grp_t, 0)
    h_end = ag * hcar_ref[...] + ug
    h_in = jnp.where(grow == 0, hcar_ref[...], pltpu.roll(h_end, 1, 0))
    hcar_ref[...] = h_end[n_grp_t - 1:]
    for c in range(n_lt):
        for k in range(SUBLANES):
            a_scr[c, pl.ds(k, n_grp_t, stride=SUBLANES), :] = h_in[:, c * LANES:(c + 1) * LANES]
    h = a * jnp.concatenate([a_scr[c] for c in range(n_lt)], axis=1) + u
    o_ref[...] = (h * jax.nn.gelu(gr_ref[...])).astype(o_ref.dtype)


def _block_diag_groups(wblk):
    nb, bw, _ = wblk.shape
    per = MXU_DIM // bw
    g = nb // per
    eye = jnp.eye(per, dtype=wblk.dtype)
    w5 = wblk.reshape(g, per, bw, bw)
    return jnp.einsum("gawv,ab->gawbv", w5, eye).reshape(g, MXU_DIM, MXU_DIM).astype(BF16)


def _lru(xr, gr, w_conv, b_conv, w_rg, b_rg, w_ig, b_ig, lru_lambda, b, s):
    n, w = xr.shape
    t = T_LRU
    n_t = s // t
    row_spec = pl.BlockSpec((t, w), lambda bi, ti: (bi * n_t + ti, 0))
    vec = lambda r: pl.BlockSpec((r, w), lambda bi, ti: (0, 0))
    blk = pl.BlockSpec((w // MXU_DIM, MXU_DIM, MXU_DIM), lambda bi, ti: (0, 0, 0))
    return pl.pallas_call(
        functools.partial(_lru_kernel, t_tile=t),
        grid=(b, n_t),
        in_specs=[row_spec, row_spec, vec(CONV_WIDTH), vec(1), blk, vec(1), blk, vec(1), vec(1)],
        out_specs=row_spec,
        out_shape=jax.ShapeDtypeStruct((n, w), BF16),
        scratch_shapes=[pltpu.VMEM((SUBLANES, w), F32), pltpu.VMEM((1, w), F32),
                        pltpu.VMEM((w // LANES, t, LANES), F32), pltpu.VMEM((w // LANES, t, LANES), F32)],
        compiler_params=pltpu.CompilerParams(dimension_semantics=("arbitrary", "arbitrary")),
        name="lru",
    )(xr, gr, w_conv, b_conv.reshape(1, w), _block_diag_groups(w_rg), b_rg.reshape(1, w),
      _block_diag_groups(w_ig), b_ig.reshape(1, w), lru_lambda.reshape(1, w))


def _split_bf16(x):
    hi = x.astype(BF16)
    return hi, (x - hi.astype(F32)).astype(BF16)


def _to_token_tiles(ref, x):
    tm, d = x.shape
    rpt = d // LANES
    for c in range(rpt):
        ref[pl.ds(c, tm, stride=rpt), :] = x[:, c * LANES:(c + 1) * LANES]


def _from_token_tiles(ref, tm, rpt):
    return jnp.concatenate([ref[pl.ds(c, tm, stride=rpt), :] for c in range(rpt)], axis=1)


def _merge_kernel(o_ref, hg_ref, ga_ref, gb_ref, x_ref, wa_ref, wl_ref, wo_ref, gm_ref, wrh_ref, wrl_ref, br_ref,
                  x1_ref, h2_ref, lg_ref):
    attn_br = jnp.dot(o_ref[...], wa_ref[...], preferred_element_type=F32)
    lru_br = jnp.dot(hg_ref[...], wl_ref[...], preferred_element_type=F32)
    mixed = _sigmoid(ga_ref[...]) * attn_br + _sigmoid(gb_ref[...]) * lru_br
    x1 = x_ref[...] + jnp.dot(mixed.astype(BF16), wo_ref[...], preferred_element_type=F32)
    x1_ref[...] = x1
    h2 = _rmsnorm(x1, gm_ref[...], NORM_EPS)
    _to_token_tiles(h2_ref, h2)
    hi, lo = _split_bf16(h2)
    wrh = wrh_ref[...]
    lg_ref[...] = (jnp.dot(hi, wrh, preferred_element_type=F32) + jnp.dot(lo, wrh, preferred_element_type=F32)
                   + jnp.dot(hi, wrl_ref[...], preferred_element_type=F32)) + br_ref[...]


def _merge(o, hg, ga, gb, x2, wa, wl, wo, g_moe, w_router, b_router):
    n, d = x2.shape
    tm = TM_MERGE
    rpt = d // LANES
    row = pl.BlockSpec((tm, d), lambda i: (i, 0))
    full = lambda r, c: pl.BlockSpec((r, c), lambda i: (0, 0))
    wr_hi, wr_lo = _split_bf16(w_router)
    return pl.pallas_call(
        _merge_kernel,
        grid=(n // tm,),
        in_specs=[row, row, row, row, row, full(d, d), full(d, d), full(d, d), full(1, d),
                  full(d, LANES), full(d, LANES), full(1, LANES)],
        out_specs=[row, pl.BlockSpec((tm * rpt, LANES), lambda i: (i, 0)), pl.BlockSpec((tm, LANES), lambda i: (i, 0))],
        out_shape=[jax.ShapeDtypeStruct((n, d), F32), jax.ShapeDtypeStruct((n * rpt, LANES), F32),
                   jax.ShapeDtypeStruct((n, LANES), F32)],
        compiler_params=pltpu.CompilerParams(dimension_semantics=("arbitrary",)),
        name="merge",
    )(o, hg, ga, gb, x2, wa, wl, wo, g_moe, wr_hi, wr_lo, b_router)


def _lane_pick(x, lane, idx):
    return jnp.sum(jnp.where(lane == idx, x, jnp.zeros_like(x)), axis=1, keepdims=True)


def _route_kernel(lg_ref, pos_ref, wts_ref, offs_ref, *, n, t_tile, seg_tile):
    shape = (t_tile, LANES)
    lane = lax.broadcasted_iota(I32, shape, 1)
    neg_inf = jnp.float32(-jnp.inf)
    tri = (lax.broadcasted_iota(I32, (t_tile, t_tile), 0) > lax.broadcasted_iota(I32, (t_tile, t_tile), 1)).astype(BF16)

    def phase1(t, cnt):
        rows = pl.ds(pl.multiple_of(t * t_tile, t_tile), t_tile)
        lg = lg_ref[rows, :]
        is_grp = lane < N_GROUPS
        gl = jnp.where(is_grp, lg, neg_inf)
        gmax = jnp.max(gl, axis=1, keepdims=True)
        g_idx = jnp.min(jnp.where(gl == gmax, lane, LANES), axis=1, keepdims=True)
        g_w = 1.0 / jnp.sum(jnp.where(is_grp, jnp.exp(lg - gmax), 0.0), axis=1, keepdims=True)
        lo = N_GROUPS + EXPERTS_PER_GROUP * g_idx
        in_grp = (lane >= lo) & (lane < lo + EXPERTS_PER_GROUP)
        fl = jnp.where(in_grp, lg, neg_inf)
        v1 = jnp.max(fl, axis=1, keepdims=True)
        i1 = jnp.min(jnp.where(in_grp & (fl == v1), lane, LANES), axis=1, keepdims=True)
        rest = in_grp & (lane != i1)
        fl2 = jnp.where(rest, lg, neg_inf)
        v2 = jnp.max(fl2, axis=1, keepdims=True)
        i2 = jnp.min(jnp.where(rest & (fl2 == v2), lane, LANES), axis=1, keepdims=True)
        t2 = jnp.exp(v2 - v1)
        den = 1.0 + t2
        w1 = g_w * (1.0 / den)
        w2 = g_w * (t2 / den)
        e1 = i1 - N_GROUPS
        e2 = i2 - N_GROUPS
        onehot = ((lane == e1) | (lane == e2)).astype(F32)
        before = jnp.dot(tri, onehot.astype(BF16), preferred_element_type=F32) + cnt
        rank1 = _lane_pick(before, lane, e1)
        rank2 = _lane_pick(before, lane, e2)
        pos_ref[rows, :] = jnp.where(lane == 0, e1, jnp.where(lane == 1, e2, jnp.where(
            lane == 2, rank1.astype(I32), jnp.where(lane == 3, rank2.astype(I32), 0))))
        wts_ref[rows, :] = jnp.where(lane == 0, w1, jnp.where(lane == 1, w2, 0.0))
        return cnt + jnp.sum(onehot, axis=0, keepdims=True)

    cnt = lax.fori_loop(0, n // t_tile, phase1, jnp.zeros((1, LANES), F32))

    lane8 = lax.broadcasted_iota(I32, (SUBLANES, LANES), 1)
    row8 = lax.broadcasted_iota(I32, (SUBLANES, LANES), 0)
    cnt8 = jnp.broadcast_to(cnt, (SUBLANES, LANES))
    padded = jnp.ceil(cnt8 * (1.0 / seg_tile)) * seg_tile
    incl = padded
    dist = 1
    while dist < LANES:
        incl = incl + jnp.where(lane8 >= dist, pltpu.roll(incl, dist, 1), 0.0)
        dist *= 2
    offs = incl - padded
    offs_ref[...] = jnp.where(row8 == 0, offs, jnp.where(row8 == 1, cnt8, 0.0)).astype(I32)
    offs_row = offs[0:1]

    def phase2(t, carry):
        rows = pl.ds(pl.multiple_of(t * t_tile, t_tile), t_tile)
        info = pos_ref[rows, :]
        e1 = _lane_pick(info, lane, 0)
        e2 = _lane_pick(info, lane, 1)
        r1 = _lane_pick(info, lane, 2)
        r2 = _lane_pick(info, lane, 3)
        offs_b = jnp.broadcast_to(offs_row, shape).astype(I32)
        p1 = r1 + _lane_pick(offs_b, lane, e1)
        p2 = r2 + _lane_pick(offs_b, lane, e2)
        pos_ref[rows, :] = jnp.where(lane == 0, p1, jnp.where(lane == 1, p2, 0))
        return carry

    lax.fori_loop(0, n // t_tile, phase2, 0)


def _route(logits):
    n = logits.shape[0]
    return pl.pallas_call(
        functools.partial(_route_kernel, n=n, t_tile=T_ROUTE, seg_tile=TM_EXPERT),
        out_shape=[jax.ShapeDtypeStruct((n, LANES), I32), jax.ShapeDtypeStruct((n, LANES), F32),
                   jax.ShapeDtypeStruct((SUBLANES, LANES), I32)],
        name="route",
    )(logits)


def _token_rows(idx, rpt):
    return pl.ds(pl.multiple_of(idx * rpt, rpt), rpt)


def _dispatch_kernel(seg_ref, cnt_ref, p1_ref, p2_ref, h_ref, hs_ref, zero_scr, sem, *, tm, rpt, n_tiles,
                     seg_tile, cap_tiles):
    def issue(r, carry):
        src = h_ref.at[_token_rows(r, rpt)]
        pltpu.make_async_copy(src, hs_ref.at[_token_rows(p1_ref[r], rpt)], sem).start(priority=0)
        pltpu.make_async_copy(src, hs_ref.at[_token_rows(p2_ref[r], rpt)], sem).start(priority=1)
        return carry

    lax.fori_loop(0, tm, issue, 0, unroll=DMA_ISSUE_UNROLL)
    whole = pltpu.make_async_copy(h_ref, hs_ref.at[pl.ds(0, tm * rpt)], sem)
    whole.wait()
    whole.wait()

    @pl.when(pl.program_id(0) == n_tiles - 1)
    def _():
        zero_scr[...] = jnp.zeros_like(zero_scr)
        zero_row = zero_scr.at[pl.ds(0, rpt)]

        def per_expert(e, carry):
            first_pad = seg_ref[e] + cnt_ref[e]
            n_pad = seg_ref[e + 1] - first_pad

            def fill(r, c):
                pltpu.make_async_copy(zero_row, hs_ref.at[_token_rows(first_pad + r, rpt)], sem).start()
                return c

            def drain(r, c):
                pltpu.make_async_copy(zero_row, hs_ref.at[_token_rows(first_pad + r, rpt)], sem).wait()
                return c

            lax.fori_loop(0, n_pad, fill, 0)
            lax.fori_loop(0, n_pad, drain, 0)
            return carry

        lax.fori_loop(0, N_EXPERTS, per_expert, 0)

        used_tiles = seg_ref[N_EXPERTS] // seg_tile

        def tail_tile(c):
            return hs_ref.at[pl.ds(pl.multiple_of((used_tiles + c) * (seg_tile * rpt), seg_tile * rpt), seg_tile * rpt)]

        def fill_tail(c, carry):
            pltpu.make_async_copy(zero_scr, tail_tile(c), sem).start()
            return carry

        def drain_tail(c, carry):
            pltpu.make_async_copy(zero_scr, tail_tile(c), sem).wait()
            return carry

        lax.fori_loop(0, cap_tiles - used_tiles, fill_tail, 0)
        lax.fori_loop(0, cap_tiles - used_tiles, drain_tail, 0)


def _dispatch(seg_start, counts, pos1, pos2, h2t, rpt):
    n = h2t.shape[0] // rpt
    tm = TM_MOE
    n_tiles = n // tm
    idx = pl.BlockSpec((tm,), lambda i, *_: (i,), memory_space=pltpu.SMEM)
    grid_spec = pltpu.PrefetchScalarGridSpec(
        num_scalar_prefetch=2,
        grid=(n_tiles,),
        in_specs=[idx, idx, pl.BlockSpec((tm * rpt, LANES), lambda i, *_: (i, 0))],
        out_specs=pl.BlockSpec(memory_space=pl.ANY),
        scratch_shapes=[pltpu.VMEM((TM_EXPERT * rpt, LANES), F32), pltpu.SemaphoreType.DMA(())],
    )
    cap_tiles = 2 * n // TM_EXPERT + N_EXPERTS
    return pl.pallas_call(
        functools.partial(_dispatch_kernel, tm=tm, rpt=rpt, n_tiles=n_tiles, seg_tile=TM_EXPERT, cap_tiles=cap_tiles),
        grid_spec=grid_spec,
        out_shape=jax.ShapeDtypeStruct((cap_tiles * TM_EXPERT * rpt, LANES), F32),
        compiler_params=pltpu.CompilerParams(dimension_semantics=("arbitrary",), has_side_effects=True),
        name="dispatch",
    )(seg_start, counts, pos1, pos2, h2t)


def _expert_kernel(t_blk, t_exp, t_valid, t_new, t_slot, t_next, hs_ref, wg_hbm, wu_hbm, wd_hbm, ys_ref,
                   wg_f32, wu_f32, wd_f32, wg_bf, wu_bf, wd_bf, sems, *, tm, rpt):
    t = pl.program_id(0)

    def weight_copies(e, slot):
        return [pltpu.make_async_copy(src.at[e], dst.at[slot], sems.at[slot])
                for src, dst in ((wg_hbm, wg_f32), (wu_hbm, wu_f32), (wd_hbm, wd_f32))]

    @pl.when(t_valid[t] == 1)
    def _():
        @pl.when(t_new[t] == 1)
        def _():
            slot = t_slot[t]

            @pl.when(t == 0)
            def _():
                for c in weight_copies(t_exp[t], slot):
                    c.start()

            for c in weight_copies(t_exp[t], slot):
                c.wait()
            wg_bf[...] = wg_f32[slot].astype(BF16)
            wu_bf[...] = wu_f32[slot].astype(BF16)
            wd_bf[...] = wd_f32[slot].astype(BF16)

            @pl.when(t_next[t] >= 0)
            def _():
                for c in weight_copies(t_next[t], 1 - slot):
                    c.start()

        h = _from_token_tiles(hs_ref, tm, rpt).astype(BF16)
        gate = jnp.dot(h, wg_bf[...], preferred_element_type=F32)
        up = jnp.dot(h, wu_bf[...], preferred_element_type=F32)
        hid = ((gate * _sigmoid(gate)) * up).astype(BF16)
        _to_token_tiles(ys_ref, jnp.dot(hid, wd_bf[...], preferred_element_type=F32))

    @pl.when(t_valid[t] == 0)
    def _():
        ys_ref[...] = jnp.zeros_like(ys_ref)


def _tile_table(seg_start, n_tiles, tm):
    total = seg_start[N_EXPERTS] // tm
    t = jnp.arange(n_tiles, dtype=I32)
    blk = jnp.minimum(t, total - 1)
    exp = jnp.sum(seg_start[None, 1:N_EXPERTS + 1] <= (blk * tm)[:, None], axis=1).astype(I32)
    valid = (t < total).astype(I32)
    new = (exp != jnp.concatenate([jnp.full((1,), -1, I32), exp[:-1]])).astype(I32)
    slot = (jnp.cumsum(new) - 1) % 2
    later = jnp.where(exp[None, :] > exp[:, None], exp[None, :], N_EXPERTS)
    nxt = jnp.min(later, axis=1)
    nxt = jnp.where(nxt < N_EXPERTS, nxt, -1).astype(I32)
    return blk, exp, valid, new, slot.astype(I32), nxt


def _experts(hs, seg_start, w_e_gate, w_e_up, w_e_down, rpt):
    d, ff = w_e_gate.shape[-2:]
    tm = TM_EXPERT
    n_tiles = hs.shape[0] // (rpt * tm)
    table = _tile_table(seg_start, n_tiles, tm)
    hbm = pl.BlockSpec(memory_space=pl.ANY)
    grid_spec = pltpu.PrefetchScalarGridSpec(
        num_scalar_prefetch=len(table),
        grid=(n_tiles,),
        in_specs=[pl.BlockSpec((tm * rpt, LANES), lambda t, b, *_: (b[t], 0)), hbm, hbm, hbm],
        out_specs=pl.BlockSpec((tm * rpt, LANES), lambda t, *_: (t, 0)),
        scratch_shapes=[pltpu.VMEM((2, d, ff), F32), pltpu.VMEM((2, d, ff), F32), pltpu.VMEM((2, ff, d), F32),
                        pltpu.VMEM((d, ff), BF16), pltpu.VMEM((d, ff), BF16), pltpu.VMEM((ff, d), BF16),
                        pltpu.SemaphoreType.DMA((2,))],
    )
    return pl.pallas_call(
        functools.partial(_expert_kernel, tm=tm, rpt=rpt),
        grid_spec=grid_spec,
        out_shape=jax.ShapeDtypeStruct(hs.shape, F32),
        compiler_params=pltpu.CompilerParams(dimension_semantics=("arbitrary",)),
        name="experts",
    )(*table, hs, w_e_gate, w_e_up, w_e_down)


def _combine_kernel(p1_ref, p2_ref, p1n_ref, p2n_ref, wts_ref, x1_ref, p_ref, gp_ref, wpg_ref, wpp_ref, gf_ref,
                    ys_ref, out_ref, ybuf, sems, *, tm, rpt, n_tiles):
    i = pl.program_id(0)
    slot = i % 2

    def gather(a_ref, b_ref, dst):
        def issue(r, carry):
            rows = _token_rows(r, rpt)
            pltpu.make_async_copy(ys_ref.at[_token_rows(a_ref[r], rpt)], ybuf.at[dst, 0, rows],
                                  sems.at[dst]).start(priority=0)
            pltpu.make_async_copy(ys_ref.at[_token_rows(b_ref[r], rpt)], ybuf.at[dst, 1, rows],
                                  sems.at[dst]).start(priority=1)
            return carry

        lax.fori_loop(0, tm, issue, 0, unroll=DMA_ISSUE_UNROLL)

    @pl.when(i == 0)
    def _():
        gather(p1_ref, p2_ref, 0)

    @pl.when(i + 1 < n_tiles)
    def _():
        gather(p1n_ref, p2n_ref, 1 - slot)

    for half in range(2):
        pltpu.make_async_copy(ys_ref.at[pl.ds(0, tm * rpt)], ybuf.at[slot, half], sems.at[slot]).wait()

    wts = wts_ref[...]
    y1 = _from_token_tiles(ybuf.at[slot, 0], tm, rpt)
    y2 = _from_token_tiles(ybuf.at[slot, 1], tm, rpt)
    x2 = x1_ref[...] + (wts[:, 0:1] * y1 + wts[:, 1:2] * y2)
    hp = _rmsnorm(x2, gp_ref[...], NORM_EPS).astype(BF16)
    gate = _sigmoid(jnp.dot(hp, wpg_ref[...], preferred_element_type=F32))
    proj = jnp.dot(p_ref[...].astype(BF16), wpp_ref[...], preferred_element_type=F32)
    x3 = x2 + gate * proj
    out_ref[...] = _rmsnorm(x3, gf_ref[...], NORM_EPS)


def _combine(pos1, pos2, wts, x1, p2d, g_ple, wpg, wpp, g_final, ys, rpt):
    n, d = x1.shape
    pd = p2d.shape[1]
    tm = TM_MOE
    n_tiles = n // tm
    idx = pl.BlockSpec((tm,), lambda i: (i,), memory_space=pltpu.SMEM)
    idx_next = pl.BlockSpec((tm,), lambda i: (jnp.minimum(i + 1, n_tiles - 1),), memory_space=pltpu.SMEM)
    row = pl.BlockSpec((tm, d), lambda i: (i, 0))
    full = lambda r, c: pl.BlockSpec((r, c), lambda i: (0, 0))
    return pl.pallas_call(
        functools.partial(_combine_kernel, tm=tm, rpt=rpt, n_tiles=n_tiles),
        grid=(n_tiles,),
        in_specs=[idx, idx, idx_next, idx_next, pl.BlockSpec((tm, LANES), lambda i: (i, 0)), row,
                  pl.BlockSpec((tm, pd), lambda i: (i, 0)), full(1, d), full(d, d), full(pd, d), full(1, d),
                  pl.BlockSpec(memory_space=pl.ANY)],
        out_specs=row,
        out_shape=jax.ShapeDtypeStruct((n, d), F32),
        scratch_shapes=[pltpu.VMEM((2, 2, tm * rpt, LANES), F32), pltpu.SemaphoreType.DMA((2,))],
        compiler_params=pltpu.CompilerParams(dimension_semantics=("arbitrary",)),
        name="combine",
    )(pos1, pos2, pos1, pos2, wts, x1, p2d, g_ple, wpg, wpp, g_final, ys)


def _layer(i, x2, p2d, b, s, g_mix, w_in, lam_q1, lam_k1, lam_q2, lam_k2, g_subln, w_conv, b_conv, w_rg, b_rg,
           w_ig, b_ig, lru_lambda, w_attn_br, w_lru_br, w_out, g_moe, w_rt_group, b_rt_group, w_rt_expert,
           b_rt_expert, w_e_gate, w_e_up, w_e_down, g_ple, w_ple_gate, w_ple_proj):
    n, d = x2.shape
    lam_init = 0.8 - 0.6 * math.exp(-0.3 * i)
    row = lambda a: a.reshape(1, -1)

    q, k, v, xr, gr, ga, gb = _inproj(x2, row(g_mix), w_in.astype(BF16))
    o = _attention(q.reshape(b, s, d), k.reshape(b, s, d), v.reshape(b, s, d),
                   row(lam_q1), row(lam_k1), row(lam_q2), row(lam_k2), row(g_subln), lam_init).reshape(n, d)
    hg = _lru(xr, gr, w_conv, b_conv, w_rg, b_rg, w_ig, b_ig, lru_lambda, b, s)

    pad = LANES - N_GROUPS - N_EXPERTS
    w_router = jnp.concatenate(
        [w_rt_group, w_rt_expert.transpose(1, 0, 2).reshape(d, N_EXPERTS), jnp.zeros((d, pad), F32)], axis=1)
    b_router = jnp.concatenate([b_rt_group, b_rt_expert.reshape(N_EXPERTS), jnp.zeros((pad,), F32)]).reshape(1, LANES)
    x1, h2, logits = _merge(o, hg, ga, gb, x2, w_attn_br.astype(BF16), w_lru_br.astype(BF16), w_out.astype(BF16),
                            row(g_moe), w_router, b_router)

    pos, wts, offs = _route(logits)
    pos1, pos2 = pos[:, 0], pos[:, 1]
    rpt = d // LANES
    seg_start, counts = offs[0, :N_EXPERTS + 1], offs[1, :N_EXPERTS]
    hs = _dispatch(seg_start, counts, pos1, pos2, h2, rpt)
    ys = _experts(hs, seg_start, w_e_gate, w_e_up, w_e_down, rpt)
    return pos1, pos2, wts, x1, ys


def kernel(x, p, g_mix, w_in, lam_q1, lam_k1, lam_q2, lam_k2, g_subln, w_conv, b_conv, w_rg, b_rg, w_ig, b_ig, lru_lambda, w_attn_br, w_lru_br, w_out, g_moe, w_rt_group, b_rt_group, w_rt_expert, b_rt_expert, w_e_gate, w_e_up, w_e_down, g_ple, w_ple_gate, w_ple_proj, g_final):
    b, s, d = x.shape
    depth = p.shape[0]
    assert depth == 1, "the final RMSNorm is fused into the last layer's combine step; one layer supported"
    n = b * s
    x2 = x.reshape(n, d)
    i = 0
    p2d = p[i].reshape(n, -1)
    pos1, pos2, wts, x1, ys = _layer(
        i, x2, p2d, b, s, g_mix[i], w_in[i], lam_q1[i], lam_k1[i], lam_q2[i], lam_k2[i], g_subln[i], w_conv[i],
        b_conv[i], w_rg[i], b_rg[i], w_ig[i], b_ig[i], lru_lambda[i], w_attn_br[i], w_lru_br[i], w_out[i],
        g_moe[i], w_rt_group[i], b_rt_group[i], w_rt_expert[i], b_rt_expert[i], w_e_gate[i], w_e_up[i],
        w_e_down[i], g_ple[i], w_ple_gate[i], w_ple_proj[i])
    out = _combine(pos1, pos2, wts, x1, p2d, g_ple[i].reshape(1, d), w_ple_gate[i].astype(BF16),
                   w_ple_proj[i].astype(BF16), g_final.reshape(1, d), ys, d // LANES)
    return out.reshape(b, s, d)
```

```python
import functools
import math

import jax
import jax.numpy as jnp
from jax import lax
from jax.experimental import pallas as pl
from jax.experimental.pallas import tpu as pltpu

F32 = jnp.float32
BF16 = jnp.bfloat16
I32 = jnp.int32

N_HEADS = 8
HEAD_DIM = 64
V_DIM = 2 * HEAD_DIM
SUBLN_EPS = 1e-5
NORM_EPS = 1e-6
LRU_BLOCKS = 16
CONV_WIDTH = 4
LRU_C = 8.0
N_GROUPS = 4
EXPERTS_PER_GROUP = 8
N_EXPERTS = N_GROUPS * EXPERTS_PER_GROUP
N_IN_PARTS = 7
LOG2_E = 1.4426950408889634
Q_SCALE = HEAD_DIM ** -0.5 * LOG2_E

LANES = 128
SUBLANES = 8
MXU_DIM = 256

TM_INPROJ = 512
T_ATTN = 256
T_LRU = 512
TM_MERGE = 512
T_ROUTE = 512
TM_MOE = 256
TM_EXPERT = 256
DMA_ISSUE_UNROLL = 8
MASK_NEG = -1e30


def _sigmoid(x):
    return 0.5 * jnp.tanh(0.5 * x) + 0.5


def _rmsnorm(x, g, eps):
    return (x * lax.rsqrt(jnp.mean(x * x, axis=-1, keepdims=True) + eps)) * g


def _inproj_kernel(x_ref, g_ref, w_ref, *out_refs, d):
    h = _rmsnorm(x_ref[...], g_ref[...], NORM_EPS).astype(BF16)
    for c, o_ref in enumerate(out_refs):
        z = jnp.dot(h, w_ref[:, c * d:(c + 1) * d], preferred_element_type=F32)
        if c == 0:
            z = z * Q_SCALE
        o_ref[...] = z.astype(o_ref.dtype)


def _inproj(x2, g_mix, w_in_bf):
    n, d = x2.shape
    tm = TM_INPROJ
    out_dtypes = (BF16, BF16, BF16, F32, F32, F32, F32)
    row_spec = pl.BlockSpec((tm, d), lambda i: (i, 0))
    return pl.pallas_call(
        functools.partial(_inproj_kernel, d=d),
        grid=(n // tm,),
        in_specs=[
            row_spec,
            pl.BlockSpec((1, d), lambda i: (0, 0)),
            pl.BlockSpec((d, N_IN_PARTS * d), lambda i: (0, 0), pipeline_mode=pl.Buffered(1)),
        ],
        out_specs=[row_spec] * N_IN_PARTS,
        out_shape=[jax.ShapeDtypeStruct((n, d), dt) for dt in out_dtypes],
        compiler_params=pltpu.CompilerParams(dimension_semantics=("arbitrary",)),
        name="inproj",
    )(x2, g_mix, w_in_bf)


V_AUG = V_DIM + 16
PIPE_GROUP = 4
PIPE_LAG = 2
PIPE_RING = 2 * PIPE_LAG
PIPE_SLOTS = PIPE_GROUP * PIPE_RING


def _attn_kernel(lq1_ref, lk1_ref, lq2_ref, lk2_ref, g_ref, q_ref, k_ref, v_ref, o_ref,
                 qt_scr, vt_scr, acc_scr, m_scr, bias_scr, s_scr, p_scr, cmax_scr, alpha_scr, *, t, lam_init):
    nq = q_ref.shape[0] // t
    w = 2 * t

    feat = lax.broadcasted_iota(I32, (t, V_DIM), 1)
    ones_pad = jnp.where(lax.broadcasted_iota(I32, (V_AUG - V_DIM, t), 0) == 0, 1.0, 0.0).astype(F32)

    def prep(i, carry):
        rows = pl.ds(pl.multiple_of(i * t, t), t)
        q = q_ref[rows, :].astype(F32)
        q1t = jnp.where(feat < HEAD_DIM, q, 0.0).T
        q2t = jnp.where(feat >= HEAD_DIM, q, 0.0).T
        qt_scr[i] = jnp.concatenate([q1t, q2t], axis=1).astype(BF16)
        vt = v_ref[rows, :].astype(F32).T
        vt_scr[i] = jnp.concatenate([vt, ones_pad], axis=0).astype(BF16)
        m_scr[i] = jnp.full((1, w), MASK_NEG, F32)
        acc_scr[i] = jnp.zeros((V_AUG, w), F32)
        return carry

    lax.fori_loop(0, nq, prep, 0, unroll=8)
    key = lax.broadcasted_iota(I32, (t, w), 0)
    qry = lax.broadcasted_iota(I32, (t, w), 1) & (t - 1)
    bias_scr[...] = jnp.where(key <= qry, 0.0, MASK_NEG)

    def stage_a(j, i, slot, masked):
        kj = k_ref[pl.ds(pl.multiple_of(j * t, t), t), :]
        s = jnp.dot(kj, qt_scr[i], preferred_element_type=F32)
        if masked:
            s = s + bias_scr[...]
        s_scr[slot] = s
        cmax_scr[slot] = jnp.max(s, axis=0, keepdims=True)

    def stage_b(i, slot):
        m_old = m_scr[i]
        m_new = jnp.maximum(m_old, cmax_scr[slot])
        m_scr[i] = m_new
        p_scr[slot] = jnp.exp2(s_scr[slot] - m_new).astype(BF16)
        alpha_scr[slot] = jnp.exp2(m_old - m_new)

    def stage_c(j, i, slot):
        acc_scr[i] = alpha_scr[slot] * acc_scr[i] + jnp.dot(vt_scr[j], p_scr[slot], preferred_element_type=F32)

    def run(n_steps, n_masked, first, nxt):
        body_steps = PIPE_RING * PIPE_GROUP
        assert n_steps % body_steps == 0 and n_masked % body_steps == 0 and n_masked >= body_steps
        n_groups = n_steps // PIPE_GROUP
        n_masked_groups = n_masked // PIPE_GROUP

        def steps_of(st):
            out = []
            for _ in range(PIPE_GROUP):
                out.append(st)
                st = nxt(*st)
            return out, st

        def turn(t_mod, a=None, b=None, c=None, masked=False):
            base_a = (t_mod % PIPE_RING) * PIPE_GROUP
            base_b = ((t_mod - PIPE_LAG) % PIPE_RING) * PIPE_GROUP
            a_steps, following = steps_of(a) if a is not None else (None, None)
            b_steps = steps_of(b)[0] if b is not None else None
            c_steps = steps_of(c)[0] if c is not None else None
            for kk in range(PIPE_GROUP):
                if a_steps is not None:
                    stage_a(*a_steps[kk], base_a + kk, masked)
            for kk in range(PIPE_GROUP):
                if b_steps is not None:
                    stage_b(b_steps[kk][1], base_b + kk)
            for kk in range(PIPE_GROUP):
                if c_steps is not None:
                    stage_c(*c_steps[kk], base_a + kk)
            return following

        f = [first]
        for tt in range(PIPE_RING):
            f.append(turn(tt, a=f[tt], b=f[tt - PIPE_LAG] if tt >= PIPE_LAG else None, masked=True))

        def ring(_, f, masked):
            f = list(f)
            for r in range(PIPE_RING):
                f.append(turn(r, a=f[-1], b=f[-1 - PIPE_LAG], c=f[-1 - PIPE_RING], masked=masked))
            return tuple(f[-(PIPE_RING + 1):])

        f = lax.fori_loop(0, (n_masked_groups - PIPE_RING) // PIPE_RING, functools.partial(ring, masked=True), tuple(f))
        f = lax.fori_loop(0, (n_groups - n_masked_groups) // PIPE_RING, functools.partial(ring, masked=False), f)
        for r in range(PIPE_RING):
            turn(r, b=f[r + PIPE_LAG] if r < PIPE_LAG else None, c=f[r])

    def next_step(j, i):
        wrap = i + 1 >= nq
        return jnp.where(wrap, 0, j + 1), jnp.where(wrap, i - j + 1, i + 1)

    zero = jnp.int32(0)
    run(nq * (nq + 1) // 2, nq, (zero, zero), next_step)

    lam = (jnp.exp(jnp.sum(lq1_ref[...] * lk1_ref[...], axis=1, keepdims=True))
           - jnp.exp(jnp.sum(lq2_ref[...] * lk2_ref[...], axis=1, keepdims=True)) + lam_init)
    gain = g_ref[...]

    def finish(i, carry):
        acc = acc_scr[i]
        o1 = acc[:V_DIM, :t] / acc[V_DIM:V_DIM + 1, :t]
        o2 = acc[:V_DIM, t:] / acc[V_DIM:V_DIM + 1, t:]
        o = o1 - lam * o2
        y = ((o * lax.rsqrt(jnp.mean(o * o, axis=0, keepdims=True) + SUBLN_EPS)) * gain) * (1.0 - lam_init)
        o_ref[pl.ds(pl.multiple_of(i * t, t), t), :] = y.T.astype(o_ref.dtype)
        return carry

    lax.fori_loop(0, nq, finish, 0, unroll=8)


def _attention(q, k, v, lam_q1, lam_k1, lam_q2, lam_k2, g_subln, lam_init):
    b, s, d = q.shape
    t = T_ATTN
    assert s % t == 0 and d == N_HEADS * V_DIM
    nq = s // t
    small = lambda w: pl.BlockSpec((1, w), lambda bi, hi: (0, 0))
    head = pl.BlockSpec((None, s, V_DIM), lambda bi, hi: (bi, 0, hi))
    return pl.pallas_call(
        functools.partial(_attn_kernel, t=t, lam_init=lam_init),
        grid=(b, N_HEADS),
        in_specs=[small(HEAD_DIM), small(HEAD_DIM), small(HEAD_DIM), small(HEAD_DIM),
                  pl.BlockSpec((V_DIM, 1), lambda bi, hi: (0, 0)), head, head, head],
        out_specs=head,
        out_shape=jax.ShapeDtypeStruct((b, s, d), BF16),
        scratch_shapes=[
            pltpu.VMEM((nq, V_DIM, 2 * t), BF16),
            pltpu.VMEM((nq, V_AUG, t), BF16),
            pltpu.VMEM((nq, V_AUG, 2 * t), F32),
            pltpu.VMEM((nq, 1, 2 * t), F32),
            pltpu.VMEM((t, 2 * t), F32),
            pltpu.VMEM((PIPE_SLOTS,t, 2 * t), F32),
            pltpu.VMEM((PIPE_SLOTS,t, 2 * t), BF16),
            pltpu.VMEM((PIPE_SLOTS,1, 2 * t), F32),
            pltpu.VMEM((PIPE_SLOTS,1, 2 * t), F32),
        ],
        compiler_params=pltpu.CompilerParams(dimension_semantics=("arbitrary", "arbitrary")),
        name="attn",
    )(lam_q1, lam_k1, lam_q2, lam_k2, g_subln.reshape(V_DIM, 1), q, k, v)


def _lru_kernel(xr_ref, gr_ref, wc_ref, bc_ref, wrg_ref, brg_ref, wig_ref, big_ref, lam_ref, o_ref,
                xprev_ref, hcar_ref, a_scr, u_scr, *, t_tile):
    @pl.when(pl.program_id(1) == 0)
    def _():
        xprev_ref[...] = jnp.zeros_like(xprev_ref)
        hcar_ref[...] = jnp.zeros_like(hcar_ref)

    x = xr_ref[...]
    w = x.shape[1]
    prev = xprev_ref[...]
    wc = wc_ref[...]
    row8 = lax.broadcasted_iota(I32, (SUBLANES, w), 0)
    xc = bc_ref[...]
    for j in range(CONV_WIDTH):
        shift = CONV_WIDTH - 1 - j
        if shift == 0:
            xs = x
        else:
            rolled = pltpu.roll(x, shift, 0)
            head = jnp.where(row8 < shift, pltpu.roll(prev, shift, 0), rolled[:SUBLANES])
            xs = jnp.concatenate([head, rolled[SUBLANES:]], axis=0)
        xc = xc + xs * wc[j:j + 1]
    xprev_ref[...] = x[t_tile - SUBLANES:]

    xcb = xc.astype(BF16)
    n_grp = w // MXU_DIM
    pre_r = jnp.concatenate(
        [jnp.dot(xcb[:, g * MXU_DIM:(g + 1) * MXU_DIM], wrg_ref[g], preferred_element_type=F32) for g in range(n_grp)],
        axis=1) + brg_ref[...]
    pre_i = jnp.concatenate(
        [jnp.dot(xcb[:, g * MXU_DIM:(g + 1) * MXU_DIM], wig_ref[g], preferred_element_type=F32) for g in range(n_grp)],
        axis=1) + big_ref[...]
    r = _sigmoid(pre_r)
    ig = _sigmoid(pre_i)
    neg_lam = -lam_ref[...]
    softplus = jnp.maximum(neg_lam, 0.0) + jnp.log1p(jnp.exp(-jnp.abs(neg_lam)))
    log_a = (-LRU_C) * r * softplus
    a = jnp.exp(log_a)
    th = jnp.tanh(log_a)
    num = -2.0 * th
    mult = jnp.where(num > 0.0, num * lax.rsqrt(num * (1.0 - th)), 0.0)
    u = mult * (ig * xc)

    def scan_steps(a, u, pos, length, axis):
        dist = 1
        while dist < length:
            keep = pos >= dist
            u = jnp.where(keep, a * pltpu.roll(u, dist, axis) + u, u)
            a = jnp.where(keep, a * pltpu.roll(a, dist, axis), a)
            dist *= 2
        return a, u

    n_grp_t = t_tile // SUBLANES
    grouped = (n_grp_t, SUBLANES, w)
    a, u = scan_steps(a.reshape(grouped), u.reshape(grouped), lax.broadcasted_iota(I32, grouped, 1), SUBLANES, 1)
    a, u = a.reshape(x.shape), u.reshape(x.shape)
    n_lt = w // LANES
    for c in range(n_lt):
        a_scr[c] = a[:, c * LANES:(c + 1) * LANES]
        u_scr[c] = u[:, c * LANES:(c + 1) * LANES]
    last = pl.ds(SUBLANES - 1, n_grp_t, stride=SUBLANES)
    grow = lax.broadcasted_iota(I32, (n_grp_t, w), 0)
    ag, ug = scan_steps(jnp.concatenate([a_scr[c, last, :] for c in range(n_lt)], axis=1),
                        jnp.concatenate([u_scr[c, last, :] for c in range(n_lt)], axis=1), grow, n_grp_t, 0)
    h_end = ag * hcar_ref[...] + ug
    h_in = jnp.where(grow == 0, hcar_ref[...], pltpu.roll(h_end, 1, 0))
    hcar_ref[...] = h_end[n_grp_t - 1:]
    for c in range(n_lt):
        for k in range(SUBLANES):
            a_scr[c, pl.ds(k, n_grp_t, stride=SUBLANES), :] = h_in[:, c * LANES:(c + 1) * LANES]
    h = a * jnp.concatenate([a_scr[c] for c in range(n_lt)], axis=1) + u
    o_ref[...] = (h * jax.nn.gelu(gr_ref[...])).astype(o_ref.dtype)


def _block_diag_groups(wblk):
    nb, bw, _ = wblk.shape
    per = MXU_DIM // bw
    g = nb // per
    eye = jnp.eye(per, dtype=wblk.dtype)
    w5 = wblk.reshape(g, per, bw, bw)
    return jnp.einsum("gawv,ab->gawbv", w5, eye).reshape(g, MXU_DIM, MXU_DIM).astype(BF16)


def _lru(xr, gr, w_conv, b_conv, w_rg, b_rg, w_ig, b_ig, lru_lambda, b, s):
    n, w = xr.shape
    t = T_LRU
    n_t = s // t
    row_spec = pl.BlockSpec((t, w), lambda bi, ti: (bi * n_t + ti, 0))
    vec = lambda r: pl.BlockSpec((r, w), lambda bi, ti: (0, 0))
    blk = pl.BlockSpec((w // MXU_DIM, MXU_DIM, MXU_DIM), lambda bi, ti: (0, 0, 0))
    return pl.pallas_call(
        functools.partial(_lru_kernel, t_tile=t),
        grid=(b, n_t),
        in_specs=[row_spec, row_spec, vec(CONV_WIDTH), vec(1), blk, vec(1), blk, vec(1), vec(1)],
        out_specs=row_spec,
        out_shape=jax.ShapeDtypeStruct((n, w), BF16),
        scratch_shapes=[pltpu.VMEM((SUBLANES, w), F32), pltpu.VMEM((1, w), F32),
                        pltpu.VMEM((w // LANES, t, LANES), F32), pltpu.VMEM((w // LANES, t, LANES), F32)],
        compiler_params=pltpu.CompilerParams(dimension_semantics=("arbitrary", "arbitrary")),
        name="lru",
    )(xr, gr, w_conv, b_conv.reshape(1, w), _block_diag_groups(w_rg), b_rg.reshape(1, w),
      _block_diag_groups(w_ig), b_ig.reshape(1, w), lru_lambda.reshape(1, w))


def _split_bf16(x):
    hi = x.astype(BF16)
    return hi, (x - hi.astype(F32)).astype(BF16)


def _to_token_tiles(ref, x):
    tm, d = x.shape
    rpt = d // LANES
    for c in range(rpt):
        ref[pl.ds(c, tm, stride=rpt), :] = x[:, c * LANES:(c + 1) * LANES]


def _from_token_tiles(ref, tm, rpt):
    return jnp.concatenate([ref[pl.ds(c, tm, stride=rpt), :] for c in range(rpt)], axis=1)


def _merge_kernel(o_ref, hg_ref, ga_ref, gb_ref, x_ref, wa_ref, wl_ref, wo_ref, gm_ref, wrh_ref, wrl_ref, br_ref,
                  x1_ref, h2_ref, lg_ref):
    attn_br = jnp.dot(o_ref[...], wa_ref[...], preferred_element_type=F32)
    lru_br = jnp.dot(hg_ref[...], wl_ref[...], preferred_element_type=F32)
    mixed = _sigmoid(ga_ref[...]) * attn_br + _sigmoid(gb_ref[...]) * lru_br
    x1 = x_ref[...] + jnp.dot(mixed.astype(BF16), wo_ref[...], preferred_element_type=F32)
    x1_ref[...] = x1
    h2 = _rmsnorm(x1, gm_ref[...], NORM_EPS)
    _to_token_tiles(h2_ref, h2)
    hi, lo = _split_bf16(h2)
    wrh = wrh_ref[...]
    lg_ref[...] = (jnp.dot(hi, wrh, preferred_element_type=F32) + jnp.dot(lo, wrh, preferred_element_type=F32)
                   + jnp.dot(hi, wrl_ref[...], preferred_element_type=F32)) + br_ref[...]


def _merge(o, hg, ga, gb, x2, wa, wl, wo, g_moe, w_router, b_router):
    n, d = x2.shape
    tm = TM_MERGE
    rpt = d // LANES
    row = pl.BlockSpec((tm, d), lambda i: (i, 0))
    full = lambda r, c: pl.BlockSpec((r, c), lambda i: (0, 0))
    wr_hi, wr_lo = _split_bf16(w_router)
    return pl.pallas_call(
        _merge_kernel,
        grid=(n // tm,),
        in_specs=[row, row, row, row, row, full(d, d), full(d, d), full(d, d), full(1, d),
                  full(d, LANES), full(d, LANES), full(1, LANES)],
        out_specs=[row, pl.BlockSpec((tm * rpt, LANES), lambda i: (i, 0)), pl.BlockSpec((tm, LANES), lambda i: (i, 0))],
        out_shape=[jax.ShapeDtypeStruct((n, d), F32), jax.ShapeDtypeStruct((n * rpt, LANES), F32),
                   jax.ShapeDtypeStruct((n, LANES), F32)],
        compiler_params=pltpu.CompilerParams(dimension_semantics=("arbitrary",)),
        name="merge",
    )(o, hg, ga, gb, x2, wa, wl, wo, g_moe, wr_hi, wr_lo, b_router)


ROUTE_LOGIT_ROWS = 40


def _row_pick(x, rows, idx):
    return jnp.sum(jnp.where(rows == idx, x, jnp.zeros_like(x)), axis=0, keepdims=True)


def _route_kernel(lg_ref, pos_ref, wts_ref, tab_ref, *, n, t_tile, seg_tile):
    row = lax.broadcasted_iota(I32, (ROUTE_LOGIT_ROWS, t_tile), 0)
    erow = lax.broadcasted_iota(I32, (N_EXPERTS, t_tile), 0)
    prow = lax.broadcasted_iota(I32, (SUBLANES, t_tile), 0)
    wrow = lax.broadcasted_iota(I32, (LANES, t_tile), 0)
    neg_inf = jnp.float32(-jnp.inf)
    earlier = (lax.broadcasted_iota(I32, (t_tile, t_tile), 0) < lax.broadcasted_iota(I32, (t_tile, t_tile), 1)).astype(BF16)

    def phase1(t, cnt):
        start = pl.multiple_of(t * t_tile, t_tile)
        lg = lg_ref[pl.ds(start, t_tile), :].T[:ROUTE_LOGIT_ROWS]
        is_grp = row < N_GROUPS
        gl = jnp.where(is_grp, lg, neg_inf)
        gmax = jnp.max(gl, axis=0, keepdims=True)
        g_idx = jnp.min(jnp.where(gl == gmax, row, LANES), axis=0, keepdims=True)
        g_w = 1.0 / jnp.sum(jnp.where(is_grp, jnp.exp(lg - gmax), 0.0), axis=0, keepdims=True)
        lo = N_GROUPS + EXPERTS_PER_GROUP * g_idx
        in_grp = (row >= lo) & (row < lo + EXPERTS_PER_GROUP)
        fl = jnp.where(in_grp, lg, neg_inf)
        v1 = jnp.max(fl, axis=0, keepdims=True)
        i1 = jnp.min(jnp.where(in_grp & (fl == v1), row, LANES), axis=0, keepdims=True)
        rest = in_grp & (row != i1)
        fl2 = jnp.where(rest, lg, neg_inf)
        v2 = jnp.max(fl2, axis=0, keepdims=True)
        i2 = jnp.min(jnp.where(rest & (fl2 == v2), row, LANES), axis=0, keepdims=True)
        t2 = jnp.exp(v2 - v1)
        den = 1.0 + t2
        w1 = g_w * (1.0 / den)
        w2 = g_w * (t2 / den)
        e1 = i1 - N_GROUPS
        e2 = i2 - N_GROUPS
        onehot = ((erow == e1) | (erow == e2)).astype(F32)
        before = jnp.dot(onehot.astype(BF16), earlier, preferred_element_type=F32) + cnt
        rank1 = _row_pick(before, erow, e1).astype(I32)
        rank2 = _row_pick(before, erow, e2).astype(I32)
        pos_ref[:, pl.ds(start, t_tile)] = jnp.where(prow == 0, e1, jnp.where(prow == 1, e2, jnp.where(
            prow == 2, rank1, jnp.where(prow == 3, rank2, 0))))
        wts_ref[pl.ds(start, t_tile), :] = jnp.where(wrow == 0, w1, jnp.where(wrow == 1, w2, 0.0)).T
        return cnt + jnp.sum(onehot, axis=1, keepdims=True)

    cnt = lax.fori_loop(0, n // t_tile, phase1, jnp.zeros((N_EXPERTS, 1), F32))

    rows_sq = lax.broadcasted_iota(I32, (LANES, LANES), 0)
    lanes_sq = lax.broadcasted_iota(I32, (LANES, LANES), 1)
    cnt_sq = jnp.broadcast_to(jnp.concatenate([cnt, jnp.zeros((LANES - N_EXPERTS, 1), F32)], axis=0), (LANES, LANES))
    padded = jnp.ceil(cnt_sq * (1.0 / seg_tile)) * seg_tile
    incl = padded
    dist = 1
    while dist < LANES:
        incl = incl + jnp.where(rows_sq >= dist, pltpu.roll(incl, dist, 0), 0.0)
        dist *= 2
    offs = incl - padded
    tab_ref[...] = jnp.where(lanes_sq == 0, offs, jnp.where(lanes_sq == 1, cnt_sq, 0.0)).astype(I32)
    offs_col = offs[:N_EXPERTS, :1].astype(I32)

    def phase2(t, carry):
        cols = pl.ds(pl.multiple_of(t * t_tile, t_tile), t_tile)
        info = pos_ref[:, cols]
        offs_b = jnp.broadcast_to(offs_col, (N_EXPERTS, t_tile))
        p1 = info[2:3] + _row_pick(offs_b, erow, info[0:1])
        p2 = info[3:4] + _row_pick(offs_b, erow, info[1:2])
        pos_ref[:, cols] = jnp.where(prow == 0, p1, jnp.where(prow == 1, p2, 0))
        return carry

    lax.fori_loop(0, n // t_tile, phase2, 0)


def _route(logits):
    n = logits.shape[0]
    return pl.pallas_call(
        functools.partial(_route_kernel, n=n, t_tile=T_ROUTE, seg_tile=TM_EXPERT),
        out_shape=[jax.ShapeDtypeStruct((SUBLANES, n), I32), jax.ShapeDtypeStruct((n, LANES), F32),
                   jax.ShapeDtypeStruct((LANES, LANES), I32)],
        name="route",
    )(logits)


def _token_rows(idx, rpt):
    return pl.ds(pl.multiple_of(idx * rpt, rpt), rpt)


def _dispatch_kernel(seg_ref, cnt_ref, p1_ref, p2_ref, h_ref, hs_ref, zero_scr, sem, *, tm, rpt, n_tiles,
                     seg_tile, cap_tiles):
    def issue(r, carry):
        src = h_ref.at[_token_rows(r, rpt)]
        pltpu.make_async_copy(src, hs_ref.at[_token_rows(p1_ref[r], rpt)], sem).start(priority=0)
        pltpu.make_async_copy(src, hs_ref.at[_token_rows(p2_ref[r], rpt)], sem).start(priority=1)
        return carry

    lax.fori_loop(0, tm, issue, 0, unroll=DMA_ISSUE_UNROLL)
    whole = pltpu.make_async_copy(h_ref, hs_ref.at[pl.ds(0, tm * rpt)], sem)
    whole.wait()
    whole.wait()

    @pl.when(pl.program_id(0) == n_tiles - 1)
    def _():
        zero_scr[...] = jnp.zeros_like(zero_scr)
        zero_row = zero_scr.at[pl.ds(0, rpt)]

        def per_expert(e, carry):
            first_pad = seg_ref[e] + cnt_ref[e]
            n_pad = seg_ref[e + 1] - first_pad

            def fill(r, c):
                pltpu.make_async_copy(zero_row, hs_ref.at[_token_rows(first_pad + r, rpt)], sem).start()
                return c

            def drain(r, c):
                pltpu.make_async_copy(zero_row, hs_ref.at[_token_rows(first_pad + r, rpt)], sem).wait()
                return c

            lax.fori_loop(0, n_pad, fill, 0)
            lax.fori_loop(0, n_pad, drain, 0)
            return carry

        lax.fori_loop(0, N_EXPERTS, per_expert, 0)

        used_tiles = seg_ref[N_EXPERTS] // seg_tile

        def tail_tile(c):
            return hs_ref.at[pl.ds(pl.multiple_of((used_tiles + c) * (seg_tile * rpt), seg_tile * rpt), seg_tile * rpt)]

        def fill_tail(c, carry):
            pltpu.make_async_copy(zero_scr, tail_tile(c), sem).start()
            return carry

        def drain_tail(c, carry):
            pltpu.make_async_copy(zero_scr, tail_tile(c), sem).wait()
            return carry

        lax.fori_loop(0, cap_tiles - used_tiles, fill_tail, 0)
        lax.fori_loop(0, cap_tiles - used_tiles, drain_tail, 0)


def _dispatch(seg_start, counts, pos1, pos2, h2t, rpt):
    n = h2t.shape[0] // rpt
    tm = TM_MOE
    n_tiles = n // tm
    idx = pl.BlockSpec((tm,), lambda i, *_: (i,), memory_space=pltpu.SMEM)
    grid_spec = pltpu.PrefetchScalarGridSpec(
        num_scalar_prefetch=2,
        grid=(n_tiles,),
        in_specs=[idx, idx, pl.BlockSpec((tm * rpt, LANES), lambda i, *_: (i, 0))],
        out_specs=pl.BlockSpec(memory_space=pl.ANY),
        scratch_shapes=[pltpu.VMEM((TM_EXPERT * rpt, LANES), F32), pltpu.SemaphoreType.DMA(())],
    )
    cap_tiles = 2 * n // TM_EXPERT + N_EXPERTS
    return pl.pallas_call(
        functools.partial(_dispatch_kernel, tm=tm, rpt=rpt, n_tiles=n_tiles, seg_tile=TM_EXPERT, cap_tiles=cap_tiles),
        grid_spec=grid_spec,
        out_shape=jax.ShapeDtypeStruct((cap_tiles * TM_EXPERT * rpt, LANES), F32),
        compiler_params=pltpu.CompilerParams(dimension_semantics=("arbitrary",), has_side_effects=True),
        name="dispatch",
    )(seg_start, counts, pos1, pos2, h2t)


def _expert_kernel(t_blk, t_exp, t_valid, t_new, t_slot, t_next, hs_ref, wg_hbm, wu_hbm, wd_hbm, ys_ref,
                   wg_f32, wu_f32, wd_f32, wg_bf, wu_bf, wd_bf, sems, *, tm, rpt):
    t = pl.program_id(0)

    def weight_copies(e, slot):
        return [pltpu.make_async_copy(src.at[e], dst.at[slot], sems.at[slot])
                for src, dst in ((wg_hbm, wg_f32), (wu_hbm, wu_f32), (wd_hbm, wd_f32))]

    @pl.when(t_valid[t] == 1)
    def _():
        @pl.when(t_new[t] == 1)
        def _():
            slot = t_slot[t]

            @pl.when(t == 0)
            def _():
                for c in weight_copies(t_exp[t], slot):
                    c.start()

            for c in weight_copies(t_exp[t], slot):
                c.wait()
            wg_bf[...] = wg_f32[slot].astype(BF16)
            wu_bf[...] = wu_f32[slot].astype(BF16)
            wd_bf[...] = wd_f32[slot].astype(BF16)

            @pl.when(t_next[t] >= 0)
            def _():
                for c in weight_copies(t_next[t], 1 - slot):
                    c.start()

        h = _from_token_tiles(hs_ref, tm, rpt).astype(BF16)
        gate = jnp.dot(h, wg_bf[...], preferred_element_type=F32)
        up = jnp.dot(h, wu_bf[...], preferred_element_type=F32)
        hid = ((gate * _sigmoid(gate)) * up).astype(BF16)
        _to_token_tiles(ys_ref, jnp.dot(hid, wd_bf[...], preferred_element_type=F32))

    @pl.when(t_valid[t] == 0)
    def _():
        ys_ref[...] = jnp.zeros_like(ys_ref)


def _tile_table(seg_start, n_tiles, tm):
    total = seg_start[N_EXPERTS] // tm
    t = jnp.arange(n_tiles, dtype=I32)
    blk = jnp.minimum(t, total - 1)
    exp = jnp.sum(seg_start[None, 1:N_EXPERTS + 1] <= (blk * tm)[:, None], axis=1).astype(I32)
    valid = (t < total).astype(I32)
    new = (exp != jnp.concatenate([jnp.full((1,), -1, I32), exp[:-1]])).astype(I32)
    slot = (jnp.cumsum(new) - 1) % 2
    later = jnp.where(exp[None, :] > exp[:, None], exp[None, :], N_EXPERTS)
    nxt = jnp.min(later, axis=1)
    nxt = jnp.where(nxt < N_EXPERTS, nxt, -1).astype(I32)
    return blk, exp, valid, new, slot.astype(I32), nxt


def _experts(hs, seg_start, w_e_gate, w_e_up, w_e_down, rpt):
    d, ff = w_e_gate.shape[-2:]
    tm = TM_EXPERT
    n_tiles = hs.shape[0] // (rpt * tm)
    table = _tile_table(seg_start, n_tiles, tm)
    hbm = pl.BlockSpec(memory_space=pl.ANY)
    grid_spec = pltpu.PrefetchScalarGridSpec(
        num_scalar_prefetch=len(table),
        grid=(n_tiles,),
        in_specs=[pl.BlockSpec((tm * rpt, LANES), lambda t, b, *_: (b[t], 0)), hbm, hbm, hbm],
        out_specs=pl.BlockSpec((tm * rpt, LANES), lambda t, *_: (t, 0)),
        scratch_shapes=[pltpu.VMEM((2, d, ff), F32), pltpu.VMEM((2, d, ff), F32), pltpu.VMEM((2, ff, d), F32),
                        pltpu.VMEM((d, ff), BF16), pltpu.VMEM((d, ff), BF16), pltpu.VMEM((ff, d), BF16),
                        pltpu.SemaphoreType.DMA((2,))],
    )
    return pl.pallas_call(
        functools.partial(_expert_kernel, tm=tm, rpt=rpt),
        grid_spec=grid_spec,
        out_shape=jax.ShapeDtypeStruct(hs.shape, F32),
        compiler_params=pltpu.CompilerParams(dimension_semantics=("arbitrary",)),
        name="experts",
    )(*table, hs, w_e_gate, w_e_up, w_e_down)


def _combine_kernel(p1_ref, p2_ref, p1n_ref, p2n_ref, wts_ref, x1_ref, p_ref, gp_ref, wpg_ref, wpp_ref, gf_ref,
                    ys_ref, out_ref, ybuf, sems, *, tm, rpt, n_tiles):
    i = pl.program_id(0)
    slot = i % 2

    def gather(a_ref, b_ref, dst):
        def issue(r, carry):
            rows = _token_rows(r, rpt)
            pltpu.make_async_copy(ys_ref.at[_token_rows(a_ref[r], rpt)], ybuf.at[dst, 0, rows],
                                  sems.at[dst]).start(priority=0)
            pltpu.make_async_copy(ys_ref.at[_token_rows(b_ref[r], rpt)], ybuf.at[dst, 1, rows],
                                  sems.at[dst]).start(priority=1)
            return carry

        lax.fori_loop(0, tm, issue, 0, unroll=DMA_ISSUE_UNROLL)

    @pl.when(i == 0)
    def _():
        gather(p1_ref, p2_ref, 0)

    @pl.when(i + 1 < n_tiles)
    def _():
        gather(p1n_ref, p2n_ref, 1 - slot)

    for half in range(2):
        pltpu.make_async_copy(ys_ref.at[pl.ds(0, tm * rpt)], ybuf.at[slot, half], sems.at[slot]).wait()

    wts = wts_ref[...]
    y1 = _from_token_tiles(ybuf.at[slot, 0], tm, rpt)
    y2 = _from_token_tiles(ybuf.at[slot, 1], tm, rpt)
    x2 = x1_ref[...] + (wts[:, 0:1] * y1 + wts[:, 1:2] * y2)
    hp = _rmsnorm(x2, gp_ref[...], NORM_EPS).astype(BF16)
    gate = _sigmoid(jnp.dot(hp, wpg_ref[...], preferred_element_type=F32))
    proj = jnp.dot(p_ref[...].astype(BF16), wpp_ref[...], preferred_element_type=F32)
    x3 = x2 + gate * proj
    out_ref[...] = _rmsnorm(x3, gf_ref[...], NORM_EPS)


def _combine(pos1, pos2, wts, x1, p2d, g_ple, wpg, wpp, g_final, ys, rpt):
    n, d = x1.shape
    pd = p2d.shape[1]
    tm = TM_MOE
    n_tiles = n // tm
    idx = pl.BlockSpec((tm,), lambda i: (i,), memory_space=pltpu.SMEM)
    idx_next = pl.BlockSpec((tm,), lambda i: (jnp.minimum(i + 1, n_tiles - 1),), memory_space=pltpu.SMEM)
    row = pl.BlockSpec((tm, d), lambda i: (i, 0))
    full = lambda r, c: pl.BlockSpec((r, c), lambda i: (0, 0))
    return pl.pallas_call(
        functools.partial(_combine_kernel, tm=tm, rpt=rpt, n_tiles=n_tiles),
        grid=(n_tiles,),
        in_specs=[idx, idx, idx_next, idx_next, pl.BlockSpec((tm, LANES), lambda i: (i, 0)), row,
                  pl.BlockSpec((tm, pd), lambda i: (i, 0)), full(1, d), full(d, d), full(pd, d), full(1, d),
                  pl.BlockSpec(memory_space=pl.ANY)],
        out_specs=row,
        out_shape=jax.ShapeDtypeStruct((n, d), F32),
        scratch_shapes=[pltpu.VMEM((2, 2, tm * rpt, LANES), F32), pltpu.SemaphoreType.DMA((2,))],
        compiler_params=pltpu.CompilerParams(dimension_semantics=("arbitrary",)),
        name="combine",
    )(pos1, pos2, pos1, pos2, wts, x1, p2d, g_ple, wpg, wpp, g_final, ys)


def _layer(i, x2, p2d, b, s, g_mix, w_in, lam_q1, lam_k1, lam_q2, lam_k2, g_subln, w_conv, b_conv, w_rg, b_rg,
           w_ig, b_ig, lru_lambda, w_attn_br, w_lru_br, w_out, g_moe, w_rt_group, b_rt_group, w_rt_expert,
           b_rt_expert, w_e_gate, w_e_up, w_e_down, g_ple, w_ple_gate, w_ple_proj):
    n, d = x2.shape
    lam_init = 0.8 - 0.6 * math.exp(-0.3 * i)
    row = lambda a: a.reshape(1, -1)

    q, k, v, xr, gr, ga, gb = _inproj(x2, row(g_mix), w_in.astype(BF16))
    o = _attention(q.reshape(b, s, d), k.reshape(b, s, d), v.reshape(b, s, d),
                   row(lam_q1), row(lam_k1), row(lam_q2), row(lam_k2), row(g_subln), lam_init).reshape(n, d)
    hg = _lru(xr, gr, w_conv, b_conv, w_rg, b_rg, w_ig, b_ig, lru_lambda, b, s)

    pad = LANES - N_GROUPS - N_EXPERTS
    w_router = jnp.concatenate(
        [w_rt_group, w_rt_expert.transpose(1, 0, 2).reshape(d, N_EXPERTS), jnp.zeros((d, pad), F32)], axis=1)
    b_router = jnp.concatenate([b_rt_group, b_rt_expert.reshape(N_EXPERTS), jnp.zeros((pad,), F32)]).reshape(1, LANES)
    x1, h2, logits = _merge(o, hg, ga, gb, x2, w_attn_br.astype(BF16), w_lru_br.astype(BF16), w_out.astype(BF16),
                            row(g_moe), w_router, b_router)

    pos, wts, tab = _route(logits)
    pos1, pos2 = pos[0], pos[1]
    rpt = d // LANES
    seg_start, counts = tab[:N_EXPERTS + 1, 0], tab[:N_EXPERTS, 1]
    hs = _dispatch(seg_start, counts, pos1, pos2, h2, rpt)
    ys = _experts(hs, seg_start, w_e_gate, w_e_up, w_e_down, rpt)
    return pos1, pos2, wts, x1, ys


def kernel(x, p, g_mix, w_in, lam_q1, lam_k1, lam_q2, lam_k2, g_subln, w_conv, b_conv, w_rg, b_rg, w_ig, b_ig, lru_lambda, w_attn_br, w_lru_br, w_out, g_moe, w_rt_group, b_rt_group, w_rt_expert, b_rt_expert, w_e_gate, w_e_up, w_e_down, g_ple, w_ple_gate, w_ple_proj, g_final):
    b, s, d = x.shape
    depth = p.shape[0]
    assert depth == 1, "the final RMSNorm is fused into the last layer's combine step; one layer supported"
    n = b * s
    x2 = x.reshape(n, d)
    i = 0
    p2d = p[i].reshape(n, -1)
    pos1, pos2, wts, x1, ys = _layer(
        i, x2, p2d, b, s, g_mix[i], w_in[i], lam_q1[i], lam_k1[i], lam_q2[i], lam_k2[i], g_subln[i], w_conv[i],
        b_conv[i], w_rg[i], b_rg[i], w_ig[i], b_ig[i], lru_lambda[i], w_attn_br[i], w_lru_br[i], w_out[i],
        g_moe[i], w_rt_group[i], b_rt_group[i], w_rt_expert[i], b_rt_expert[i], w_e_gate[i], w_e_up[i],
        w_e_down[i], g_ple[i], w_ple_gate[i], w_ple_proj[i])
    out = _combine(pos1, pos2, wts, x1, p2d, g_ple[i].reshape(1, d), w_ple_gate[i].astype(BF16),
                   w_ple_proj[i].astype(BF16), g_final.reshape(1, d), ys, d // LANES)
    return out.reshape(b, s, d)
```

```python
import functools
import math

import jax
import jax.numpy as jnp
from jax import lax
from jax.experimental import pallas as pl
from jax.experimental.pallas import tpu as pltpu

F32 = jnp.float32
BF16 = jnp.bfloat16
I32 = jnp.int32

N_HEADS = 8
HEAD_DIM = 64
V_DIM = 2 * HEAD_DIM
SUBLN_EPS = 1e-5
NORM_EPS = 1e-6
LRU_BLOCKS = 16
CONV_WIDTH = 4
LRU_C = 8.0
N_GROUPS = 4
EXPERTS_PER_GROUP = 8
N_EXPERTS = N_GROUPS * EXPERTS_PER_GROUP
N_IN_PARTS = 7
LOG2_E = 1.4426950408889634
Q_SCALE = HEAD_DIM ** -0.5 * LOG2_E

LANES = 128
SUBLANES = 8
MXU_DIM = 256

TM_INPROJ = 512
T_ATTN = 256
T_LRU = 512
TM_MERGE = 512
T_ROUTE = 512
TM_MOE = 512
TM_EXPERT = 256
EXPERT_TILES_PER_STEP = 2
DMA_ISSUE_UNROLL = 8
MASK_NEG = -1e30


def _sigmoid(x):
    return 0.5 * jnp.tanh(0.5 * x) + 0.5


def _rmsnorm(x, g, eps):
    return (x * lax.rsqrt(jnp.mean(x * x, axis=-1, keepdims=True) + eps)) * g


def _inproj_kernel(x_ref, g_ref, w_ref, *out_refs, d):
    h = _rmsnorm(x_ref[...], g_ref[...], NORM_EPS).astype(BF16)
    for c, o_ref in enumerate(out_refs):
        z = jnp.dot(h, w_ref[:, c * d:(c + 1) * d], preferred_element_type=F32)
        if c == 0:
            z = z * Q_SCALE
        o_ref[...] = z.astype(o_ref.dtype)


def _inproj(x2, g_mix, w_in_bf):
    n, d = x2.shape
    tm = TM_INPROJ
    out_dtypes = (BF16, BF16, BF16, F32, F32, F32, F32)
    row_spec = pl.BlockSpec((tm, d), lambda i: (i, 0))
    return pl.pallas_call(
        functools.partial(_inproj_kernel, d=d),
        grid=(n // tm,),
        in_specs=[
            row_spec,
            pl.BlockSpec((1, d), lambda i: (0, 0)),
            pl.BlockSpec((d, N_IN_PARTS * d), lambda i: (0, 0), pipeline_mode=pl.Buffered(1)),
        ],
        out_specs=[row_spec] * N_IN_PARTS,
        out_shape=[jax.ShapeDtypeStruct((n, d), dt) for dt in out_dtypes],
        compiler_params=pltpu.CompilerParams(dimension_semantics=("arbitrary",)),
        name="inproj",
    )(x2, g_mix, w_in_bf)


V_AUG = V_DIM + 16
PIPE_GROUP = 4
PIPE_LAG = 2
PIPE_RING = 2 * PIPE_LAG
PIPE_SLOTS = PIPE_GROUP * PIPE_RING


def _attn_kernel(lq1_ref, lk1_ref, lq2_ref, lk2_ref, g_ref, q_ref, k_ref, v_ref, o_ref,
                 qt_scr, vt_scr, acc_scr, m_scr, bias_scr, s_scr, p_scr, cmax_scr, alpha_scr, *, t, lam_init):
    nq = q_ref.shape[0] // t
    w = 2 * t

    feat = lax.broadcasted_iota(I32, (t, V_DIM), 1)
    ones_pad = jnp.where(lax.broadcasted_iota(I32, (V_AUG - V_DIM, t), 0) == 0, 1.0, 0.0).astype(F32)

    def prep(i, carry):
        rows = pl.ds(pl.multiple_of(i * t, t), t)
        q = q_ref[rows, :].astype(F32)
        q1t = jnp.where(feat < HEAD_DIM, q, 0.0).T
        q2t = jnp.where(feat >= HEAD_DIM, q, 0.0).T
        qt_scr[i] = jnp.concatenate([q1t, q2t], axis=1).astype(BF16)
        vt = v_ref[rows, :].astype(F32).T
        vt_scr[i] = jnp.concatenate([vt, ones_pad], axis=0).astype(BF16)
        m_scr[i] = jnp.full((1, w), MASK_NEG, F32)
        acc_scr[i] = jnp.zeros((V_AUG, w), F32)
        return carry

    lax.fori_loop(0, nq, prep, 0, unroll=8)
    key = lax.broadcasted_iota(I32, (t, w), 0)
    qry = lax.broadcasted_iota(I32, (t, w), 1) & (t - 1)
    bias_scr[...] = jnp.where(key <= qry, 0.0, MASK_NEG)

    def stage_a(j, i, slot, masked):
        kj = k_ref[pl.ds(pl.multiple_of(j * t, t), t), :]
        s = jnp.dot(kj, qt_scr[i], preferred_element_type=F32)
        if masked:
            s = s + bias_scr[...]
        s_scr[slot] = s
        cmax_scr[slot] = jnp.max(s, axis=0, keepdims=True)

    def stage_b(i, slot):
        m_old = m_scr[i]
        m_new = jnp.maximum(m_old, cmax_scr[slot])
        m_scr[i] = m_new
        p_scr[slot] = jnp.exp2(s_scr[slot] - m_new).astype(BF16)
        alpha_scr[slot] = jnp.exp2(m_old - m_new)

    def stage_c(j, i, slot):
        acc_scr[i] = alpha_scr[slot] * acc_scr[i] + jnp.dot(vt_scr[j], p_scr[slot], preferred_element_type=F32)

    def run(n_steps, n_masked, first, nxt):
        body_steps = PIPE_RING * PIPE_GROUP
        assert n_steps % body_steps == 0 and n_masked % body_steps == 0 and n_masked >= body_steps
        n_groups = n_steps // PIPE_GROUP
        n_masked_groups = n_masked // PIPE_GROUP

        def steps_of(st):
            out = []
            for _ in range(PIPE_GROUP):
                out.append(st)
                st = nxt(*st)
            return out, st

        def turn(t_mod, a=None, b=None, c=None, masked=False):
            base_a = (t_mod % PIPE_RING) * PIPE_GROUP
            base_b = ((t_mod - PIPE_LAG) % PIPE_RING) * PIPE_GROUP
            a_steps, following = steps_of(a) if a is not None else (None, None)
            b_steps = steps_of(b)[0] if b is not None else None
            c_steps = steps_of(c)[0] if c is not None else None
            for kk in range(PIPE_GROUP):
                if a_steps is not None:
                    stage_a(*a_steps[kk], base_a + kk, masked)
            for kk in range(PIPE_GROUP):
                if b_steps is not None:
                    stage_b(b_steps[kk][1], base_b + kk)
            for kk in range(PIPE_GROUP):
                if c_steps is not None:
                    stage_c(*c_steps[kk], base_a + kk)
            return following

        f = [first]
        for tt in range(PIPE_RING):
            f.append(turn(tt, a=f[tt], b=f[tt - PIPE_LAG] if tt >= PIPE_LAG else None, masked=True))

        def ring(_, f, masked):
            f = list(f)
            for r in range(PIPE_RING):
                f.append(turn(r, a=f[-1], b=f[-1 - PIPE_LAG], c=f[-1 - PIPE_RING], masked=masked))
            return tuple(f[-(PIPE_RING + 1):])

        f = lax.fori_loop(0, (n_masked_groups - PIPE_RING) // PIPE_RING, functools.partial(ring, masked=True), tuple(f))
        f = lax.fori_loop(0, (n_groups - n_masked_groups) // PIPE_RING, functools.partial(ring, masked=False), f)
        for r in range(PIPE_RING):
            turn(r, b=f[r + PIPE_LAG] if r < PIPE_LAG else None, c=f[r])

    def next_step(j, i):
        wrap = i + 1 >= nq
        return jnp.where(wrap, 0, j + 1), jnp.where(wrap, i - j + 1, i + 1)

    zero = jnp.int32(0)
    run(nq * (nq + 1) // 2, nq, (zero, zero), next_step)

    lam = (jnp.exp(jnp.sum(lq1_ref[...] * lk1_ref[...], axis=1, keepdims=True))
           - jnp.exp(jnp.sum(lq2_ref[...] * lk2_ref[...], axis=1, keepdims=True)) + lam_init)
    gain = g_ref[...]

    def finish(i, carry):
        acc = acc_scr[i]
        o1 = acc[:V_DIM, :t] / acc[V_DIM:V_DIM + 1, :t]
        o2 = acc[:V_DIM, t:] / acc[V_DIM:V_DIM + 1, t:]
        o = o1 - lam * o2
        y = ((o * lax.rsqrt(jnp.mean(o * o, axis=0, keepdims=True) + SUBLN_EPS)) * gain) * (1.0 - lam_init)
        o_ref[pl.ds(pl.multiple_of(i * t, t), t), :] = y.T.astype(o_ref.dtype)
        return carry

    lax.fori_loop(0, nq, finish, 0, unroll=8)


def _attention(q, k, v, lam_q1, lam_k1, lam_q2, lam_k2, g_subln, lam_init):
    b, s, d = q.shape
    t = T_ATTN
    assert s % t == 0 and d == N_HEADS * V_DIM
    nq = s // t
    small = lambda w: pl.BlockSpec((1, w), lambda bi, hi: (0, 0))
    head = pl.BlockSpec((None, s, V_DIM), lambda bi, hi: (bi, 0, hi))
    return pl.pallas_call(
        functools.partial(_attn_kernel, t=t, lam_init=lam_init),
        grid=(b, N_HEADS),
        in_specs=[small(HEAD_DIM), small(HEAD_DIM), small(HEAD_DIM), small(HEAD_DIM),
                  pl.BlockSpec((V_DIM, 1), lambda bi, hi: (0, 0)), head, head, head],
        out_specs=head,
        out_shape=jax.ShapeDtypeStruct((b, s, d), BF16),
        scratch_shapes=[
            pltpu.VMEM((nq, V_DIM, 2 * t), BF16),
            pltpu.VMEM((nq, V_AUG, t), BF16),
            pltpu.VMEM((nq, V_AUG, 2 * t), F32),
            pltpu.VMEM((nq, 1, 2 * t), F32),
            pltpu.VMEM((t, 2 * t), F32),
            pltpu.VMEM((PIPE_SLOTS,t, 2 * t), F32),
            pltpu.VMEM((PIPE_SLOTS,t, 2 * t), BF16),
            pltpu.VMEM((PIPE_SLOTS,1, 2 * t), F32),
            pltpu.VMEM((PIPE_SLOTS,1, 2 * t), F32),
        ],
        compiler_params=pltpu.CompilerParams(dimension_semantics=("arbitrary", "arbitrary")),
        name="attn",
    )(lam_q1, lam_k1, lam_q2, lam_k2, g_subln.reshape(V_DIM, 1), q, k, v)


def _lru_kernel(xr_ref, gr_ref, wc_ref, bc_ref, wrg_ref, brg_ref, wig_ref, big_ref, lam_ref, o_ref,
                xprev_ref, hcar_ref, a_scr, u_scr, *, t_tile):
    @pl.when(pl.program_id(1) == 0)
    def _():
        xprev_ref[...] = jnp.zeros_like(xprev_ref)
        hcar_ref[...] = jnp.zeros_like(hcar_ref)

    x = xr_ref[...]
    w = x.shape[1]
    prev = xprev_ref[...]
    wc = wc_ref[...]
    row8 = lax.broadcasted_iota(I32, (SUBLANES, w), 0)
    xc = bc_ref[...]
    for j in range(CONV_WIDTH):
        shift = CONV_WIDTH - 1 - j
        if shift == 0:
            xs = x
        else:
            rolled = pltpu.roll(x, shift, 0)
            head = jnp.where(row8 < shift, pltpu.roll(prev, shift, 0), rolled[:SUBLANES])
            xs = jnp.concatenate([head, rolled[SUBLANES:]], axis=0)
        xc = xc + xs * wc[j:j + 1]
    xprev_ref[...] = x[t_tile - SUBLANES:]

    xcb = xc.astype(BF16)
    n_grp = w // MXU_DIM
    pre_r = jnp.concatenate(
        [jnp.dot(xcb[:, g * MXU_DIM:(g + 1) * MXU_DIM], wrg_ref[g], preferred_element_type=F32) for g in range(n_grp)],
        axis=1) + brg_ref[...]
    pre_i = jnp.concatenate(
        [jnp.dot(xcb[:, g * MXU_DIM:(g + 1) * MXU_DIM], wig_ref[g], preferred_element_type=F32) for g in range(n_grp)],
        axis=1) + big_ref[...]
    r = _sigmoid(pre_r)
    ig = _sigmoid(pre_i)
    neg_lam = -lam_ref[...]
    softplus = jnp.maximum(neg_lam, 0.0) + jnp.log1p(jnp.exp(-jnp.abs(neg_lam)))
    log_a = (-LRU_C) * r * softplus
    a = jnp.exp(log_a)
    th = jnp.tanh(log_a)
    num = -2.0 * th
    mult = jnp.where(num > 0.0, num * lax.rsqrt(num * (1.0 - th)), 0.0)
    u = mult * (ig * xc)

    def scan_steps(a, u, pos, length, axis):
        dist = 1
        while dist < length:
            keep = pos >= dist
            u = jnp.where(keep, a * pltpu.roll(u, dist, axis) + u, u)
            a = jnp.where(keep, a * pltpu.roll(a, dist, axis), a)
            dist *= 2
        return a, u

    n_grp_t = t_tile // SUBLANES
    grouped = (n_grp_t, SUBLANES, w)
    a, u = scan_steps(a.reshape(grouped), u.reshape(grouped), lax.broadcasted_iota(I32, grouped, 1), SUBLANES, 1)
    a, u = a.reshape(x.shape), u.reshape(x.shape)
    n_lt = w // LANES
    for c in range(n_lt):
        a_scr[c] = a[:, c * LANES:(c + 1) * LANES]
        u_scr[c] = u[:, c * LANES:(c + 1) * LANES]
    last = pl.ds(SUBLANES - 1, n_grp_t, stride=SUBLANES)
    grow = lax.broadcasted_iota(I32, (n_grp_t, w), 0)
    ag, ug = scan_steps(jnp.concatenate([a_scr[c, last, :] for c in range(n_lt)], axis=1),
                        jnp.concatenate([u_scr[c, last, :] for c in range(n_lt)], axis=1), grow, n_grp_t, 0)
    h_end = ag * hcar_ref[...] + ug
    h_in = jnp.where(grow == 0, hcar_ref[...], pltpu.roll(h_end, 1, 0))
    hcar_ref[...] = h_end[n_grp_t - 1:]
    for c in range(n_lt):
        for k in range(SUBLANES):
            a_scr[c, pl.ds(k, n_grp_t, stride=SUBLANES), :] = h_in[:, c * LANES:(c + 1) * LANES]
    h = a * jnp.concatenate([a_scr[c] for c in range(n_lt)], axis=1) + u
    o_ref[...] = (h * jax.nn.gelu(gr_ref[...])).astype(o_ref.dtype)


def _block_diag_groups(wblk):
    nb, bw, _ = wblk.shape
    per = MXU_DIM // bw
    g = nb // per
    eye = jnp.eye(per, dtype=wblk.dtype)
    w5 = wblk.reshape(g, per, bw, bw)
    return jnp.einsum("gawv,ab->gawbv", w5, eye).reshape(g, MXU_DIM, MXU_DIM).astype(BF16)


def _lru(xr, gr, w_conv, b_conv, w_rg, b_rg, w_ig, b_ig, lru_lambda, b, s):
    n, w = xr.shape
    t = T_LRU
    n_t = s // t
    row_spec = pl.BlockSpec((t, w), lambda bi, ti: (bi * n_t + ti, 0))
    vec = lambda r: pl.BlockSpec((r, w), lambda bi, ti: (0, 0))
    blk = pl.BlockSpec((w // MXU_DIM, MXU_DIM, MXU_DIM), lambda bi, ti: (0, 0, 0))
    return pl.pallas_call(
        functools.partial(_lru_kernel, t_tile=t),
        grid=(b, n_t),
        in_specs=[row_spec, row_spec, vec(CONV_WIDTH), vec(1), blk, vec(1), blk, vec(1), vec(1)],
        out_specs=row_spec,
        out_shape=jax.ShapeDtypeStruct((n, w), BF16),
        scratch_shapes=[pltpu.VMEM((SUBLANES, w), F32), pltpu.VMEM((1, w), F32),
                        pltpu.VMEM((w // LANES, t, LANES), F32), pltpu.VMEM((w // LANES, t, LANES), F32)],
        compiler_params=pltpu.CompilerParams(dimension_semantics=("arbitrary", "arbitrary")),
        name="lru",
    )(xr, gr, w_conv, b_conv.reshape(1, w), _block_diag_groups(w_rg), b_rg.reshape(1, w),
      _block_diag_groups(w_ig), b_ig.reshape(1, w), lru_lambda.reshape(1, w))


def _split_bf16(x):
    hi = x.astype(BF16)
    return hi, (x - hi.astype(F32)).astype(BF16)


def _to_token_tiles(ref, x, first_token=0):
    tm, d = x.shape
    rpt = d // LANES
    for c in range(rpt):
        ref[pl.ds(first_token * rpt + c, tm, stride=rpt), :] = x[:, c * LANES:(c + 1) * LANES]


def _from_token_tiles(ref, tm, rpt, first_token=0):
    return jnp.concatenate([ref[pl.ds(first_token * rpt + c, tm, stride=rpt), :] for c in range(rpt)], axis=1)


def _merge_kernel(o_ref, hg_ref, ga_ref, gb_ref, x_ref, wa_ref, wl_ref, wo_ref, gm_ref, wrh_ref, wrl_ref, br_ref,
                  x1_ref, h2_ref, lg_ref):
    attn_br = jnp.dot(o_ref[...], wa_ref[...], preferred_element_type=F32)
    lru_br = jnp.dot(hg_ref[...], wl_ref[...], preferred_element_type=F32)
    mixed = _sigmoid(ga_ref[...]) * attn_br + _sigmoid(gb_ref[...]) * lru_br
    x1 = x_ref[...] + jnp.dot(mixed.astype(BF16), wo_ref[...], preferred_element_type=F32)
    x1_ref[...] = x1
    h2 = _rmsnorm(x1, gm_ref[...], NORM_EPS)
    _to_token_tiles(h2_ref, h2)
    hi, lo = _split_bf16(h2)
    wrh = wrh_ref[...]
    lg_ref[...] = (jnp.dot(hi, wrh, preferred_element_type=F32) + jnp.dot(lo, wrh, preferred_element_type=F32)
                   + jnp.dot(hi, wrl_ref[...], preferred_element_type=F32)) + br_ref[...]


def _merge(o, hg, ga, gb, x2, wa, wl, wo, g_moe, w_router, b_router):
    n, d = x2.shape
    tm = TM_MERGE
    rpt = d // LANES
    row = pl.BlockSpec((tm, d), lambda i: (i, 0))
    full = lambda r, c: pl.BlockSpec((r, c), lambda i: (0, 0))
    wr_hi, wr_lo = _split_bf16(w_router)
    return pl.pallas_call(
        _merge_kernel,
        grid=(n // tm,),
        in_specs=[row, row, row, row, row, full(d, d), full(d, d), full(d, d), full(1, d),
                  full(d, LANES), full(d, LANES), full(1, LANES)],
        out_specs=[row, pl.BlockSpec((tm * rpt, LANES), lambda i: (i, 0)), pl.BlockSpec((tm, LANES), lambda i: (i, 0))],
        out_shape=[jax.ShapeDtypeStruct((n, d), F32), jax.ShapeDtypeStruct((n * rpt, LANES), F32),
                   jax.ShapeDtypeStruct((n, LANES), F32)],
        compiler_params=pltpu.CompilerParams(dimension_semantics=("arbitrary",)),
        name="merge",
    )(o, hg, ga, gb, x2, wa, wl, wo, g_moe, wr_hi, wr_lo, b_router)


ROUTE_LOGIT_ROWS = 40


def _row_pick(x, rows, idx):
    return jnp.sum(jnp.where(rows == idx, x, jnp.zeros_like(x)), axis=0, keepdims=True)


def _route_kernel(lg_ref, pos_ref, wts_ref, tab_ref, *, n, t_tile, seg_tile):
    row = lax.broadcasted_iota(I32, (ROUTE_LOGIT_ROWS, t_tile), 0)
    erow = lax.broadcasted_iota(I32, (N_EXPERTS, t_tile), 0)
    prow = lax.broadcasted_iota(I32, (SUBLANES, t_tile), 0)
    wrow = lax.broadcasted_iota(I32, (LANES, t_tile), 0)
    neg_inf = jnp.float32(-jnp.inf)
    earlier = (lax.broadcasted_iota(I32, (t_tile, t_tile), 0) < lax.broadcasted_iota(I32, (t_tile, t_tile), 1)).astype(BF16)

    def phase1(t, cnt):
        start = pl.multiple_of(t * t_tile, t_tile)
        lg = lg_ref[pl.ds(start, t_tile), :].T[:ROUTE_LOGIT_ROWS]
        is_grp = row < N_GROUPS
        gl = jnp.where(is_grp, lg, neg_inf)
        gmax = jnp.max(gl, axis=0, keepdims=True)
        g_idx = jnp.min(jnp.where(gl == gmax, row, LANES), axis=0, keepdims=True)
        g_w = 1.0 / jnp.sum(jnp.where(is_grp, jnp.exp(lg - gmax), 0.0), axis=0, keepdims=True)
        lo = N_GROUPS + EXPERTS_PER_GROUP * g_idx
        in_grp = (row >= lo) & (row < lo + EXPERTS_PER_GROUP)
        fl = jnp.where(in_grp, lg, neg_inf)
        v1 = jnp.max(fl, axis=0, keepdims=True)
        i1 = jnp.min(jnp.where(in_grp & (fl == v1), row, LANES), axis=0, keepdims=True)
        rest = in_grp & (row != i1)
        fl2 = jnp.where(rest, lg, neg_inf)
        v2 = jnp.max(fl2, axis=0, keepdims=True)
        i2 = jnp.min(jnp.where(rest & (fl2 == v2), row, LANES), axis=0, keepdims=True)
        t2 = jnp.exp(v2 - v1)
        den = 1.0 + t2
        w1 = g_w * (1.0 / den)
        w2 = g_w * (t2 / den)
        e1 = i1 - N_GROUPS
        e2 = i2 - N_GROUPS
        onehot = ((erow == e1) | (erow == e2)).astype(F32)
        before = jnp.dot(onehot.astype(BF16), earlier, preferred_element_type=F32) + cnt
        rank1 = _row_pick(before, erow, e1).astype(I32)
        rank2 = _row_pick(before, erow, e2).astype(I32)
        pos_ref[:, pl.ds(start, t_tile)] = jnp.where(prow == 0, e1, jnp.where(prow == 1, e2, jnp.where(
            prow == 2, rank1, jnp.where(prow == 3, rank2, 0))))
        wts_ref[pl.ds(start, t_tile), :] = jnp.where(wrow == 0, w1, jnp.where(wrow == 1, w2, 0.0)).T
        return cnt + jnp.sum(onehot, axis=1, keepdims=True)

    cnt = lax.fori_loop(0, n // t_tile, phase1, jnp.zeros((N_EXPERTS, 1), F32))

    rows_sq = lax.broadcasted_iota(I32, (LANES, LANES), 0)
    lanes_sq = lax.broadcasted_iota(I32, (LANES, LANES), 1)
    cnt_sq = jnp.broadcast_to(jnp.concatenate([cnt, jnp.zeros((LANES - N_EXPERTS, 1), F32)], axis=0), (LANES, LANES))
    padded = jnp.ceil(cnt_sq * (1.0 / seg_tile)) * seg_tile
    incl = padded
    dist = 1
    while dist < LANES:
        incl = incl + jnp.where(rows_sq >= dist, pltpu.roll(incl, dist, 0), 0.0)
        dist *= 2
    offs = incl - padded
    tab_ref[...] = jnp.where(lanes_sq == 0, offs, jnp.where(lanes_sq == 1, cnt_sq, 0.0)).astype(I32)
    offs_col = offs[:N_EXPERTS, :1].astype(I32)

    def phase2(t, carry):
        cols = pl.ds(pl.multiple_of(t * t_tile, t_tile), t_tile)
        info = pos_ref[:, cols]
        offs_b = jnp.broadcast_to(offs_col, (N_EXPERTS, t_tile))
        p1 = info[2:3] + _row_pick(offs_b, erow, info[0:1])
        p2 = info[3:4] + _row_pick(offs_b, erow, info[1:2])
        pos_ref[:, cols] = jnp.where(prow == 0, p1, jnp.where(prow == 1, p2, 0))
        return carry

    lax.fori_loop(0, n // t_tile, phase2, 0)


def _route(logits):
    n = logits.shape[0]
    return pl.pallas_call(
        functools.partial(_route_kernel, n=n, t_tile=T_ROUTE, seg_tile=TM_EXPERT),
        out_shape=[jax.ShapeDtypeStruct((SUBLANES, n), I32), jax.ShapeDtypeStruct((n, LANES), F32),
                   jax.ShapeDtypeStruct((LANES, LANES), I32)],
        name="route",
    )(logits)


def _token_rows(idx, rpt):
    return pl.ds(pl.multiple_of(idx * rpt, rpt), rpt)


def _dispatch_kernel(seg_ref, cnt_ref, p1_ref, p2_ref, h_ref, hs_ref, zero_scr, sem, *, tm, rpt, n_tiles,
                     seg_tile, cap_tiles):
    def issue(r, carry):
        src = h_ref.at[_token_rows(r, rpt)]
        pltpu.make_async_copy(src, hs_ref.at[_token_rows(p1_ref[r], rpt)], sem).start(priority=0)
        pltpu.make_async_copy(src, hs_ref.at[_token_rows(p2_ref[r], rpt)], sem).start(priority=1)
        return carry

    lax.fori_loop(0, tm, issue, 0, unroll=DMA_ISSUE_UNROLL)
    whole = pltpu.make_async_copy(h_ref, hs_ref.at[pl.ds(0, tm * rpt)], sem)
    whole.wait()
    whole.wait()

    @pl.when(pl.program_id(0) == n_tiles - 1)
    def _():
        zero_scr[...] = jnp.zeros_like(zero_scr)
        zero_row = zero_scr.at[pl.ds(0, rpt)]

        def per_expert(e, carry):
            first_pad = seg_ref[e] + cnt_ref[e]
            n_pad = seg_ref[e + 1] - first_pad

            def fill(r, c):
                pltpu.make_async_copy(zero_row, hs_ref.at[_token_rows(first_pad + r, rpt)], sem).start()
                return c

            def drain(r, c):
                pltpu.make_async_copy(zero_row, hs_ref.at[_token_rows(first_pad + r, rpt)], sem).wait()
                return c

            lax.fori_loop(0, n_pad, fill, 0)
            lax.fori_loop(0, n_pad, drain, 0)
            return carry

        lax.fori_loop(0, N_EXPERTS, per_expert, 0)

        used_tiles = seg_ref[N_EXPERTS] // seg_tile

        def tail_tile(c):
            return hs_ref.at[pl.ds(pl.multiple_of((used_tiles + c) * (seg_tile * rpt), seg_tile * rpt), seg_tile * rpt)]

        def fill_tail(c, carry):
            pltpu.make_async_copy(zero_scr, tail_tile(c), sem).start()
            return carry

        def drain_tail(c, carry):
            pltpu.make_async_copy(zero_scr, tail_tile(c), sem).wait()
            return carry

        lax.fori_loop(0, cap_tiles - used_tiles, fill_tail, 0)
        lax.fori_loop(0, cap_tiles - used_tiles, drain_tail, 0)


def _dispatch(seg_start, counts, pos1, pos2, h2t, rpt):
    n = h2t.shape[0] // rpt
    tm = TM_MOE
    n_tiles = n // tm
    idx = pl.BlockSpec((tm,), lambda i, *_: (i,), memory_space=pltpu.SMEM)
    grid_spec = pltpu.PrefetchScalarGridSpec(
        num_scalar_prefetch=2,
        grid=(n_tiles,),
        in_specs=[idx, idx, pl.BlockSpec((tm * rpt, LANES), lambda i, *_: (i, 0))],
        out_specs=pl.BlockSpec(memory_space=pl.ANY),
        scratch_shapes=[pltpu.VMEM((TM_EXPERT * rpt, LANES), F32), pltpu.SemaphoreType.DMA(())],
    )
    cap_tiles = 2 * n // TM_EXPERT + N_EXPERTS
    return pl.pallas_call(
        functools.partial(_dispatch_kernel, tm=tm, rpt=rpt, n_tiles=n_tiles, seg_tile=TM_EXPERT, cap_tiles=cap_tiles),
        grid_spec=grid_spec,
        out_shape=jax.ShapeDtypeStruct((cap_tiles * TM_EXPERT * rpt, LANES), F32),
        compiler_params=pltpu.CompilerParams(dimension_semantics=("arbitrary",), has_side_effects=True),
        name="dispatch",
    )(seg_start, counts, pos1, pos2, h2t)


def _expert_kernel(t_exp, t_valid, t_new, t_slot, t_next, hs_ref, wg_hbm, wu_hbm, wd_hbm, ys_ref,
                   wg_f32, wu_f32, wd_f32, wg_bf, wu_bf, wd_bf, sems, *, tm, rpt, tiles_per_step):
    def weight_copies(e, slot):
        return [pltpu.make_async_copy(src.at[e], dst.at[slot], sems.at[slot])
                for src, dst in ((wg_hbm, wg_f32), (wu_hbm, wu_f32), (wd_hbm, wd_f32))]

    def one_tile(t, first_token):
        @pl.when(t_valid[t] == 1)
        def _():
            @pl.when(t_new[t] == 1)
            def _():
                slot = t_slot[t]

                @pl.when(t == 0)
                def _():
                    for c in weight_copies(t_exp[t], slot):
                        c.start()

                for c in weight_copies(t_exp[t], slot):
                    c.wait()
                wg_bf[...] = wg_f32[slot].astype(BF16)
                wu_bf[...] = wu_f32[slot].astype(BF16)
                wd_bf[...] = wd_f32[slot].astype(BF16)

                @pl.when(t_next[t] >= 0)
                def _():
                    for c in weight_copies(t_next[t], 1 - slot):
                        c.start()

            h = _from_token_tiles(hs_ref, tm, rpt, first_token).astype(BF16)
            gate = jnp.dot(h, wg_bf[...], preferred_element_type=F32)
            up = jnp.dot(h, wu_bf[...], preferred_element_type=F32)
            hid = ((gate * _sigmoid(gate)) * up).astype(BF16)
            _to_token_tiles(ys_ref, jnp.dot(hid, wd_bf[...], preferred_element_type=F32), first_token)

        @pl.when(t_valid[t] == 0)
        def _():
            ys_ref[pl.ds(first_token * rpt, tm * rpt), :] = jnp.zeros((tm * rpt, LANES), F32)

    for k in range(tiles_per_step):
        one_tile(pl.program_id(0) * tiles_per_step + k, k * tm)


def _tile_table(seg_start, n_tiles, tm):
    total = seg_start[N_EXPERTS] // tm
    t = jnp.arange(n_tiles, dtype=I32)
    blk = jnp.minimum(t, total - 1)
    exp = jnp.sum(seg_start[None, 1:N_EXPERTS + 1] <= (blk * tm)[:, None], axis=1).astype(I32)
    valid = (t < total).astype(I32)
    new = (exp != jnp.concatenate([jnp.full((1,), -1, I32), exp[:-1]])).astype(I32)
    slot = (jnp.cumsum(new) - 1) % 2
    later = jnp.where(exp[None, :] > exp[:, None], exp[None, :], N_EXPERTS)
    nxt = jnp.min(later, axis=1)
    nxt = jnp.where(nxt < N_EXPERTS, nxt, -1).astype(I32)
    return exp, valid, new, slot.astype(I32), nxt


def _experts(hs, seg_start, w_e_gate, w_e_up, w_e_down, rpt):
    d, ff = w_e_gate.shape[-2:]
    tm = TM_EXPERT
    n_tiles = hs.shape[0] // (rpt * tm)
    table = _tile_table(seg_start, n_tiles, tm)
    hbm = pl.BlockSpec(memory_space=pl.ANY)
    per_step = EXPERT_TILES_PER_STEP
    assert n_tiles % per_step == 0
    rows = pl.BlockSpec((per_step * tm * rpt, LANES), lambda s, *_: (s, 0))
    grid_spec = pltpu.PrefetchScalarGridSpec(
        num_scalar_prefetch=len(table),
        grid=(n_tiles // per_step,),
        in_specs=[rows, hbm, hbm, hbm],
        out_specs=rows,
        scratch_shapes=[pltpu.VMEM((2, d, ff), F32), pltpu.VMEM((2, d, ff), F32), pltpu.VMEM((2, ff, d), F32),
                        pltpu.VMEM((d, ff), BF16), pltpu.VMEM((d, ff), BF16), pltpu.VMEM((ff, d), BF16),
                        pltpu.SemaphoreType.DMA((2,))],
    )
    return pl.pallas_call(
        functools.partial(_expert_kernel, tm=tm, rpt=rpt, tiles_per_step=per_step),
        grid_spec=grid_spec,
        out_shape=jax.ShapeDtypeStruct(hs.shape, F32),
        compiler_params=pltpu.CompilerParams(dimension_semantics=("arbitrary",)),
        name="experts",
    )(*table, hs, w_e_gate, w_e_up, w_e_down)


def _combine_kernel(p1_ref, p2_ref, p1n_ref, p2n_ref, wts_ref, x1_ref, p_ref, gp_ref, wpg_ref, wpp_ref, gf_ref,
                    ys_ref, out_ref, ybuf, sems, *, tm, rpt, n_tiles):
    i = pl.program_id(0)
    slot = i % 2

    def gather(a_ref, b_ref, dst):
        def issue(r, carry):
            rows = _token_rows(r, rpt)
            pltpu.make_async_copy(ys_ref.at[_token_rows(a_ref[r], rpt)], ybuf.at[dst, 0, rows],
                                  sems.at[dst]).start(priority=0)
            pltpu.make_async_copy(ys_ref.at[_token_rows(b_ref[r], rpt)], ybuf.at[dst, 1, rows],
                                  sems.at[dst]).start(priority=1)
            return carry

        lax.fori_loop(0, tm, issue, 0, unroll=DMA_ISSUE_UNROLL)

    @pl.when(i == 0)
    def _():
        gather(p1_ref, p2_ref, 0)

    @pl.when(i + 1 < n_tiles)
    def _():
        gather(p1n_ref, p2n_ref, 1 - slot)

    for half in range(2):
        pltpu.make_async_copy(ys_ref.at[pl.ds(0, tm * rpt)], ybuf.at[slot, half], sems.at[slot]).wait()

    wts = wts_ref[...]
    y1 = _from_token_tiles(ybuf.at[slot, 0], tm, rpt)
    y2 = _from_token_tiles(ybuf.at[slot, 1], tm, rpt)
    x2 = x1_ref[...] + (wts[:, 0:1] * y1 + wts[:, 1:2] * y2)
    hp = _rmsnorm(x2, gp_ref[...], NORM_EPS).astype(BF16)
    gate = _sigmoid(jnp.dot(hp, wpg_ref[...], preferred_element_type=F32))
    proj = jnp.dot(p_ref[...].astype(BF16), wpp_ref[...], preferred_element_type=F32)
    x3 = x2 + gate * proj
    out_ref[...] = _rmsnorm(x3, gf_ref[...], NORM_EPS)


def _combine(pos1, pos2, wts, x1, p2d, g_ple, wpg, wpp, g_final, ys, rpt):
    n, d = x1.shape
    pd = p2d.shape[1]
    tm = TM_MOE
    n_tiles = n // tm
    idx = pl.BlockSpec((tm,), lambda i: (i,), memory_space=pltpu.SMEM)
    idx_next = pl.BlockSpec((tm,), lambda i: (jnp.minimum(i + 1, n_tiles - 1),), memory_space=pltpu.SMEM)
    row = pl.BlockSpec((tm, d), lambda i: (i, 0))
    full = lambda r, c: pl.BlockSpec((r, c), lambda i: (0, 0))
    return pl.pallas_call(
        functools.partial(_combine_kernel, tm=tm, rpt=rpt, n_tiles=n_tiles),
        grid=(n_tiles,),
        in_specs=[idx, idx, idx_next, idx_next, pl.BlockSpec((tm, LANES), lambda i: (i, 0)), row,
                  pl.BlockSpec((tm, pd), lambda i: (i, 0)), full(1, d), full(d, d), full(pd, d), full(1, d),
                  pl.BlockSpec(memory_space=pl.ANY)],
        out_specs=row,
        out_shape=jax.ShapeDtypeStruct((n, d), F32),
        scratch_shapes=[pltpu.VMEM((2, 2, tm * rpt, LANES), F32), pltpu.SemaphoreType.DMA((2,))],
        compiler_params=pltpu.CompilerParams(dimension_semantics=("arbitrary",)),
        name="combine",
    )(pos1, pos2, pos1, pos2, wts, x1, p2d, g_ple, wpg, wpp, g_final, ys)


def _layer(i, x2, p2d, b, s, g_mix, w_in, lam_q1, lam_k1, lam_q2, lam_k2, g_subln, w_conv, b_conv, w_rg, b_rg,
           w_ig, b_ig, lru_lambda, w_attn_br, w_lru_br, w_out, g_moe, w_rt_group, b_rt_group, w_rt_expert,
           b_rt_expert, w_e_gate, w_e_up, w_e_down, g_ple, w_ple_gate, w_ple_proj):
    n, d = x2.shape
    lam_init = 0.8 - 0.6 * math.exp(-0.3 * i)
    row = lambda a: a.reshape(1, -1)

    q, k, v, xr, gr, ga, gb = _inproj(x2, row(g_mix), w_in.astype(BF16))
    o = _attention(q.reshape(b, s, d), k.reshape(b, s, d), v.reshape(b, s, d),
                   row(lam_q1), row(lam_k1), row(lam_q2), row(lam_k2), row(g_subln), lam_init).reshape(n, d)
    hg = _lru(xr, gr, w_conv, b_conv, w_rg, b_rg, w_ig, b_ig, lru_lambda, b, s)

    pad = LANES - N_GROUPS - N_EXPERTS
    w_router = jnp.concatenate(
        [w_rt_group, w_rt_expert.transpose(1, 0, 2).reshape(d, N_EXPERTS), jnp.zeros((d, pad), F32)], axis=1)
    b_router = jnp.concatenate([b_rt_group, b_rt_expert.reshape(N_EXPERTS), jnp.zeros((pad,), F32)]).reshape(1, LANES)
    x1, h2, logits = _merge(o, hg, ga, gb, x2, w_attn_br.astype(BF16), w_lru_br.astype(BF16), w_out.astype(BF16),
                            row(g_moe), w_router, b_router)

    pos, wts, tab = _route(logits)
    pos1, pos2 = pos[0], pos[1]
    rpt = d // LANES
    seg_start, counts = tab[:N_EXPERTS + 1, 0], tab[:N_EXPERTS, 1]
    hs = _dispatch(seg_start, counts, pos1, pos2, h2, rpt)
    ys = _experts(hs, seg_start, w_e_gate, w_e_up, w_e_down, rpt)
    return pos1, pos2, wts, x1, ys


def kernel(x, p, g_mix, w_in, lam_q1, lam_k1, lam_q2, lam_k2, g_subln, w_conv, b_conv, w_rg, b_rg, w_ig, b_ig, lru_lambda, w_attn_br, w_lru_br, w_out, g_moe, w_rt_group, b_rt_group, w_rt_expert, b_rt_expert, w_e_gate, w_e_up, w_e_down, g_ple, w_ple_gate, w_ple_proj, g_final):
    b, s, d = x.shape
    depth = p.shape[0]
    assert depth == 1, "the final RMSNorm is fused into the last layer's combine step; one layer supported"
    n = b * s
    x2 = x.reshape(n, d)
    i = 0
    p2d = p[i].reshape(n, -1)
    pos1, pos2, wts, x1, ys = _layer(
        i, x2, p2d, b, s, g_mix[i], w_in[i], lam_q1[i], lam_k1[i], lam_q2[i], lam_k2[i], g_subln[i], w_conv[i],
        b_conv[i], w_rg[i], b_rg[i], w_ig[i], b_ig[i], lru_lambda[i], w_attn_br[i], w_lru_br[i], w_out[i],
        g_moe[i], w_rt_group[i], b_rt_group[i], w_rt_expert[i], b_rt_expert[i], w_e_gate[i], w_e_up[i],
        w_e_down[i], g_ple[i], w_ple_gate[i], w_ple_proj[i])
    out = _combine(pos1, pos2, wts, x1, p2d, g_ple[i].reshape(1, d), w_ple_gate[i].astype(BF16),
                   w_ple_proj[i].astype(BF16), g_final.reshape(1, d), ys, d // LANES)
    return out.reshape(b, s, d)
```

```python
import functools
import math

import jax
import jax.numpy as jnp
from jax import lax
from jax.experimental import pallas as pl
from jax.experimental.pallas import tpu as pltpu

F32 = jnp.float32
BF16 = jnp.bfloat16
I32 = jnp.int32

N_HEADS = 8
HEAD_DIM = 64
V_DIM = 2 * HEAD_DIM
SUBLN_EPS = 1e-5
NORM_EPS = 1e-6
LRU_BLOCKS = 16
CONV_WIDTH = 4
LRU_C = 8.0
N_GROUPS = 4
EXPERTS_PER_GROUP = 8
N_EXPERTS = N_GROUPS * EXPERTS_PER_GROUP
N_IN_PARTS = 7
LOG2_E = 1.4426950408889634
Q_SCALE = HEAD_DIM ** -0.5 * LOG2_E

LANES = 128
SUBLANES = 8
MXU_DIM = 256

TM_INPROJ = 512
T_ATTN = 256
T_LRU = 512
TM_MERGE = 512
T_ROUTE = 512
TM_DISPATCH = 1024
TM_COMBINE = 512
TM_EXPERT = 256
EXPERT_TILES_PER_STEP = 4
DMA_ISSUE_UNROLL = 8
MASK_NEG = -1e30


def _sigmoid(x):
    return 0.5 * jnp.tanh(0.5 * x) + 0.5


def _rmsnorm(x, g, eps):
    return (x * lax.rsqrt(jnp.mean(x * x, axis=-1, keepdims=True) + eps)) * g


def _inproj_kernel(x_ref, g_ref, w_ref, *out_refs, d):
    h = _rmsnorm(x_ref[...], g_ref[...], NORM_EPS).astype(BF16)
    for c, o_ref in enumerate(out_refs):
        z = jnp.dot(h, w_ref[:, c * d:(c + 1) * d], preferred_element_type=F32)
        if c == 0:
            z = z * Q_SCALE
        o_ref[...] = z.astype(o_ref.dtype)


def _inproj(x2, g_mix, w_in_bf):
    n, d = x2.shape
    tm = TM_INPROJ
    out_dtypes = (BF16, BF16, BF16, F32, F32, F32, F32)
    row_spec = pl.BlockSpec((tm, d), lambda i: (i, 0))
    return pl.pallas_call(
        functools.partial(_inproj_kernel, d=d),
        grid=(n // tm,),
        in_specs=[
            row_spec,
            pl.BlockSpec((1, d), lambda i: (0, 0)),
            pl.BlockSpec((d, N_IN_PARTS * d), lambda i: (0, 0), pipeline_mode=pl.Buffered(1)),
        ],
        out_specs=[row_spec] * N_IN_PARTS,
        out_shape=[jax.ShapeDtypeStruct((n, d), dt) for dt in out_dtypes],
        compiler_params=pltpu.CompilerParams(dimension_semantics=("arbitrary",)),
        name="inproj",
    )(x2, g_mix, w_in_bf)


V_AUG = V_DIM + 16
PIPE_GROUP = 4
PIPE_LAG = 2
PIPE_RING = 2 * PIPE_LAG
PIPE_SLOTS = PIPE_GROUP * PIPE_RING


def _attn_kernel(lq1_ref, lk1_ref, lq2_ref, lk2_ref, g_ref, q_ref, k_ref, v_ref, o_ref,
                 qt_scr, vt_scr, acc_scr, m_scr, bias_scr, s_scr, p_scr, cmax_scr, alpha_scr, *, t, lam_init):
    nq = q_ref.shape[0] // t
    w = 2 * t

    feat = lax.broadcasted_iota(I32, (t, V_DIM), 1)
    ones_pad = jnp.where(lax.broadcasted_iota(I32, (V_AUG - V_DIM, t), 0) == 0, 1.0, 0.0).astype(F32)

    def prep(i, carry):
        rows = pl.ds(pl.multiple_of(i * t, t), t)
        q = q_ref[rows, :].astype(F32)
        q1t = jnp.where(feat < HEAD_DIM, q, 0.0).T
        q2t = jnp.where(feat >= HEAD_DIM, q, 0.0).T
        qt_scr[i] = jnp.concatenate([q1t, q2t], axis=1).astype(BF16)
        vt = v_ref[rows, :].astype(F32).T
        vt_scr[i] = jnp.concatenate([vt, ones_pad], axis=0).astype(BF16)
        m_scr[i] = jnp.full((1, w), MASK_NEG, F32)
        acc_scr[i] = jnp.zeros((V_AUG, w), F32)
        return carry

    lax.fori_loop(0, nq, prep, 0, unroll=8)
    key = lax.broadcasted_iota(I32, (t, w), 0)
    qry = lax.broadcasted_iota(I32, (t, w), 1) & (t - 1)
    bias_scr[...] = jnp.where(key <= qry, 0.0, MASK_NEG)

    def stage_a(j, i, slot, masked):
        kj = k_ref[pl.ds(pl.multiple_of(j * t, t), t), :]
        s = jnp.dot(kj, qt_scr[i], preferred_element_type=F32)
        if masked:
            s = s + bias_scr[...]
        s_scr[slot] = s
        cmax_scr[slot] = jnp.max(s, axis=0, keepdims=True)

    def stage_b(i, slot):
        m_old = m_scr[i]
        m_new = jnp.maximum(m_old, cmax_scr[slot])
        m_scr[i] = m_new
        p_scr[slot] = jnp.exp2(s_scr[slot] - m_new).astype(BF16)
        alpha_scr[slot] = jnp.exp2(m_old - m_new)

    def stage_c(j, i, slot):
        acc_scr[i] = alpha_scr[slot] * acc_scr[i] + jnp.dot(vt_scr[j], p_scr[slot], preferred_element_type=F32)

    def run(n_steps, n_masked, first, nxt):
        body_steps = PIPE_RING * PIPE_GROUP
        assert n_steps % body_steps == 0 and n_masked % body_steps == 0 and n_masked >= body_steps
        n_groups = n_steps // PIPE_GROUP
        n_masked_groups = n_masked // PIPE_GROUP

        def steps_of(st):
            out = []
            for _ in range(PIPE_GROUP):
                out.append(st)
                st = nxt(*st)
            return out, st

        def turn(t_mod, a=None, b=None, c=None, masked=False):
            base_a = (t_mod % PIPE_RING) * PIPE_GROUP
            base_b = ((t_mod - PIPE_LAG) % PIPE_RING) * PIPE_GROUP
            a_steps, following = steps_of(a) if a is not None else (None, None)
            b_steps = steps_of(b)[0] if b is not None else None
            c_steps = steps_of(c)[0] if c is not None else None
            for kk in range(PIPE_GROUP):
                if a_steps is not None:
                    stage_a(*a_steps[kk], base_a + kk, masked)
            for kk in range(PIPE_GROUP):
                if b_steps is not None:
                    stage_b(b_steps[kk][1], base_b + kk)
            for kk in range(PIPE_GROUP):
                if c_steps is not None:
                    stage_c(*c_steps[kk], base_a + kk)
            return following

        f = [first]
        for tt in range(PIPE_RING):
            f.append(turn(tt, a=f[tt], b=f[tt - PIPE_LAG] if tt >= PIPE_LAG else None, masked=True))

        def ring(_, f, masked):
            f = list(f)
            for r in range(PIPE_RING):
                f.append(turn(r, a=f[-1], b=f[-1 - PIPE_LAG], c=f[-1 - PIPE_RING], masked=masked))
            return tuple(f[-(PIPE_RING + 1):])

        f = lax.fori_loop(0, (n_masked_groups - PIPE_RING) // PIPE_RING, functools.partial(ring, masked=True), tuple(f))
        f = lax.fori_loop(0, (n_groups - n_masked_groups) // PIPE_RING, functools.partial(ring, masked=False), f)
        for r in range(PIPE_RING):
            turn(r, b=f[r + PIPE_LAG] if r < PIPE_LAG else None, c=f[r])

    def next_step(j, i):
        wrap = i + 1 >= nq
        return jnp.where(wrap, 0, j + 1), jnp.where(wrap, i - j + 1, i + 1)

    zero = jnp.int32(0)
    run(nq * (nq + 1) // 2, nq, (zero, zero), next_step)

    lam = (jnp.exp(jnp.sum(lq1_ref[...] * lk1_ref[...], axis=1, keepdims=True))
           - jnp.exp(jnp.sum(lq2_ref[...] * lk2_ref[...], axis=1, keepdims=True)) + lam_init)
    gain = g_ref[...]

    def finish(i, carry):
        acc = acc_scr[i]
        o1 = acc[:V_DIM, :t] / acc[V_DIM:V_DIM + 1, :t]
        o2 = acc[:V_DIM, t:] / acc[V_DIM:V_DIM + 1, t:]
        o = o1 - lam * o2
        y = ((o * lax.rsqrt(jnp.mean(o * o, axis=0, keepdims=True) + SUBLN_EPS)) * gain) * (1.0 - lam_init)
        o_ref[pl.ds(pl.multiple_of(i * t, t), t), :] = y.T.astype(o_ref.dtype)
        return carry

    lax.fori_loop(0, nq, finish, 0, unroll=8)


def _attention(q, k, v, lam_q1, lam_k1, lam_q2, lam_k2, g_subln, lam_init):
    b, s, d = q.shape
    t = T_ATTN
    assert s % t == 0 and d == N_HEADS * V_DIM
    nq = s // t
    small = lambda w: pl.BlockSpec((1, w), lambda bi, hi: (0, 0))
    head = pl.BlockSpec((None, s, V_DIM), lambda bi, hi: (bi, 0, hi))
    return pl.pallas_call(
        functools.partial(_attn_kernel, t=t, lam_init=lam_init),
        grid=(b, N_HEADS),
        in_specs=[small(HEAD_DIM), small(HEAD_DIM), small(HEAD_DIM), small(HEAD_DIM),
                  pl.BlockSpec((V_DIM, 1), lambda bi, hi: (0, 0)), head, head, head],
        out_specs=head,
        out_shape=jax.ShapeDtypeStruct((b, s, d), BF16),
        scratch_shapes=[
            pltpu.VMEM((nq, V_DIM, 2 * t), BF16),
            pltpu.VMEM((nq, V_AUG, t), BF16),
            pltpu.VMEM((nq, V_AUG, 2 * t), F32),
            pltpu.VMEM((nq, 1, 2 * t), F32),
            pltpu.VMEM((t, 2 * t), F32),
            pltpu.VMEM((PIPE_SLOTS,t, 2 * t), F32),
            pltpu.VMEM((PIPE_SLOTS,t, 2 * t), BF16),
            pltpu.VMEM((PIPE_SLOTS,1, 2 * t), F32),
            pltpu.VMEM((PIPE_SLOTS,1, 2 * t), F32),
        ],
        compiler_params=pltpu.CompilerParams(dimension_semantics=("arbitrary", "arbitrary")),
        name="attn",
    )(lam_q1, lam_k1, lam_q2, lam_k2, g_subln.reshape(V_DIM, 1), q, k, v)


def _lru_kernel(xr_ref, gr_ref, wc_ref, bc_ref, wrg_ref, brg_ref, wig_ref, big_ref, lam_ref, o_ref,
                xprev_ref, hcar_ref, a_scr, u_scr, *, t_tile):
    @pl.when(pl.program_id(1) == 0)
    def _():
        xprev_ref[...] = jnp.zeros_like(xprev_ref)
        hcar_ref[...] = jnp.zeros_like(hcar_ref)

    x = xr_ref[...]
    w = x.shape[1]
    prev = xprev_ref[...]
    wc = wc_ref[...]
    row8 = lax.broadcasted_iota(I32, (SUBLANES, w), 0)
    xc = bc_ref[...]
    for j in range(CONV_WIDTH):
        shift = CONV_WIDTH - 1 - j
        if shift == 0:
            xs = x
        else:
            rolled = pltpu.roll(x, shift, 0)
            head = jnp.where(row8 < shift, pltpu.roll(prev, shift, 0), rolled[:SUBLANES])
            xs = jnp.concatenate([head, rolled[SUBLANES:]], axis=0)
        xc = xc + xs * wc[j:j + 1]
    xprev_ref[...] = x[t_tile - SUBLANES:]

    xcb = xc.astype(BF16)
    n_grp = w // MXU_DIM
    pre_r = jnp.concatenate(
        [jnp.dot(xcb[:, g * MXU_DIM:(g + 1) * MXU_DIM], wrg_ref[g], preferred_element_type=F32) for g in range(n_grp)],
        axis=1) + brg_ref[...]
    pre_i = jnp.concatenate(
        [jnp.dot(xcb[:, g * MXU_DIM:(g + 1) * MXU_DIM], wig_ref[g], preferred_element_type=F32) for g in range(n_grp)],
        axis=1) + big_ref[...]
    r = _sigmoid(pre_r)
    ig = _sigmoid(pre_i)
    neg_lam = -lam_ref[...]
    softplus = jnp.maximum(neg_lam, 0.0) + jnp.log1p(jnp.exp(-jnp.abs(neg_lam)))
    log_a = (-LRU_C) * r * softplus
    a = jnp.exp(log_a)
    th = jnp.tanh(log_a)
    num = -2.0 * th
    mult = jnp.where(num > 0.0, num * lax.rsqrt(num * (1.0 - th)), 0.0)
    u = mult * (ig * xc)

    def scan_steps(a, u, pos, length, axis):
        dist = 1
        while dist < length:
            keep = pos >= dist
            u = jnp.where(keep, a * pltpu.roll(u, dist, axis) + u, u)
            a = jnp.where(keep, a * pltpu.roll(a, dist, axis), a)
            dist *= 2
        return a, u

    n_grp_t = t_tile // SUBLANES
    grouped = (n_grp_t, SUBLANES, w)
    a, u = scan_steps(a.reshape(grouped), u.reshape(grouped), lax.broadcasted_iota(I32, grouped, 1), SUBLANES, 1)
    a, u = a.reshape(x.shape), u.reshape(x.shape)
    n_lt = w // LANES
    for c in range(n_lt):
        a_scr[c] = a[:, c * LANES:(c + 1) * LANES]
        u_scr[c] = u[:, c * LANES:(c + 1) * LANES]
    last = pl.ds(SUBLANES - 1, n_grp_t, stride=SUBLANES)
    grow = lax.broadcasted_iota(I32, (n_grp_t, w), 0)
    ag, ug = scan_steps(jnp.concatenate([a_scr[c, last, :] for c in range(n_lt)], axis=1),
                        jnp.concatenate([u_scr[c, last, :] for c in range(n_lt)], axis=1), grow, n_grp_t, 0)
    h_end = ag * hcar_ref[...] + ug
    h_in = jnp.where(grow == 0, hcar_ref[...], pltpu.roll(h_end, 1, 0))
    hcar_ref[...] = h_end[n_grp_t - 1:]
    for c in range(n_lt):
        for k in range(SUBLANES):
            a_scr[c, pl.ds(k, n_grp_t, stride=SUBLANES), :] = h_in[:, c * LANES:(c + 1) * LANES]
    h = a * jnp.concatenate([a_scr[c] for c in range(n_lt)], axis=1) + u
    o_ref[...] = (h * jax.nn.gelu(gr_ref[...])).astype(o_ref.dtype)


def _block_diag_groups(wblk):
    nb, bw, _ = wblk.shape
    per = MXU_DIM // bw
    g = nb // per
    eye = jnp.eye(per, dtype=wblk.dtype)
    w5 = wblk.reshape(g, per, bw, bw)
    return jnp.einsum("gawv,ab->gawbv", w5, eye).reshape(g, MXU_DIM, MXU_DIM).astype(BF16)


def _lru(xr, gr, w_conv, b_conv, w_rg, b_rg, w_ig, b_ig, lru_lambda, b, s):
    n, w = xr.shape
    t = T_LRU
    n_t = s // t
    row_spec = pl.BlockSpec((t, w), lambda bi, ti: (bi * n_t + ti, 0))
    vec = lambda r: pl.BlockSpec((r, w), lambda bi, ti: (0, 0))
    blk = pl.BlockSpec((w // MXU_DIM, MXU_DIM, MXU_DIM), lambda bi, ti: (0, 0, 0))
    return pl.pallas_call(
        functools.partial(_lru_kernel, t_tile=t),
        grid=(b, n_t),
        in_specs=[row_spec, row_spec, vec(CONV_WIDTH), vec(1), blk, vec(1), blk, vec(1), vec(1)],
        out_specs=row_spec,
        out_shape=jax.ShapeDtypeStruct((n, w), BF16),
        scratch_shapes=[pltpu.VMEM((SUBLANES, w), F32), pltpu.VMEM((1, w), F32),
                        pltpu.VMEM((w // LANES, t, LANES), F32), pltpu.VMEM((w // LANES, t, LANES), F32)],
        compiler_params=pltpu.CompilerParams(dimension_semantics=("arbitrary", "arbitrary")),
        name="lru",
    )(xr, gr, w_conv, b_conv.reshape(1, w), _block_diag_groups(w_rg), b_rg.reshape(1, w),
      _block_diag_groups(w_ig), b_ig.reshape(1, w), lru_lambda.reshape(1, w))


def _split_bf16(x):
    hi = x.astype(BF16)
    return hi, (x - hi.astype(F32)).astype(BF16)


def _to_token_tiles(ref, x, first_token=0):
    tm, d = x.shape
    rpt = d // LANES
    for c in range(rpt):
        ref[pl.ds(first_token * rpt + c, tm, stride=rpt), :] = x[:, c * LANES:(c + 1) * LANES]


def _from_token_tiles(ref, tm, rpt, first_token=0):
    return jnp.concatenate([ref[pl.ds(first_token * rpt + c, tm, stride=rpt), :] for c in range(rpt)], axis=1)


def _merge_kernel(o_ref, hg_ref, ga_ref, gb_ref, x_ref, wa_ref, wl_ref, wo_ref, gm_ref, wrh_ref, wrl_ref, br_ref,
                  x1_ref, h2_ref, lg_ref):
    attn_br = jnp.dot(o_ref[...], wa_ref[...], preferred_element_type=F32)
    lru_br = jnp.dot(hg_ref[...], wl_ref[...], preferred_element_type=F32)
    mixed = _sigmoid(ga_ref[...]) * attn_br + _sigmoid(gb_ref[...]) * lru_br
    x1 = x_ref[...] + jnp.dot(mixed.astype(BF16), wo_ref[...], preferred_element_type=F32)
    x1_ref[...] = x1
    h2 = _rmsnorm(x1, gm_ref[...], NORM_EPS)
    _to_token_tiles(h2_ref, h2)
    hi, lo = _split_bf16(h2)
    wrh = wrh_ref[...]
    lg_ref[...] = (jnp.dot(hi, wrh, preferred_element_type=F32) + jnp.dot(lo, wrh, preferred_element_type=F32)
                   + jnp.dot(hi, wrl_ref[...], preferred_element_type=F32)) + br_ref[...]


def _merge(o, hg, ga, gb, x2, wa, wl, wo, g_moe, w_router, b_router):
    n, d = x2.shape
    tm = TM_MERGE
    rpt = d // LANES
    row = pl.BlockSpec((tm, d), lambda i: (i, 0))
    full = lambda r, c: pl.BlockSpec((r, c), lambda i: (0, 0))
    wr_hi, wr_lo = _split_bf16(w_router)
    return pl.pallas_call(
        _merge_kernel,
        grid=(n // tm,),
        in_specs=[row, row, row, row, row, full(d, d), full(d, d), full(d, d), full(1, d),
                  full(d, LANES), full(d, LANES), full(1, LANES)],
        out_specs=[row, pl.BlockSpec((tm * rpt, LANES), lambda i: (i, 0)), pl.BlockSpec((tm, LANES), lambda i: (i, 0))],
        out_shape=[jax.ShapeDtypeStruct((n, d), F32), jax.ShapeDtypeStruct((n * rpt, LANES), F32),
                   jax.ShapeDtypeStruct((n, LANES), F32)],
        compiler_params=pltpu.CompilerParams(dimension_semantics=("arbitrary",)),
        name="merge",
    )(o, hg, ga, gb, x2, wa, wl, wo, g_moe, wr_hi, wr_lo, b_router)


ROUTE_LOGIT_ROWS = 40


def _row_pick(x, rows, idx):
    return jnp.sum(jnp.where(rows == idx, x, jnp.zeros_like(x)), axis=0, keepdims=True)


def _route_kernel(lg_ref, pos_ref, wts_ref, tab_ref, *, n, t_tile, seg_tile):
    row = lax.broadcasted_iota(I32, (ROUTE_LOGIT_ROWS, t_tile), 0)
    erow = lax.broadcasted_iota(I32, (N_EXPERTS, t_tile), 0)
    prow = lax.broadcasted_iota(I32, (SUBLANES, t_tile), 0)
    wrow = lax.broadcasted_iota(I32, (LANES, t_tile), 0)
    neg_inf = jnp.float32(-jnp.inf)
    earlier = (lax.broadcasted_iota(I32, (t_tile, t_tile), 0) < lax.broadcasted_iota(I32, (t_tile, t_tile), 1)).astype(BF16)

    def phase1(t, cnt):
        start = pl.multiple_of(t * t_tile, t_tile)
        lg = lg_ref[pl.ds(start, t_tile), :].T[:ROUTE_LOGIT_ROWS]
        is_grp = row < N_GROUPS
        gl = jnp.where(is_grp, lg, neg_inf)
        gmax = jnp.max(gl, axis=0, keepdims=True)
        g_idx = jnp.min(jnp.where(gl == gmax, row, LANES), axis=0, keepdims=True)
        g_w = 1.0 / jnp.sum(jnp.where(is_grp, jnp.exp(lg - gmax), 0.0), axis=0, keepdims=True)
        lo = N_GROUPS + EXPERTS_PER_GROUP * g_idx
        in_grp = (row >= lo) & (row < lo + EXPERTS_PER_GROUP)
        fl = jnp.where(in_grp, lg, neg_inf)
        v1 = jnp.max(fl, axis=0, keepdims=True)
        i1 = jnp.min(jnp.where(in_grp & (fl == v1), row, LANES), axis=0, keepdims=True)
        rest = in_grp & (row != i1)
        fl2 = jnp.where(rest, lg, neg_inf)
        v2 = jnp.max(fl2, axis=0, keepdims=True)
        i2 = jnp.min(jnp.where(rest & (fl2 == v2), row, LANES), axis=0, keepdims=True)
        t2 = jnp.exp(v2 - v1)
        den = 1.0 + t2
        w1 = g_w * (1.0 / den)
        w2 = g_w * (t2 / den)
        e1 = i1 - N_GROUPS
        e2 = i2 - N_GROUPS
        onehot = ((erow == e1) | (erow == e2)).astype(F32)
        before = jnp.dot(onehot.astype(BF16), earlier, preferred_element_type=F32) + cnt
        rank1 = _row_pick(before, erow, e1).astype(I32)
        rank2 = _row_pick(before, erow, e2).astype(I32)
        pos_ref[:, pl.ds(start, t_tile)] = jnp.where(prow == 0, e1, jnp.where(prow == 1, e2, jnp.where(
            prow == 2, rank1, jnp.where(prow == 3, rank2, 0))))
        wts_ref[pl.ds(start, t_tile), :] = jnp.where(wrow == 0, w1, jnp.where(wrow == 1, w2, 0.0)).T
        return cnt + jnp.sum(onehot, axis=1, keepdims=True)

    cnt = lax.fori_loop(0, n // t_tile, phase1, jnp.zeros((N_EXPERTS, 1), F32))

    rows_sq = lax.broadcasted_iota(I32, (LANES, LANES), 0)
    lanes_sq = lax.broadcasted_iota(I32, (LANES, LANES), 1)
    cnt_sq = jnp.broadcast_to(jnp.concatenate([cnt, jnp.zeros((LANES - N_EXPERTS, 1), F32)], axis=0), (LANES, LANES))
    padded = jnp.ceil(cnt_sq * (1.0 / seg_tile)) * seg_tile
    incl = padded
    dist = 1
    while dist < LANES:
        incl = incl + jnp.where(rows_sq >= dist, pltpu.roll(incl, dist, 0), 0.0)
        dist *= 2
    offs = incl - padded
    tab_ref[...] = jnp.where(lanes_sq == 0, offs, jnp.where(lanes_sq == 1, cnt_sq, 0.0)).astype(I32)
    offs_col = offs[:N_EXPERTS, :1].astype(I32)

    def phase2(t, carry):
        cols = pl.ds(pl.multiple_of(t * t_tile, t_tile), t_tile)
        info = pos_ref[:, cols]
        offs_b = jnp.broadcast_to(offs_col, (N_EXPERTS, t_tile))
        p1 = info[2:3] + _row_pick(offs_b, erow, info[0:1])
        p2 = info[3:4] + _row_pick(offs_b, erow, info[1:2])
        pos_ref[:, cols] = jnp.where(prow == 0, p1, jnp.where(prow == 1, p2, 0))
        return carry

    lax.fori_loop(0, n // t_tile, phase2, 0)


def _route(logits):
    n = logits.shape[0]
    return pl.pallas_call(
        functools.partial(_route_kernel, n=n, t_tile=T_ROUTE, seg_tile=TM_EXPERT),
        out_shape=[jax.ShapeDtypeStruct((SUBLANES, n), I32), jax.ShapeDtypeStruct((n, LANES), F32),
                   jax.ShapeDtypeStruct((LANES, LANES), I32)],
        name="route",
    )(logits)


def _token_rows(idx, rpt):
    return pl.ds(pl.multiple_of(idx * rpt, rpt), rpt)


def _dispatch_kernel(seg_ref, cnt_ref, p1_ref, p2_ref, h_ref, hs_ref, zero_scr, sem, *, tm, rpt, n_tiles,
                     seg_tile, cap_tiles):
    def issue(r, carry):
        src = h_ref.at[_token_rows(r, rpt)]
        pltpu.make_async_copy(src, hs_ref.at[_token_rows(p1_ref[r], rpt)], sem).start(priority=0)
        pltpu.make_async_copy(src, hs_ref.at[_token_rows(p2_ref[r], rpt)], sem).start(priority=1)
        return carry

    lax.fori_loop(0, tm, issue, 0, unroll=DMA_ISSUE_UNROLL)
    whole = pltpu.make_async_copy(h_ref, hs_ref.at[pl.ds(0, tm * rpt)], sem)
    whole.wait()
    whole.wait()

    @pl.when(pl.program_id(0) == n_tiles - 1)
    def _():
        zero_scr[...] = jnp.zeros_like(zero_scr)
        zero_row = zero_scr.at[pl.ds(0, rpt)]

        def per_expert(e, carry):
            first_pad = seg_ref[e] + cnt_ref[e]
            n_pad = seg_ref[e + 1] - first_pad

            def fill(r, c):
                pltpu.make_async_copy(zero_row, hs_ref.at[_token_rows(first_pad + r, rpt)], sem).start()
                return c

            def drain(r, c):
                pltpu.make_async_copy(zero_row, hs_ref.at[_token_rows(first_pad + r, rpt)], sem).wait()
                return c

            lax.fori_loop(0, n_pad, fill, 0)
            lax.fori_loop(0, n_pad, drain, 0)
            return carry

        lax.fori_loop(0, N_EXPERTS, per_expert, 0)

        used_tiles = seg_ref[N_EXPERTS] // seg_tile

        def tail_tile(c):
            return hs_ref.at[pl.ds(pl.multiple_of((used_tiles + c) * (seg_tile * rpt), seg_tile * rpt), seg_tile * rpt)]

        def fill_tail(c, carry):
            pltpu.make_async_copy(zero_scr, tail_tile(c), sem).start()
            return carry

        def drain_tail(c, carry):
            pltpu.make_async_copy(zero_scr, tail_tile(c), sem).wait()
            return carry

        lax.fori_loop(0, cap_tiles - used_tiles, fill_tail, 0)
        lax.fori_loop(0, cap_tiles - used_tiles, drain_tail, 0)


def _dispatch(seg_start, counts, pos1, pos2, h2t, rpt):
    n = h2t.shape[0] // rpt
    tm = TM_DISPATCH
    n_tiles = n // tm
    idx = pl.BlockSpec((tm,), lambda i, *_: (i,), memory_space=pltpu.SMEM)
    grid_spec = pltpu.PrefetchScalarGridSpec(
        num_scalar_prefetch=2,
        grid=(n_tiles,),
        in_specs=[idx, idx, pl.BlockSpec((tm * rpt, LANES), lambda i, *_: (i, 0))],
        out_specs=pl.BlockSpec(memory_space=pl.ANY),
        scratch_shapes=[pltpu.VMEM((TM_EXPERT * rpt, LANES), F32), pltpu.SemaphoreType.DMA(())],
    )
    cap_tiles = 2 * n // TM_EXPERT + N_EXPERTS
    return pl.pallas_call(
        functools.partial(_dispatch_kernel, tm=tm, rpt=rpt, n_tiles=n_tiles, seg_tile=TM_EXPERT, cap_tiles=cap_tiles),
        grid_spec=grid_spec,
        out_shape=jax.ShapeDtypeStruct((cap_tiles * TM_EXPERT * rpt, LANES), F32),
        compiler_params=pltpu.CompilerParams(dimension_semantics=("arbitrary",), has_side_effects=True),
        name="dispatch",
    )(seg_start, counts, pos1, pos2, h2t)


def _expert_kernel(t_exp, t_valid, t_new, t_slot, t_next, hs_ref, wg_hbm, wu_hbm, wd_hbm, ys_ref,
                   wg_f32, wu_f32, wd_f32, wg_bf, wu_bf, wd_bf, sems, *, tm, rpt, tiles_per_step):
    def weight_copies(e, slot):
        return [pltpu.make_async_copy(src.at[e], dst.at[slot], sems.at[slot])
                for src, dst in ((wg_hbm, wg_f32), (wu_hbm, wu_f32), (wd_hbm, wd_f32))]

    def one_tile(t, first_token):
        @pl.when(t_valid[t] == 1)
        def _():
            @pl.when(t_new[t] == 1)
            def _():
                slot = t_slot[t]

                @pl.when(t == 0)
                def _():
                    for c in weight_copies(t_exp[t], slot):
                        c.start()

                for c in weight_copies(t_exp[t], slot):
                    c.wait()
                wg_bf[...] = wg_f32[slot].astype(BF16)
                wu_bf[...] = wu_f32[slot].astype(BF16)
                wd_bf[...] = wd_f32[slot].astype(BF16)

                @pl.when(t_next[t] >= 0)
                def _():
                    for c in weight_copies(t_next[t], 1 - slot):
                        c.start()

            h = _from_token_tiles(hs_ref, tm, rpt, first_token).astype(BF16)
            gate = jnp.dot(h, wg_bf[...], preferred_element_type=F32)
            up = jnp.dot(h, wu_bf[...], preferred_element_type=F32)
            hid = ((gate * _sigmoid(gate)) * up).astype(BF16)
            _to_token_tiles(ys_ref, jnp.dot(hid, wd_bf[...], preferred_element_type=F32), first_token)

        @pl.when(t_valid[t] == 0)
        def _():
            ys_ref[pl.ds(first_token * rpt, tm * rpt), :] = jnp.zeros((tm * rpt, LANES), F32)

    for k in range(tiles_per_step):
        one_tile(pl.program_id(0) * tiles_per_step + k, k * tm)


def _tile_table(seg_start, n_tiles, tm):
    total = seg_start[N_EXPERTS] // tm
    t = jnp.arange(n_tiles, dtype=I32)
    blk = jnp.minimum(t, total - 1)
    exp = jnp.sum(seg_start[None, 1:N_EXPERTS + 1] <= (blk * tm)[:, None], axis=1).astype(I32)
    valid = (t < total).astype(I32)
    new = (exp != jnp.concatenate([jnp.full((1,), -1, I32), exp[:-1]])).astype(I32)
    slot = (jnp.cumsum(new) - 1) % 2
    later = jnp.where(exp[None, :] > exp[:, None], exp[None, :], N_EXPERTS)
    nxt = jnp.min(later, axis=1)
    nxt = jnp.where(nxt < N_EXPERTS, nxt, -1).astype(I32)
    return exp, valid, new, slot.astype(I32), nxt


def _experts(hs, seg_start, w_e_gate, w_e_up, w_e_down, rpt):
    d, ff = w_e_gate.shape[-2:]
    tm = TM_EXPERT
    n_tiles = hs.shape[0] // (rpt * tm)
    table = _tile_table(seg_start, n_tiles, tm)
    hbm = pl.BlockSpec(memory_space=pl.ANY)
    per_step = EXPERT_TILES_PER_STEP
    assert n_tiles % per_step == 0
    rows = pl.BlockSpec((per_step * tm * rpt, LANES), lambda s, *_: (s, 0))
    grid_spec = pltpu.PrefetchScalarGridSpec(
        num_scalar_prefetch=len(table),
        grid=(n_tiles // per_step,),
        in_specs=[rows, hbm, hbm, hbm],
        out_specs=rows,
        scratch_shapes=[pltpu.VMEM((2, d, ff), F32), pltpu.VMEM((2, d, ff), F32), pltpu.VMEM((2, ff, d), F32),
                        pltpu.VMEM((d, ff), BF16), pltpu.VMEM((d, ff), BF16), pltpu.VMEM((ff, d), BF16),
                        pltpu.SemaphoreType.DMA((2,))],
    )
    return pl.pallas_call(
        functools.partial(_expert_kernel, tm=tm, rpt=rpt, tiles_per_step=per_step),
        grid_spec=grid_spec,
        out_shape=jax.ShapeDtypeStruct(hs.shape, F32),
        compiler_params=pltpu.CompilerParams(dimension_semantics=("arbitrary",)),
        name="experts",
    )(*table, hs, w_e_gate, w_e_up, w_e_down)


def _combine_kernel(p1_ref, p2_ref, p1n_ref, p2n_ref, wts_ref, x1_ref, p_ref, gp_ref, wpg_ref, wpp_ref, gf_ref,
                    ys_ref, out_ref, ybuf, sems, *, tm, rpt, n_tiles):
    i = pl.program_id(0)
    slot = i % 2

    def gather(a_ref, b_ref, dst):
        def issue(r, carry):
            rows = _token_rows(r, rpt)
            pltpu.make_async_copy(ys_ref.at[_token_rows(a_ref[r], rpt)], ybuf.at[dst, 0, rows],
                                  sems.at[dst]).start(priority=0)
            pltpu.make_async_copy(ys_ref.at[_token_rows(b_ref[r], rpt)], ybuf.at[dst, 1, rows],
                                  sems.at[dst]).start(priority=1)
            return carry

        lax.fori_loop(0, tm, issue, 0, unroll=DMA_ISSUE_UNROLL)

    @pl.when(i == 0)
    def _():
        gather(p1_ref, p2_ref, 0)

    @pl.when(i + 1 < n_tiles)
    def _():
        gather(p1n_ref, p2n_ref, 1 - slot)

    for half in range(2):
        pltpu.make_async_copy(ys_ref.at[pl.ds(0, tm * rpt)], ybuf.at[slot, half], sems.at[slot]).wait()

    wts = wts_ref[...]
    y1 = _from_token_tiles(ybuf.at[slot, 0], tm, rpt)
    y2 = _from_token_tiles(ybuf.at[slot, 1], tm, rpt)
    x2 = x1_ref[...] + (wts[:, 0:1] * y1 + wts[:, 1:2] * y2)
    hp = _rmsnorm(x2, gp_ref[...], NORM_EPS).astype(BF16)
    gate = _sigmoid(jnp.dot(hp, wpg_ref[...], preferred_element_type=F32))
    proj = jnp.dot(p_ref[...].astype(BF16), wpp_ref[...], preferred_element_type=F32)
    x3 = x2 + gate * proj
    out_ref[...] = _rmsnorm(x3, gf_ref[...], NORM_EPS)


def _combine(pos1, pos2, wts, x1, p2d, g_ple, wpg, wpp, g_final, ys, rpt):
    n, d = x1.shape
    pd = p2d.shape[1]
    tm = TM_COMBINE
    n_tiles = n // tm
    idx = pl.BlockSpec((tm,), lambda i: (i,), memory_space=pltpu.SMEM)
    idx_next = pl.BlockSpec((tm,), lambda i: (jnp.minimum(i + 1, n_tiles - 1),), memory_space=pltpu.SMEM)
    row = pl.BlockSpec((tm, d), lambda i: (i, 0))
    full = lambda r, c: pl.BlockSpec((r, c), lambda i: (0, 0))
    return pl.pallas_call(
        functools.partial(_combine_kernel, tm=tm, rpt=rpt, n_tiles=n_tiles),
        grid=(n_tiles,),
        in_specs=[idx, idx, idx_next, idx_next, pl.BlockSpec((tm, LANES), lambda i: (i, 0)), row,
                  pl.BlockSpec((tm, pd), lambda i: (i, 0)), full(1, d), full(d, d), full(pd, d), full(1, d),
                  pl.BlockSpec(memory_space=pl.ANY)],
        out_specs=row,
        out_shape=jax.ShapeDtypeStruct((n, d), F32),
        scratch_shapes=[pltpu.VMEM((2, 2, tm * rpt, LANES), F32), pltpu.SemaphoreType.DMA((2,))],
        compiler_params=pltpu.CompilerParams(dimension_semantics=("arbitrary",)),
        name="combine",
    )(pos1, pos2, pos1, pos2, wts, x1, p2d, g_ple, wpg, wpp, g_final, ys)


def _layer(i, x2, p2d, b, s, g_mix, w_in, lam_q1, lam_k1, lam_q2, lam_k2, g_subln, w_conv, b_conv, w_rg, b_rg,
           w_ig, b_ig, lru_lambda, w_attn_br, w_lru_br, w_out, g_moe, w_rt_group, b_rt_group, w_rt_expert,
           b_rt_expert, w_e_gate, w_e_up, w_e_down, g_ple, w_ple_gate, w_ple_proj):
    n, d = x2.shape
    lam_init = 0.8 - 0.6 * math.exp(-0.3 * i)
    row = lambda a: a.reshape(1, -1)

    q, k, v, xr, gr, ga, gb = _inproj(x2, row(g_mix), w_in.astype(BF16))
    o = _attention(q.reshape(b, s, d), k.reshape(b, s, d), v.reshape(b, s, d),
                   row(lam_q1), row(lam_k1), row(lam_q2), row(lam_k2), row(g_subln), lam_init).reshape(n, d)
    hg = _lru(xr, gr, w_conv, b_conv, w_rg, b_rg, w_ig, b_ig, lru_lambda, b, s)

    pad = LANES - N_GROUPS - N_EXPERTS
    w_router = jnp.concatenate(
        [w_rt_group, w_rt_expert.transpose(1, 0, 2).reshape(d, N_EXPERTS), jnp.zeros((d, pad), F32)], axis=1)
    b_router = jnp.concatenate([b_rt_group, b_rt_expert.reshape(N_EXPERTS), jnp.zeros((pad,), F32)]).reshape(1, LANES)
    x1, h2, logits = _merge(o, hg, ga, gb, x2, w_attn_br.astype(BF16), w_lru_br.astype(BF16), w_out.astype(BF16),
                            row(g_moe), w_router, b_router)

    pos, wts, tab = _route(logits)
    pos1, pos2 = pos[0], pos[1]
    rpt = d // LANES
    seg_start, counts = tab[:N_EXPERTS + 1, 0], tab[:N_EXPERTS, 1]
    hs = _dispatch(seg_start, counts, pos1, pos2, h2, rpt)
    ys = _experts(hs, seg_start, w_e_gate, w_e_up, w_e_down, rpt)
    return pos1, pos2, wts, x1, ys


def kernel(x, p, g_mix, w_in, lam_q1, lam_k1, lam_q2, lam_k2, g_subln, w_conv, b_conv, w_rg, b_rg, w_ig, b_ig, lru_lambda, w_attn_br, w_lru_br, w_out, g_moe, w_rt_group, b_rt_group, w_rt_expert, b_rt_expert, w_e_gate, w_e_up, w_e_down, g_ple, w_ple_gate, w_ple_proj, g_final):
    b, s, d = x.shape
    depth = p.shape[0]
    assert depth == 1, "the final RMSNorm is fused into the last layer's combine step; one layer supported"
    n = b * s
    x2 = x.reshape(n, d)
    i = 0
    p2d = p[i].reshape(n, -1)
    pos1, pos2, wts, x1, ys = _layer(
        i, x2, p2d, b, s, g_mix[i], w_in[i], lam_q1[i], lam_k1[i], lam_q2[i], lam_k2[i], g_subln[i], w_conv[i],
        b_conv[i], w_rg[i], b_rg[i], w_ig[i], b_ig[i], lru_lambda[i], w_attn_br[i], w_lru_br[i], w_out[i],
        g_moe[i], w_rt_group[i], b_rt_group[i], w_rt_expert[i], b_rt_expert[i], w_e_gate[i], w_e_up[i],
        w_e_down[i], g_ple[i], w_ple_gate[i], w_ple_proj[i])
    out = _combine(pos1, pos2, wts, x1, p2d, g_ple[i].reshape(1, d), w_ple_gate[i].astype(BF16),
                   w_ple_proj[i].astype(BF16), g_final.reshape(1, d), ys, d // LANES)
    return out.reshape(b, s, d)
```

```python
import functools
import math

import jax
import jax.numpy as jnp
from jax import lax
from jax.experimental import pallas as pl
from jax.experimental.pallas import tpu as pltpu

F32 = jnp.float32
BF16 = jnp.bfloat16
I32 = jnp.int32

N_HEADS = 8
HEAD_DIM = 64
V_DIM = 2 * HEAD_DIM
SUBLN_EPS = 1e-5
NORM_EPS = 1e-6
LRU_BLOCKS = 16
CONV_WIDTH = 4
LRU_C = 8.0
N_GROUPS = 4
EXPERTS_PER_GROUP = 8
N_EXPERTS = N_GROUPS * EXPERTS_PER_GROUP
N_IN_PARTS = 7
LOG2_E = 1.4426950408889634
Q_SCALE = HEAD_DIM ** -0.5 * LOG2_E

LANES = 128
SUBLANES = 8
BF16_SUBLANES = 16
MXU_DIM = 256

TM_INPROJ = 512
T_ATTN = 256
T_LRU = 512
TM_MERGE = 512
T_ROUTE = 512
TM_DISPATCH = 1024
TM_COMBINE = 256
TM_EXPERT = 256
EXPERT_TILES_PER_STEP = 4
DMA_ISSUE_UNROLL = 8
MASK_NEG = -1e30


def _sigmoid(x):
    return 0.5 * jnp.tanh(0.5 * x) + 0.5


def _rmsnorm(x, g, eps):
    return (x * lax.rsqrt(jnp.mean(x * x, axis=-1, keepdims=True) + eps)) * g


def _inproj_kernel(x_ref, g_ref, w_ref, *out_refs, d):
    h = _rmsnorm(x_ref[...], g_ref[...], NORM_EPS).astype(BF16)
    for c, o_ref in enumerate(out_refs):
        z = jnp.dot(h, w_ref[:, c * d:(c + 1) * d], preferred_element_type=F32)
        if c == 0:
            z = z * Q_SCALE
        o_ref[...] = z.astype(o_ref.dtype)


def _inproj(x2, g_mix, w_in_bf):
    n, d = x2.shape
    tm = TM_INPROJ
    out_dtypes = (BF16, BF16, BF16, F32, F32, F32, F32)
    row_spec = pl.BlockSpec((tm, d), lambda i: (i, 0))
    return pl.pallas_call(
        functools.partial(_inproj_kernel, d=d),
        grid=(n // tm,),
        in_specs=[
            row_spec,
            pl.BlockSpec((1, d), lambda i: (0, 0)),
            pl.BlockSpec((d, N_IN_PARTS * d), lambda i: (0, 0), pipeline_mode=pl.Buffered(1)),
        ],
        out_specs=[row_spec] * N_IN_PARTS,
        out_shape=[jax.ShapeDtypeStruct((n, d), dt) for dt in out_dtypes],
        compiler_params=pltpu.CompilerParams(dimension_semantics=("arbitrary",)),
        name="inproj",
    )(x2, g_mix, w_in_bf)


V_AUG = V_DIM + BF16_SUBLANES
PIPE_GROUP = 4
PIPE_LAG = 2
PIPE_RING = 2 * PIPE_LAG
PIPE_SLOTS = PIPE_GROUP * PIPE_RING


def _attn_kernel(lq1_ref, lk1_ref, lq2_ref, lk2_ref, g_ref, q_ref, k_ref, v_ref, o_ref,
                 qt_scr, vt_scr, acc_scr, m_scr, bias_scr, s_scr, p_scr, cmax_scr, alpha_scr, *, t, lam_init):
    nq = q_ref.shape[0] // t
    w = 2 * t

    feat = lax.broadcasted_iota(I32, (t, V_DIM), 1)
    ones_pad = jnp.where(lax.broadcasted_iota(I32, (V_AUG - V_DIM, t), 0) == 0, 1.0, 0.0).astype(F32)

    def prep(i, carry):
        rows = pl.ds(pl.multiple_of(i * t, t), t)
        q = q_ref[rows, :].astype(F32)
        q1t = jnp.where(feat < HEAD_DIM, q, 0.0).T
        q2t = jnp.where(feat >= HEAD_DIM, q, 0.0).T
        qt_scr[i] = jnp.concatenate([q1t, q2t], axis=1).astype(BF16)
        vt = v_ref[rows, :].astype(F32).T
        vt_scr[i] = jnp.concatenate([vt, ones_pad], axis=0).astype(BF16)
        m_scr[i] = jnp.full((1, w), MASK_NEG, F32)
        acc_scr[i] = jnp.zeros((V_AUG, w), F32)
        return carry

    lax.fori_loop(0, nq, prep, 0, unroll=8)
    key = lax.broadcasted_iota(I32, (t, w), 0)
    qry = lax.broadcasted_iota(I32, (t, w), 1) & (t - 1)
    bias_scr[...] = jnp.where(key <= qry, 0.0, MASK_NEG)

    def stage_a(j, i, slot, masked):
        kj = k_ref[pl.ds(pl.multiple_of(j * t, t), t), :]
        s = jnp.dot(kj, qt_scr[i], preferred_element_type=F32)
        if masked:
            s = s + bias_scr[...]
        s_scr[slot] = s
        cmax_scr[slot] = jnp.max(s, axis=0, keepdims=True)

    def stage_b(i, slot):
        m_old = m_scr[i]
        m_new = jnp.maximum(m_old, cmax_scr[slot])
        m_scr[i] = m_new
        p_scr[slot] = jnp.exp2(s_scr[slot] - m_new).astype(BF16)
        alpha_scr[slot] = jnp.exp2(m_old - m_new)

    def stage_c(j, i, slot):
        acc_scr[i] = alpha_scr[slot] * acc_scr[i] + jnp.dot(vt_scr[j], p_scr[slot], preferred_element_type=F32)

    def run(n_steps, n_masked, first, nxt):
        body_steps = PIPE_RING * PIPE_GROUP
        assert n_steps % body_steps == 0 and n_masked % body_steps == 0 and n_masked >= body_steps
        n_groups = n_steps // PIPE_GROUP
        n_masked_groups = n_masked // PIPE_GROUP

        def steps_of(st):
            out = []
            for _ in range(PIPE_GROUP):
                out.append(st)
                st = nxt(*st)
            return out, st

        def turn(t_mod, a=None, b=None, c=None, masked=False):
            base_a = (t_mod % PIPE_RING) * PIPE_GROUP
            base_b = ((t_mod - PIPE_LAG) % PIPE_RING) * PIPE_GROUP
            a_steps, following = steps_of(a) if a is not None else (None, None)
            b_steps = steps_of(b)[0] if b is not None else None
            c_steps = steps_of(c)[0] if c is not None else None
            for kk in range(PIPE_GROUP):
                if a_steps is not None:
                    stage_a(*a_steps[kk], base_a + kk, masked)
            for kk in range(PIPE_GROUP):
                if b_steps is not None:
                    stage_b(b_steps[kk][1], base_b + kk)
            for kk in range(PIPE_GROUP):
                if c_steps is not None:
                    stage_c(*c_steps[kk], base_a + kk)
            return following

        f = [first]
        for tt in range(PIPE_RING):
            f.append(turn(tt, a=f[tt], b=f[tt - PIPE_LAG] if tt >= PIPE_LAG else None, masked=True))

        def ring(_, f, masked):
            f = list(f)
            for r in range(PIPE_RING):
                f.append(turn(r, a=f[-1], b=f[-1 - PIPE_LAG], c=f[-1 - PIPE_RING], masked=masked))
            return tuple(f[-(PIPE_RING + 1):])

        f = lax.fori_loop(0, (n_masked_groups - PIPE_RING) // PIPE_RING, functools.partial(ring, masked=True), tuple(f))
        f = lax.fori_loop(0, (n_groups - n_masked_groups) // PIPE_RING, functools.partial(ring, masked=False), f)
        for r in range(PIPE_RING):
            turn(r, b=f[r + PIPE_LAG] if r < PIPE_LAG else None, c=f[r])

    def next_step(j, i):
        wrap = i + 1 >= nq
        return jnp.where(wrap, 0, j + 1), jnp.where(wrap, i - j + 1, i + 1)

    zero = jnp.int32(0)
    run(nq * (nq + 1) // 2, nq, (zero, zero), next_step)

    lam = (jnp.exp(jnp.sum(lq1_ref[...] * lk1_ref[...], axis=1, keepdims=True))
           - jnp.exp(jnp.sum(lq2_ref[...] * lk2_ref[...], axis=1, keepdims=True)) + lam_init)
    gain = g_ref[...]

    def finish(i, carry):
        acc = acc_scr[i]
        o1 = acc[:V_DIM, :t] / acc[V_DIM:V_DIM + 1, :t]
        o2 = acc[:V_DIM, t:] / acc[V_DIM:V_DIM + 1, t:]
        o = o1 - lam * o2
        y = ((o * lax.rsqrt(jnp.mean(o * o, axis=0, keepdims=True) + SUBLN_EPS)) * gain) * (1.0 - lam_init)
        o_ref[pl.ds(pl.multiple_of(i * t, t), t), :] = y.T.astype(o_ref.dtype)
        return carry

    lax.fori_loop(0, nq, finish, 0, unroll=8)


def _attention(q, k, v, lam_q1, lam_k1, lam_q2, lam_k2, g_subln, lam_init):
    b, s, d = q.shape
    t = T_ATTN
    assert s % t == 0 and d == N_HEADS * V_DIM
    nq = s // t
    small = lambda w: pl.BlockSpec((1, w), lambda bi, hi: (0, 0))
    head = pl.BlockSpec((None, s, V_DIM), lambda bi, hi: (bi, 0, hi))
    return pl.pallas_call(
        functools.partial(_attn_kernel, t=t, lam_init=lam_init),
        grid=(b, N_HEADS),
        in_specs=[small(HEAD_DIM), small(HEAD_DIM), small(HEAD_DIM), small(HEAD_DIM),
                  pl.BlockSpec((V_DIM, 1), lambda bi, hi: (0, 0)), head, head, head],
        out_specs=head,
        out_shape=jax.ShapeDtypeStruct((b, s, d), BF16),
        scratch_shapes=[
            pltpu.VMEM((nq, V_DIM, 2 * t), BF16),
            pltpu.VMEM((nq, V_AUG, t), BF16),
            pltpu.VMEM((nq, V_AUG, 2 * t), F32),
            pltpu.VMEM((nq, 1, 2 * t), F32),
            pltpu.VMEM((t, 2 * t), F32),
            pltpu.VMEM((PIPE_SLOTS, t, 2 * t), F32),
            pltpu.VMEM((PIPE_SLOTS, t, 2 * t), BF16),
            pltpu.VMEM((PIPE_SLOTS, 1, 2 * t), F32),
            pltpu.VMEM((PIPE_SLOTS, 1, 2 * t), F32),
        ],
        compiler_params=pltpu.CompilerParams(dimension_semantics=("arbitrary", "arbitrary")),
        name="attn",
    )(lam_q1, lam_k1, lam_q2, lam_k2, g_subln.reshape(V_DIM, 1), q, k, v)


def _lru_kernel(xr_ref, gr_ref, wc_ref, bc_ref, wrg_ref, brg_ref, wig_ref, big_ref, lam_ref, o_ref,
                xprev_ref, hcar_ref, a_scr, u_scr, *, t_tile):
    @pl.when(pl.program_id(1) == 0)
    def _():
        xprev_ref[...] = jnp.zeros_like(xprev_ref)
        hcar_ref[...] = jnp.zeros_like(hcar_ref)

    x = xr_ref[...]
    w = x.shape[1]
    prev = xprev_ref[...]
    wc = wc_ref[...]
    row8 = lax.broadcasted_iota(I32, (SUBLANES, w), 0)
    xc = bc_ref[...]
    for j in range(CONV_WIDTH):
        shift = CONV_WIDTH - 1 - j
        if shift == 0:
            xs = x
        else:
            rolled = pltpu.roll(x, shift, 0)
            head = jnp.where(row8 < shift, pltpu.roll(prev, shift, 0), rolled[:SUBLANES])
            xs = jnp.concatenate([head, rolled[SUBLANES:]], axis=0)
        xc = xc + xs * wc[j:j + 1]
    xprev_ref[...] = x[t_tile - SUBLANES:]

    xcb = xc.astype(BF16)
    n_grp = w // MXU_DIM
    pre_r = jnp.concatenate(
        [jnp.dot(xcb[:, g * MXU_DIM:(g + 1) * MXU_DIM], wrg_ref[g], preferred_element_type=F32) for g in range(n_grp)],
        axis=1) + brg_ref[...]
    pre_i = jnp.concatenate(
        [jnp.dot(xcb[:, g * MXU_DIM:(g + 1) * MXU_DIM], wig_ref[g], preferred_element_type=F32) for g in range(n_grp)],
        axis=1) + big_ref[...]
    r = _sigmoid(pre_r)
    ig = _sigmoid(pre_i)
    neg_lam = -lam_ref[...]
    softplus = jnp.maximum(neg_lam, 0.0) + jnp.log1p(jnp.exp(-jnp.abs(neg_lam)))
    log_a = (-LRU_C) * r * softplus
    a = jnp.exp(log_a)
    th = jnp.tanh(log_a)
    num = -2.0 * th
    mult = jnp.where(num > 0.0, num * lax.rsqrt(num * (1.0 - th)), 0.0)
    u = mult * (ig * xc)

    def scan_steps(a, u, pos, length, axis):
        dist = 1
        while dist < length:
            keep = pos >= dist
            u = jnp.where(keep, a * pltpu.roll(u, dist, axis) + u, u)
            a = jnp.where(keep, a * pltpu.roll(a, dist, axis), a)
            dist *= 2
        return a, u

    n_grp_t = t_tile // SUBLANES
    grouped = (n_grp_t, SUBLANES, w)
    a, u = scan_steps(a.reshape(grouped), u.reshape(grouped), lax.broadcasted_iota(I32, grouped, 1), SUBLANES, 1)
    a, u = a.reshape(x.shape), u.reshape(x.shape)
    n_lt = w // LANES
    for c in range(n_lt):
        a_scr[c] = a[:, c * LANES:(c + 1) * LANES]
        u_scr[c] = u[:, c * LANES:(c + 1) * LANES]
    last = pl.ds(SUBLANES - 1, n_grp_t, stride=SUBLANES)
    grow = lax.broadcasted_iota(I32, (n_grp_t, w), 0)
    ag, ug = scan_steps(jnp.concatenate([a_scr[c, last, :] for c in range(n_lt)], axis=1),
                        jnp.concatenate([u_scr[c, last, :] for c in range(n_lt)], axis=1), grow, n_grp_t, 0)
    h_end = ag * hcar_ref[...] + ug
    h_in = jnp.where(grow == 0, hcar_ref[...], pltpu.roll(h_end, 1, 0))
    hcar_ref[...] = h_end[n_grp_t - 1:]
    for c in range(n_lt):
        for k in range(SUBLANES):
            a_scr[c, pl.ds(k, n_grp_t, stride=SUBLANES), :] = h_in[:, c * LANES:(c + 1) * LANES]
    h = a * jnp.concatenate([a_scr[c] for c in range(n_lt)], axis=1) + u
    o_ref[...] = (h * jax.nn.gelu(gr_ref[...])).astype(o_ref.dtype)


def _block_diag_groups(wblk):
    nb, bw, _ = wblk.shape
    per = MXU_DIM // bw
    g = nb // per
    eye = jnp.eye(per, dtype=wblk.dtype)
    w5 = wblk.reshape(g, per, bw, bw)
    return jnp.einsum("gawv,ab->gawbv", w5, eye).reshape(g, MXU_DIM, MXU_DIM).astype(BF16)


def _lru(xr, gr, w_conv, b_conv, w_rg, b_rg, w_ig, b_ig, lru_lambda, b, s):
    n, w = xr.shape
    t = T_LRU
    n_t = s // t
    row_spec = pl.BlockSpec((t, w), lambda bi, ti: (bi * n_t + ti, 0))
    vec = lambda r: pl.BlockSpec((r, w), lambda bi, ti: (0, 0))
    blk = pl.BlockSpec((w // MXU_DIM, MXU_DIM, MXU_DIM), lambda bi, ti: (0, 0, 0))
    return pl.pallas_call(
        functools.partial(_lru_kernel, t_tile=t),
        grid=(b, n_t),
        in_specs=[row_spec, row_spec, vec(CONV_WIDTH), vec(1), blk, vec(1), blk, vec(1), vec(1)],
        out_specs=row_spec,
        out_shape=jax.ShapeDtypeStruct((n, w), BF16),
        scratch_shapes=[pltpu.VMEM((SUBLANES, w), F32), pltpu.VMEM((1, w), F32),
                        pltpu.VMEM((w // LANES, t, LANES), F32), pltpu.VMEM((w // LANES, t, LANES), F32)],
        compiler_params=pltpu.CompilerParams(dimension_semantics=("arbitrary", "arbitrary")),
        name="lru",
    )(xr, gr, w_conv, b_conv.reshape(1, w), _block_diag_groups(w_rg), b_rg.reshape(1, w),
      _block_diag_groups(w_ig), b_ig.reshape(1, w), lru_lambda.reshape(1, w))


def _split_bf16(x):
    hi = x.astype(BF16)
    return hi, (x - hi.astype(F32)).astype(BF16)


def _to_token_tiles(ref, x, first_token=0):
    tm, d = x.shape
    rpt = d // LANES
    for c in range(rpt):
        ref[pl.ds(first_token * rpt + c, tm, stride=rpt), :] = x[:, c * LANES:(c + 1) * LANES]


def _from_token_tiles(ref, tm, rpt, first_token=0):
    return jnp.concatenate([ref[pl.ds(first_token * rpt + c, tm, stride=rpt), :] for c in range(rpt)], axis=1)


def _merge_kernel(o_ref, hg_ref, ga_ref, gb_ref, x_ref, wa_ref, wl_ref, wo_ref, gm_ref, wrh_ref, wrl_ref, br_ref,
                  x1_ref, h2_ref, lg_ref):
    attn_br = jnp.dot(o_ref[...], wa_ref[...], preferred_element_type=F32)
    lru_br = jnp.dot(hg_ref[...], wl_ref[...], preferred_element_type=F32)
    mixed = _sigmoid(ga_ref[...]) * attn_br + _sigmoid(gb_ref[...]) * lru_br
    x1 = x_ref[...] + jnp.dot(mixed.astype(BF16), wo_ref[...], preferred_element_type=F32)
    x1_ref[...] = x1
    h2 = _rmsnorm(x1, gm_ref[...], NORM_EPS)
    _to_token_tiles(h2_ref, h2)
    hi, lo = _split_bf16(h2)
    wrh = wrh_ref[...]
    lg_ref[...] = (jnp.dot(hi, wrh, preferred_element_type=F32) + jnp.dot(lo, wrh, preferred_element_type=F32)
                   + jnp.dot(hi, wrl_ref[...], preferred_element_type=F32)) + br_ref[...]


def _merge(o, hg, ga, gb, x2, wa, wl, wo, g_moe, w_router, b_router):
    n, d = x2.shape
    tm = TM_MERGE
    rpt = d // LANES
    row = pl.BlockSpec((tm, d), lambda i: (i, 0))
    full = lambda r, c: pl.BlockSpec((r, c), lambda i: (0, 0))
    wr_hi, wr_lo = _split_bf16(w_router)
    return pl.pallas_call(
        _merge_kernel,
        grid=(n // tm,),
        in_specs=[row, row, row, row, row, full(d, d), full(d, d), full(d, d), full(1, d),
                  full(d, LANES), full(d, LANES), full(1, LANES)],
        out_specs=[row, pl.BlockSpec((tm * rpt, LANES), lambda i: (i, 0)), pl.BlockSpec((tm, LANES), lambda i: (i, 0))],
        out_shape=[jax.ShapeDtypeStruct((n, d), F32), jax.ShapeDtypeStruct((n * rpt, LANES), F32),
                   jax.ShapeDtypeStruct((n, LANES), F32)],
        compiler_params=pltpu.CompilerParams(dimension_semantics=("arbitrary",)),
        name="merge",
    )(o, hg, ga, gb, x2, wa, wl, wo, g_moe, wr_hi, wr_lo, b_router)


ROUTE_LOGIT_ROWS = 40


def _row_pick(x, rows, idx):
    return jnp.sum(jnp.where(rows == idx, x, jnp.zeros_like(x)), axis=0, keepdims=True)


def _route_kernel(lg_ref, pos_ref, wts_ref, tab_ref, *, n, t_tile, seg_tile):
    row = lax.broadcasted_iota(I32, (ROUTE_LOGIT_ROWS, t_tile), 0)
    erow = lax.broadcasted_iota(I32, (N_EXPERTS, t_tile), 0)
    prow = lax.broadcasted_iota(I32, (SUBLANES, t_tile), 0)
    wrow = lax.broadcasted_iota(I32, (LANES, t_tile), 0)
    neg_inf = jnp.float32(-jnp.inf)
    earlier = (lax.broadcasted_iota(I32, (t_tile, t_tile), 0) < lax.broadcasted_iota(I32, (t_tile, t_tile), 1)).astype(BF16)

    def phase1(t, cnt):
        start = pl.multiple_of(t * t_tile, t_tile)
        lg = lg_ref[pl.ds(start, t_tile), :].T[:ROUTE_LOGIT_ROWS]
        is_grp = row < N_GROUPS
        gl = jnp.where(is_grp, lg, neg_inf)
        gmax = jnp.max(gl, axis=0, keepdims=True)
        g_idx = jnp.min(jnp.where(gl == gmax, row, LANES), axis=0, keepdims=True)
        g_w = 1.0 / jnp.sum(jnp.where(is_grp, jnp.exp(lg - gmax), 0.0), axis=0, keepdims=True)
        lo = N_GROUPS + EXPERTS_PER_GROUP * g_idx
        in_grp = (row >= lo) & (row < lo + EXPERTS_PER_GROUP)
        fl = jnp.where(in_grp, lg, neg_inf)
        v1 = jnp.max(fl, axis=0, keepdims=True)
        i1 = jnp.min(jnp.where(in_grp & (fl == v1), row, LANES), axis=0, keepdims=True)
        rest = in_grp & (row != i1)
        fl2 = jnp.where(rest, lg, neg_inf)
        v2 = jnp.max(fl2, axis=0, keepdims=True)
        i2 = jnp.min(jnp.where(rest & (fl2 == v2), row, LANES), axis=0, keepdims=True)
        t2 = jnp.exp(v2 - v1)
        den = 1.0 + t2
        w1 = g_w * (1.0 / den)
        w2 = g_w * (t2 / den)
        e1 = i1 - N_GROUPS
        e2 = i2 - N_GROUPS
        onehot = ((erow == e1) | (erow == e2)).astype(F32)
        before = jnp.dot(onehot.astype(BF16), earlier, preferred_element_type=F32) + cnt
        rank1 = _row_pick(before, erow, e1).astype(I32)
        rank2 = _row_pick(before, erow, e2).astype(I32)
        pos_ref[:, pl.ds(start, t_tile)] = jnp.where(prow == 0, e1, jnp.where(prow == 1, e2, jnp.where(
            prow == 2, rank1, jnp.where(prow == 3, rank2, 0))))
        wts_ref[pl.ds(start, t_tile), :] = jnp.where(wrow == 0, w1, jnp.where(wrow == 1, w2, 0.0)).T
        return cnt + jnp.sum(onehot, axis=1, keepdims=True)

    cnt = lax.fori_loop(0, n // t_tile, phase1, jnp.zeros((N_EXPERTS, 1), F32))

    rows_sq = lax.broadcasted_iota(I32, (LANES, LANES), 0)
    lanes_sq = lax.broadcasted_iota(I32, (LANES, LANES), 1)
    cnt_sq = jnp.broadcast_to(jnp.concatenate([cnt, jnp.zeros((LANES - N_EXPERTS, 1), F32)], axis=0), (LANES, LANES))
    padded = jnp.ceil(cnt_sq * (1.0 / seg_tile)) * seg_tile
    incl = padded
    dist = 1
    while dist < LANES:
        incl = incl + jnp.where(rows_sq >= dist, pltpu.roll(incl, dist, 0), 0.0)
        dist *= 2
    offs = incl - padded
    tab_ref[...] = jnp.where(lanes_sq == 0, offs, jnp.where(lanes_sq == 1, cnt_sq, 0.0)).astype(I32)
    offs_col = offs[:N_EXPERTS, :1].astype(I32)

    def phase2(t, carry):
        cols = pl.ds(pl.multiple_of(t * t_tile, t_tile), t_tile)
        info = pos_ref[:, cols]
        offs_b = jnp.broadcast_to(offs_col, (N_EXPERTS, t_tile))
        p1 = info[2:3] + _row_pick(offs_b, erow, info[0:1])
        p2 = info[3:4] + _row_pick(offs_b, erow, info[1:2])
        pos_ref[:, cols] = jnp.where(prow == 0, p1, jnp.where(prow == 1, p2, 0))
        return carry

    lax.fori_loop(0, n // t_tile, phase2, 0)


def _route(logits):
    n = logits.shape[0]
    return pl.pallas_call(
        functools.partial(_route_kernel, n=n, t_tile=T_ROUTE, seg_tile=TM_EXPERT),
        out_shape=[jax.ShapeDtypeStruct((SUBLANES, n), I32), jax.ShapeDtypeStruct((n, LANES), F32),
                   jax.ShapeDtypeStruct((LANES, LANES), I32)],
        name="route",
    )(logits)


def _token_rows(idx, rpt):
    return pl.ds(pl.multiple_of(idx * rpt, rpt), rpt)


def _dispatch_kernel(seg_ref, cnt_ref, p1_ref, p2_ref, h_ref, hs_ref, zero_scr, sem, *, tm, rpt, n_tiles,
                     seg_tile, cap_tiles):
    def issue(r, carry):
        src = h_ref.at[_token_rows(r, rpt)]
        pltpu.make_async_copy(src, hs_ref.at[_token_rows(p1_ref[r], rpt)], sem).start(priority=0)
        pltpu.make_async_copy(src, hs_ref.at[_token_rows(p2_ref[r], rpt)], sem).start(priority=1)
        return carry

    lax.fori_loop(0, tm, issue, 0, unroll=DMA_ISSUE_UNROLL)
    whole = pltpu.make_async_copy(h_ref, hs_ref.at[pl.ds(0, tm * rpt)], sem)
    whole.wait()
    whole.wait()

    @pl.when(pl.program_id(0) == n_tiles - 1)
    def _():
        zero_scr[...] = jnp.zeros_like(zero_scr)
        zero_row = zero_scr.at[pl.ds(0, rpt)]

        def per_expert(e, carry):
            first_pad = seg_ref[e] + cnt_ref[e]
            n_pad = seg_ref[e + 1] - first_pad

            def fill(r, c):
                pltpu.make_async_copy(zero_row, hs_ref.at[_token_rows(first_pad + r, rpt)], sem).start()
                return c

            def drain(r, c):
                pltpu.make_async_copy(zero_row, hs_ref.at[_token_rows(first_pad + r, rpt)], sem).wait()
                return c

            lax.fori_loop(0, n_pad, fill, 0)
            lax.fori_loop(0, n_pad, drain, 0)
            return carry

        lax.fori_loop(0, N_EXPERTS, per_expert, 0)

        used_tiles = seg_ref[N_EXPERTS] // seg_tile

        def tail_tile(c):
            return hs_ref.at[pl.ds(pl.multiple_of((used_tiles + c) * (seg_tile * rpt), seg_tile * rpt), seg_tile * rpt)]

        def fill_tail(c, carry):
            pltpu.make_async_copy(zero_scr, tail_tile(c), sem).start()
            return carry

        def drain_tail(c, carry):
            pltpu.make_async_copy(zero_scr, tail_tile(c), sem).wait()
            return carry

        lax.fori_loop(0, cap_tiles - used_tiles, fill_tail, 0)
        lax.fori_loop(0, cap_tiles - used_tiles, drain_tail, 0)


def _dispatch(seg_start, counts, pos1, pos2, h2t, rpt):
    n = h2t.shape[0] // rpt
    tm = TM_DISPATCH
    n_tiles = n // tm
    idx = pl.BlockSpec((tm,), lambda i, *_: (i,), memory_space=pltpu.SMEM)
    grid_spec = pltpu.PrefetchScalarGridSpec(
        num_scalar_prefetch=2,
        grid=(n_tiles,),
        in_specs=[idx, idx, pl.BlockSpec((tm * rpt, LANES), lambda i, *_: (i, 0))],
        out_specs=pl.BlockSpec(memory_space=pl.ANY),
        scratch_shapes=[pltpu.VMEM((TM_EXPERT * rpt, LANES), F32), pltpu.SemaphoreType.DMA(())],
    )
    cap_tiles = 2 * n // TM_EXPERT + N_EXPERTS
    return pl.pallas_call(
        functools.partial(_dispatch_kernel, tm=tm, rpt=rpt, n_tiles=n_tiles, seg_tile=TM_EXPERT, cap_tiles=cap_tiles),
        grid_spec=grid_spec,
        out_shape=jax.ShapeDtypeStruct((cap_tiles * TM_EXPERT * rpt, LANES), F32),
        compiler_params=pltpu.CompilerParams(dimension_semantics=("arbitrary",), has_side_effects=True),
        name="dispatch",
    )(seg_start, counts, pos1, pos2, h2t)


def _expert_kernel(t_exp, t_valid, t_new, t_slot, t_next, hs_ref, wg_hbm, wu_hbm, wd_hbm, ys_ref,
                   wg_f32, wu_f32, wd_f32, wg_bf, wu_bf, wd_bf, sems, *, tm, rpt, tiles_per_step):
    def weight_copies(e, slot):
        return [pltpu.make_async_copy(src.at[e], dst.at[slot], sems.at[slot])
                for src, dst in ((wg_hbm, wg_f32), (wu_hbm, wu_f32), (wd_hbm, wd_f32))]

    def one_tile(t, first_token):
        @pl.when(t_valid[t] == 1)
        def _():
            @pl.when(t_new[t] == 1)
            def _():
                slot = t_slot[t]

                @pl.when(t == 0)
                def _():
                    for c in weight_copies(t_exp[t], slot):
                        c.start()

                for c in weight_copies(t_exp[t], slot):
                    c.wait()
                wg_bf[...] = wg_f32[slot].astype(BF16)
                wu_bf[...] = wu_f32[slot].astype(BF16)
                wd_bf[...] = wd_f32[slot].astype(BF16)

                @pl.when(t_next[t] >= 0)
                def _():
                    for c in weight_copies(t_next[t], 1 - slot):
                        c.start()

            h = _from_token_tiles(hs_ref, tm, rpt, first_token).astype(BF16)
            gate = jnp.dot(h, wg_bf[...], preferred_element_type=F32)
            up = jnp.dot(h, wu_bf[...], preferred_element_type=F32)
            hid = ((gate * _sigmoid(gate)) * up).astype(BF16)
            _to_token_tiles(ys_ref, jnp.dot(hid, wd_bf[...], preferred_element_type=F32), first_token)

        @pl.when(t_valid[t] == 0)
        def _():
            ys_ref[pl.ds(first_token * rpt, tm * rpt), :] = jnp.zeros((tm * rpt, LANES), F32)

    for k in range(tiles_per_step):
        one_tile(pl.program_id(0) * tiles_per_step + k, k * tm)


def _tile_table(seg_start, n_tiles, tm):
    total = seg_start[N_EXPERTS] // tm
    t = jnp.arange(n_tiles, dtype=I32)
    blk = jnp.minimum(t, total - 1)
    exp = jnp.sum(seg_start[None, 1:N_EXPERTS + 1] <= (blk * tm)[:, None], axis=1).astype(I32)
    valid = (t < total).astype(I32)
    new = (exp != jnp.concatenate([jnp.full((1,), -1, I32), exp[:-1]])).astype(I32)
    slot = (jnp.cumsum(new) - 1) % 2
    later = jnp.where(exp[None, :] > exp[:, None], exp[None, :], N_EXPERTS)
    nxt = jnp.min(later, axis=1)
    nxt = jnp.where(nxt < N_EXPERTS, nxt, -1).astype(I32)
    return exp, valid, new, slot.astype(I32), nxt


def _experts(hs, seg_start, w_e_gate, w_e_up, w_e_down, rpt):
    d, ff = w_e_gate.shape[-2:]
    tm = TM_EXPERT
    n_tiles = hs.shape[0] // (rpt * tm)
    table = _tile_table(seg_start, n_tiles, tm)
    hbm = pl.BlockSpec(memory_space=pl.ANY)
    per_step = EXPERT_TILES_PER_STEP
    assert n_tiles % per_step == 0
    rows = pl.BlockSpec((per_step * tm * rpt, LANES), lambda s, *_: (s, 0))
    grid_spec = pltpu.PrefetchScalarGridSpec(
        num_scalar_prefetch=len(table),
        grid=(n_tiles // per_step,),
        in_specs=[rows, hbm, hbm, hbm],
        out_specs=rows,
        scratch_shapes=[pltpu.VMEM((2, d, ff), F32), pltpu.VMEM((2, d, ff), F32), pltpu.VMEM((2, ff, d), F32),
                        pltpu.VMEM((d, ff), BF16), pltpu.VMEM((d, ff), BF16), pltpu.VMEM((ff, d), BF16),
                        pltpu.SemaphoreType.DMA((2,))],
    )
    return pl.pallas_call(
        functools.partial(_expert_kernel, tm=tm, rpt=rpt, tiles_per_step=per_step),
        grid_spec=grid_spec,
        out_shape=jax.ShapeDtypeStruct(hs.shape, F32),
        compiler_params=pltpu.CompilerParams(dimension_semantics=("arbitrary",)),
        name="experts",
    )(*table, hs, w_e_gate, w_e_up, w_e_down)


def _combine_kernel(p1_ref, p2_ref, p1n_ref, p2n_ref, wts_ref, x1_ref, p_ref, gp_ref, wpg_ref, wpp_ref, gf_ref,
                    ys_ref, out_ref, ybuf, sems, *, tm, rpt, n_tiles):
    i = pl.program_id(0)
    slot = i % 2

    def gather(a_ref, b_ref, dst):
        def issue(r, carry):
            rows = _token_rows(r, rpt)
            pltpu.make_async_copy(ys_ref.at[_token_rows(a_ref[r], rpt)], ybuf.at[dst, 0, rows],
                                  sems.at[dst]).start(priority=0)
            pltpu.make_async_copy(ys_ref.at[_token_rows(b_ref[r], rpt)], ybuf.at[dst, 1, rows],
                                  sems.at[dst]).start(priority=1)
            return carry

        lax.fori_loop(0, tm, issue, 0, unroll=DMA_ISSUE_UNROLL)

    @pl.when(i == 0)
    def _():
        gather(p1_ref, p2_ref, 0)

    @pl.when(i + 1 < n_tiles)
    def _():
        gather(p1n_ref, p2n_ref, 1 - slot)

    for half in range(2):
        pltpu.make_async_copy(ys_ref.at[pl.ds(0, tm * rpt)], ybuf.at[slot, half], sems.at[slot]).wait()

    wts = wts_ref[...]
    y1 = _from_token_tiles(ybuf.at[slot, 0], tm, rpt)
    y2 = _from_token_tiles(ybuf.at[slot, 1], tm, rpt)
    x2 = x1_ref[...] + (wts[:, 0:1] * y1 + wts[:, 1:2] * y2)
    hp = _rmsnorm(x2, gp_ref[...], NORM_EPS).astype(BF16)
    gate = _sigmoid(jnp.dot(hp, wpg_ref[...], preferred_element_type=F32))
    proj = jnp.dot(p_ref[...].astype(BF16), wpp_ref[...], preferred_element_type=F32)
    x3 = x2 + gate * proj
    out_ref[...] = _rmsnorm(x3, gf_ref[...], NORM_EPS)


def _combine(pos1, pos2, wts, x1, p2d, g_ple, wpg, wpp, g_final, ys, rpt):
    n, d = x1.shape
    pd = p2d.shape[1]
    tm = TM_COMBINE
    n_tiles = n // tm
    idx = pl.BlockSpec((tm,), lambda i: (i,), memory_space=pltpu.SMEM)
    idx_next = pl.BlockSpec((tm,), lambda i: (jnp.minimum(i + 1, n_tiles - 1),), memory_space=pltpu.SMEM)
    row = pl.BlockSpec((tm, d), lambda i: (i, 0))
    full = lambda r, c: pl.BlockSpec((r, c), lambda i: (0, 0))
    return pl.pallas_call(
        functools.partial(_combine_kernel, tm=tm, rpt=rpt, n_tiles=n_tiles),
        grid=(n_tiles,),
        in_specs=[idx, idx, idx_next, idx_next, pl.BlockSpec((tm, LANES), lambda i: (i, 0)), row,
                  pl.BlockSpec((tm, pd), lambda i: (i, 0)), full(1, d), full(d, d), full(pd, d), full(1, d),
                  pl.BlockSpec(memory_space=pl.ANY)],
        out_specs=row,
        out_shape=jax.ShapeDtypeStruct((n, d), F32),
        scratch_shapes=[pltpu.VMEM((2, 2, tm * rpt, LANES), F32), pltpu.SemaphoreType.DMA((2,))],
        compiler_params=pltpu.CompilerParams(dimension_semantics=("arbitrary",)),
        name="combine",
    )(pos1, pos2, pos1, pos2, wts, x1, p2d, g_ple, wpg, wpp, g_final, ys)


def _layer(i, x2, p2d, b, s, g_mix, w_in, lam_q1, lam_k1, lam_q2, lam_k2, g_subln, w_conv, b_conv, w_rg, b_rg,
           w_ig, b_ig, lru_lambda, w_attn_br, w_lru_br, w_out, g_moe, w_rt_group, b_rt_group, w_rt_expert,
           b_rt_expert, w_e_gate, w_e_up, w_e_down, g_ple, w_ple_gate, w_ple_proj):
    n, d = x2.shape
    lam_init = 0.8 - 0.6 * math.exp(-0.3 * i)
    row = lambda a: a.reshape(1, -1)

    q, k, v, xr, gr, ga, gb = _inproj(x2, row(g_mix), w_in.astype(BF16))
    o = _attention(q.reshape(b, s, d), k.reshape(b, s, d), v.reshape(b, s, d),
                   row(lam_q1), row(lam_k1), row(lam_q2), row(lam_k2), row(g_subln), lam_init).reshape(n, d)
    hg = _lru(xr, gr, w_conv, b_conv, w_rg, b_rg, w_ig, b_ig, lru_lambda, b, s)

    pad = LANES - N_GROUPS - N_EXPERTS
    w_router = jnp.concatenate(
        [w_rt_group, w_rt_expert.transpose(1, 0, 2).reshape(d, N_EXPERTS), jnp.zeros((d, pad), F32)], axis=1)
    b_router = jnp.concatenate([b_rt_group, b_rt_expert.reshape(N_EXPERTS), jnp.zeros((pad,), F32)]).reshape(1, LANES)
    x1, h2, logits = _merge(o, hg, ga, gb, x2, w_attn_br.astype(BF16), w_lru_br.astype(BF16), w_out.astype(BF16),
                            row(g_moe), w_router, b_router)

    pos, wts, tab = _route(logits)
    pos1, pos2 = pos[0], pos[1]
    rpt = d // LANES
    seg_start, counts = tab[:N_EXPERTS + 1, 0], tab[:N_EXPERTS, 1]
    hs = _dispatch(seg_start, counts, pos1, pos2, h2, rpt)
    ys = _experts(hs, seg_start, w_e_gate, w_e_up, w_e_down, rpt)
    return pos1, pos2, wts, x1, ys


def kernel(x, p, g_mix, w_in, lam_q1, lam_k1, lam_q2, lam_k2, g_subln, w_conv, b_conv, w_rg, b_rg, w_ig, b_ig, lru_lambda, w_attn_br, w_lru_br, w_out, g_moe, w_rt_group, b_rt_group, w_rt_expert, b_rt_expert, w_e_gate, w_e_up, w_e_down, g_ple, w_ple_gate, w_ple_proj, g_final):
    b, s, d = x.shape
    depth = p.shape[0]
    assert depth == 1, "the final RMSNorm is fused into the last layer's combine step; one layer supported"
    n = b * s
    x2 = x.reshape(n, d)
    i = 0
    p2d = p[i].reshape(n, -1)
    pos1, pos2, wts, x1, ys = _layer(
        i, x2, p2d, b, s, g_mix[i], w_in[i], lam_q1[i], lam_k1[i], lam_q2[i], lam_k2[i], g_subln[i], w_conv[i],
        b_conv[i], w_rg[i], b_rg[i], w_ig[i], b_ig[i], lru_lambda[i], w_attn_br[i], w_lru_br[i], w_out[i],
        g_moe[i], w_rt_group[i], b_rt_group[i], w_rt_expert[i], b_rt_expert[i], w_e_gate[i], w_e_up[i],
        w_e_down[i], g_ple[i], w_ple_gate[i], w_ple_proj[i])
    out = _combine(pos1, pos2, wts, x1, p2d, g_ple[i].reshape(1, d), w_ple_gate[i].astype(BF16),
                   w_ple_proj[i].astype(BF16), g_final.reshape(1, d), ys, d // LANES)
    return out.reshape(b, s, d)
```

```python
import functools
import math

import jax
import jax.numpy as jnp
from jax import lax
from jax.experimental import pallas as pl
from jax.experimental.pallas import tpu as pltpu

F32 = jnp.float32
BF16 = jnp.bfloat16
I32 = jnp.int32

N_HEADS = 8
HEAD_DIM = 64
V_DIM = 2 * HEAD_DIM
SUBLN_EPS = 1e-5
NORM_EPS = 1e-6
LRU_BLOCKS = 16
CONV_WIDTH = 4
LRU_C = 8.0
N_GROUPS = 4
EXPERTS_PER_GROUP = 8
N_EXPERTS = N_GROUPS * EXPERTS_PER_GROUP
N_IN_PARTS = 7
LOG2_E = 1.4426950408889634
Q_SCALE = HEAD_DIM ** -0.5 * LOG2_E

LANES = 128
SUBLANES = 8
BF16_SUBLANES = 16
MXU_DIM = 256

TM_INPROJ = 512
T_ATTN = 256
T_LRU = 512
TM_MERGE = 512
T_ROUTE = 512
TM_DISPATCH = 1024
TM_COMBINE = 256
TM_EXPERT = 256
EXPERT_TILES_PER_STEP = 8
DMA_ISSUE_UNROLL = 8
MASK_NEG = -1e30


def _sigmoid(x):
    return 0.5 * jnp.tanh(0.5 * x) + 0.5


def _rmsnorm(x, g, eps):
    return (x * lax.rsqrt(jnp.mean(x * x, axis=-1, keepdims=True) + eps)) * g


def _inproj_kernel(x_ref, g_ref, w_ref, *out_refs, d):
    h = _rmsnorm(x_ref[...], g_ref[...], NORM_EPS).astype(BF16)
    for c, o_ref in enumerate(out_refs):
        z = jnp.dot(h, w_ref[:, c * d:(c + 1) * d], preferred_element_type=F32)
        if c == 0:
            z = z * Q_SCALE
        o_ref[...] = z.astype(o_ref.dtype)


def _inproj(x2, g_mix, w_in_bf):
    n, d = x2.shape
    tm = TM_INPROJ
    out_dtypes = (BF16, BF16, BF16, F32, F32, F32, F32)
    row_spec = pl.BlockSpec((tm, d), lambda i: (i, 0))
    return pl.pallas_call(
        functools.partial(_inproj_kernel, d=d),
        grid=(n // tm,),
        in_specs=[
            row_spec,
            pl.BlockSpec((1, d), lambda i: (0, 0)),
            pl.BlockSpec((d, N_IN_PARTS * d), lambda i: (0, 0), pipeline_mode=pl.Buffered(1)),
        ],
        out_specs=[row_spec] * N_IN_PARTS,
        out_shape=[jax.ShapeDtypeStruct((n, d), dt) for dt in out_dtypes],
        compiler_params=pltpu.CompilerParams(dimension_semantics=("arbitrary",)),
        name="inproj",
    )(x2, g_mix, w_in_bf)


V_AUG = V_DIM + BF16_SUBLANES
PIPE_GROUP = 4
PIPE_LAG = 2
PIPE_RING = 2 * PIPE_LAG
PIPE_SLOTS = PIPE_GROUP * PIPE_RING


def _attn_kernel(lq1_ref, lk1_ref, lq2_ref, lk2_ref, g_ref, q_ref, k_ref, v_ref, o_ref,
                 qt_scr, vt_scr, acc_scr, m_scr, bias_scr, s_scr, p_scr, cmax_scr, alpha_scr, *, t, lam_init):
    nq = q_ref.shape[0] // t
    w = 2 * t

    feat = lax.broadcasted_iota(I32, (t, V_DIM), 1)
    ones_pad = jnp.where(lax.broadcasted_iota(I32, (V_AUG - V_DIM, t), 0) == 0, 1.0, 0.0).astype(F32)

    def prep(i, carry):
        rows = pl.ds(pl.multiple_of(i * t, t), t)
        q = q_ref[rows, :].astype(F32)
        q1t = jnp.where(feat < HEAD_DIM, q, 0.0).T
        q2t = jnp.where(feat >= HEAD_DIM, q, 0.0).T
        qt_scr[i] = jnp.concatenate([q1t, q2t], axis=1).astype(BF16)
        vt = v_ref[rows, :].astype(F32).T
        vt_scr[i] = jnp.concatenate([vt, ones_pad], axis=0).astype(BF16)
        m_scr[i] = jnp.full((1, w), MASK_NEG, F32)
        acc_scr[i] = jnp.zeros((V_AUG, w), F32)
        return carry

    lax.fori_loop(0, nq, prep, 0, unroll=16)
    key = lax.broadcasted_iota(I32, (t, w), 0)
    qry = lax.broadcasted_iota(I32, (t, w), 1) & (t - 1)
    bias_scr[...] = jnp.where(key <= qry, 0.0, MASK_NEG)

    def stage_a(j, i, slot, masked):
        kj = k_ref[pl.ds(pl.multiple_of(j * t, t), t), :]
        s = jnp.dot(kj, qt_scr[i], preferred_element_type=F32)
        if masked:
            s = s + bias_scr[...]
        s_scr[slot] = s
        cmax_scr[slot] = jnp.max(s, axis=0, keepdims=True)

    def stage_b(i, slot):
        m_old = m_scr[i]
        m_new = jnp.maximum(m_old, cmax_scr[slot])
        m_scr[i] = m_new
        p_scr[slot] = jnp.exp2(s_scr[slot] - m_new).astype(BF16)
        alpha_scr[slot] = jnp.exp2(m_old - m_new)

    def stage_c(j, i, slot):
        acc_scr[i] = alpha_scr[slot] * acc_scr[i] + jnp.dot(vt_scr[j], p_scr[slot], preferred_element_type=F32)

    def run(n_steps, n_masked, first, nxt):
        body_steps = PIPE_RING * PIPE_GROUP
        assert n_steps % body_steps == 0 and n_masked % body_steps == 0 and n_masked >= body_steps
        n_groups = n_steps // PIPE_GROUP
        n_masked_groups = n_masked // PIPE_GROUP

        def steps_of(st):
            out = []
            for _ in range(PIPE_GROUP):
                out.append(st)
                st = nxt(*st)
            return out, st

        def turn(t_mod, a=None, b=None, c=None, masked=False):
            base_a = (t_mod % PIPE_RING) * PIPE_GROUP
            base_b = ((t_mod - PIPE_LAG) % PIPE_RING) * PIPE_GROUP
            a_steps, following = steps_of(a) if a is not None else (None, None)
            b_steps = steps_of(b)[0] if b is not None else None
            c_steps = steps_of(c)[0] if c is not None else None
            for kk in range(PIPE_GROUP):
                if a_steps is not None:
                    stage_a(*a_steps[kk], base_a + kk, masked)
            for kk in range(PIPE_GROUP):
                if b_steps is not None:
                    stage_b(b_steps[kk][1], base_b + kk)
            for kk in range(PIPE_GROUP):
                if c_steps is not None:
                    stage_c(*c_steps[kk], base_a + kk)
            return following

        f = [first]
        for tt in range(PIPE_RING):
            f.append(turn(tt, a=f[tt], b=f[tt - PIPE_LAG] if tt >= PIPE_LAG else None, masked=True))

        def ring(_, f, masked):
            f = list(f)
            for r in range(PIPE_RING):
                f.append(turn(r, a=f[-1], b=f[-1 - PIPE_LAG], c=f[-1 - PIPE_RING], masked=masked))
            return tuple(f[-(PIPE_RING + 1):])

        f = lax.fori_loop(0, (n_masked_groups - PIPE_RING) // PIPE_RING, functools.partial(ring, masked=True), tuple(f))
        f = lax.fori_loop(0, (n_groups - n_masked_groups) // PIPE_RING, functools.partial(ring, masked=False), f)
        for r in range(PIPE_RING):
            turn(r, b=f[r + PIPE_LAG] if r < PIPE_LAG else None, c=f[r])

    def next_step(j, i):
        wrap = i + 1 >= nq
        return jnp.where(wrap, 0, j + 1), jnp.where(wrap, i - j + 1, i + 1)

    zero = jnp.int32(0)
    run(nq * (nq + 1) // 2, nq, (zero, zero), next_step)

    lam = (jnp.exp(jnp.sum(lq1_ref[...] * lk1_ref[...], axis=1, keepdims=True))
           - jnp.exp(jnp.sum(lq2_ref[...] * lk2_ref[...], axis=1, keepdims=True)) + lam_init)
    gain = g_ref[...]

    def finish(i, carry):
        acc = acc_scr[i]
        o1 = acc[:V_DIM, :t] / acc[V_DIM:V_DIM + 1, :t]
        o2 = acc[:V_DIM, t:] / acc[V_DIM:V_DIM + 1, t:]
        o = o1 - lam * o2
        y = ((o * lax.rsqrt(jnp.mean(o * o, axis=0, keepdims=True) + SUBLN_EPS)) * gain) * (1.0 - lam_init)
        o_ref[pl.ds(pl.multiple_of(i * t, t), t), :] = y.T.astype(o_ref.dtype)
        return carry

    lax.fori_loop(0, nq, finish, 0, unroll=16)


def _attention(q, k, v, lam_q1, lam_k1, lam_q2, lam_k2, g_subln, lam_init):
    b, s, d = q.shape
    t = T_ATTN
    assert s % t == 0 and d == N_HEADS * V_DIM
    nq = s // t
    small = lambda w: pl.BlockSpec((1, w), lambda bi, hi: (0, 0))
    head = pl.BlockSpec((None, s, V_DIM), lambda bi, hi: (bi, 0, hi))
    return pl.pallas_call(
        functools.partial(_attn_kernel, t=t, lam_init=lam_init),
        grid=(b, N_HEADS),
        in_specs=[small(HEAD_DIM), small(HEAD_DIM), small(HEAD_DIM), small(HEAD_DIM),
                  pl.BlockSpec((V_DIM, 1), lambda bi, hi: (0, 0)), head, head, head],
        out_specs=head,
        out_shape=jax.ShapeDtypeStruct((b, s, d), BF16),
        scratch_shapes=[
            pltpu.VMEM((nq, V_DIM, 2 * t), BF16),
            pltpu.VMEM((nq, V_AUG, t), BF16),
            pltpu.VMEM((nq, V_AUG, 2 * t), F32),
            pltpu.VMEM((nq, 1, 2 * t), F32),
            pltpu.VMEM((t, 2 * t), F32),
            pltpu.VMEM((PIPE_SLOTS, t, 2 * t), F32),
            pltpu.VMEM((PIPE_SLOTS, t, 2 * t), BF16),
            pltpu.VMEM((PIPE_SLOTS, 1, 2 * t), F32),
            pltpu.VMEM((PIPE_SLOTS, 1, 2 * t), F32),
        ],
        compiler_params=pltpu.CompilerParams(dimension_semantics=("arbitrary", "arbitrary")),
        name="attn",
    )(lam_q1, lam_k1, lam_q2, lam_k2, g_subln.reshape(V_DIM, 1), q, k, v)


def _lru_kernel(xr_ref, gr_ref, wc_ref, bc_ref, wrg_ref, brg_ref, wig_ref, big_ref, lam_ref, o_ref,
                xprev_ref, hcar_ref, a_scr, u_scr, *, t_tile):
    @pl.when(pl.program_id(1) == 0)
    def _():
        xprev_ref[...] = jnp.zeros_like(xprev_ref)
        hcar_ref[...] = jnp.zeros_like(hcar_ref)

    x = xr_ref[...]
    w = x.shape[1]
    prev = xprev_ref[...]
    wc = wc_ref[...]
    row8 = lax.broadcasted_iota(I32, (SUBLANES, w), 0)
    xc = bc_ref[...]
    for j in range(CONV_WIDTH):
        shift = CONV_WIDTH - 1 - j
        if shift == 0:
            xs = x
        else:
            rolled = pltpu.roll(x, shift, 0)
            head = jnp.where(row8 < shift, pltpu.roll(prev, shift, 0), rolled[:SUBLANES])
            xs = jnp.concatenate([head, rolled[SUBLANES:]], axis=0)
        xc = xc + xs * wc[j:j + 1]
    xprev_ref[...] = x[t_tile - SUBLANES:]

    xcb = xc.astype(BF16)
    n_grp = w // MXU_DIM
    pre_r = jnp.concatenate(
        [jnp.dot(xcb[:, g * MXU_DIM:(g + 1) * MXU_DIM], wrg_ref[g], preferred_element_type=F32) for g in range(n_grp)],
        axis=1) + brg_ref[...]
    pre_i = jnp.concatenate(
        [jnp.dot(xcb[:, g * MXU_DIM:(g + 1) * MXU_DIM], wig_ref[g], preferred_element_type=F32) for g in range(n_grp)],
        axis=1) + big_ref[...]
    r = _sigmoid(pre_r)
    ig = _sigmoid(pre_i)
    neg_lam = -lam_ref[...]
    softplus = jnp.maximum(neg_lam, 0.0) + jnp.log1p(jnp.exp(-jnp.abs(neg_lam)))
    log_a = (-LRU_C) * r * softplus
    a = jnp.exp(log_a)
    th = jnp.tanh(log_a)
    num = -2.0 * th
    mult = jnp.where(num > 0.0, num * lax.rsqrt(num * (1.0 - th)), 0.0)
    u = mult * (ig * xc)

    def scan_steps(a, u, pos, length, axis):
        dist = 1
        while dist < length:
            keep = pos >= dist
            u = jnp.where(keep, a * pltpu.roll(u, dist, axis) + u, u)
            a = jnp.where(keep, a * pltpu.roll(a, dist, axis), a)
            dist *= 2
        return a, u

    n_grp_t = t_tile // SUBLANES
    grouped = (n_grp_t, SUBLANES, w)
    a, u = scan_steps(a.reshape(grouped), u.reshape(grouped), lax.broadcasted_iota(I32, grouped, 1), SUBLANES, 1)
    a, u = a.reshape(x.shape), u.reshape(x.shape)
    n_lt = w // LANES
    for c in range(n_lt):
        a_scr[c] = a[:, c * LANES:(c + 1) * LANES]
        u_scr[c] = u[:, c * LANES:(c + 1) * LANES]
    last = pl.ds(SUBLANES - 1, n_grp_t, stride=SUBLANES)
    grow = lax.broadcasted_iota(I32, (n_grp_t, w), 0)
    ag, ug = scan_steps(jnp.concatenate([a_scr[c, last, :] for c in range(n_lt)], axis=1),
                        jnp.concatenate([u_scr[c, last, :] for c in range(n_lt)], axis=1), grow, n_grp_t, 0)
    h_end = ag * hcar_ref[...] + ug
    h_in = jnp.where(grow == 0, hcar_ref[...], pltpu.roll(h_end, 1, 0))
    hcar_ref[...] = h_end[n_grp_t - 1:]
    for c in range(n_lt):
        for k in range(SUBLANES):
            a_scr[c, pl.ds(k, n_grp_t, stride=SUBLANES), :] = h_in[:, c * LANES:(c + 1) * LANES]
    h = a * jnp.concatenate([a_scr[c] for c in range(n_lt)], axis=1) + u
    o_ref[...] = (h * jax.nn.gelu(gr_ref[...])).astype(o_ref.dtype)


def _block_diag_groups(wblk):
    nb, bw, _ = wblk.shape
    per = MXU_DIM // bw
    g = nb // per
    eye = jnp.eye(per, dtype=wblk.dtype)
    w5 = wblk.reshape(g, per, bw, bw)
    return jnp.einsum("gawv,ab->gawbv", w5, eye).reshape(g, MXU_DIM, MXU_DIM).astype(BF16)


def _lru(xr, gr, w_conv, b_conv, w_rg, b_rg, w_ig, b_ig, lru_lambda, b, s):
    n, w = xr.shape
    t = T_LRU
    n_t = s // t
    row_spec = pl.BlockSpec((t, w), lambda bi, ti: (bi * n_t + ti, 0))
    vec = lambda r: pl.BlockSpec((r, w), lambda bi, ti: (0, 0))
    blk = pl.BlockSpec((w // MXU_DIM, MXU_DIM, MXU_DIM), lambda bi, ti: (0, 0, 0))
    return pl.pallas_call(
        functools.partial(_lru_kernel, t_tile=t),
        grid=(b, n_t),
        in_specs=[row_spec, row_spec, vec(CONV_WIDTH), vec(1), blk, vec(1), blk, vec(1), vec(1)],
        out_specs=row_spec,
        out_shape=jax.ShapeDtypeStruct((n, w), BF16),
        scratch_shapes=[pltpu.VMEM((SUBLANES, w), F32), pltpu.VMEM((1, w), F32),
                        pltpu.VMEM((w // LANES, t, LANES), F32), pltpu.VMEM((w // LANES, t, LANES), F32)],
        compiler_params=pltpu.CompilerParams(dimension_semantics=("arbitrary", "arbitrary")),
        name="lru",
    )(xr, gr, w_conv, b_conv.reshape(1, w), _block_diag_groups(w_rg), b_rg.reshape(1, w),
      _block_diag_groups(w_ig), b_ig.reshape(1, w), lru_lambda.reshape(1, w))


def _split_bf16(x):
    hi = x.astype(BF16)
    return hi, (x - hi.astype(F32)).astype(BF16)


def _to_token_tiles(ref, x, first_token=0):
    tm, d = x.shape
    rpt = d // LANES
    for c in range(rpt):
        ref[pl.ds(first_token * rpt + c, tm, stride=rpt), :] = x[:, c * LANES:(c + 1) * LANES]


def _from_token_tiles(ref, tm, rpt, first_token=0):
    return jnp.concatenate([ref[pl.ds(first_token * rpt + c, tm, stride=rpt), :] for c in range(rpt)], axis=1)


def _merge_kernel(o_ref, hg_ref, ga_ref, gb_ref, x_ref, wa_ref, wl_ref, wo_ref, gm_ref, wrh_ref, wrl_ref, br_ref,
                  x1_ref, h2_ref, lg_ref):
    attn_br = jnp.dot(o_ref[...], wa_ref[...], preferred_element_type=F32)
    lru_br = jnp.dot(hg_ref[...], wl_ref[...], preferred_element_type=F32)
    mixed = _sigmoid(ga_ref[...]) * attn_br + _sigmoid(gb_ref[...]) * lru_br
    x1 = x_ref[...] + jnp.dot(mixed.astype(BF16), wo_ref[...], preferred_element_type=F32)
    x1_ref[...] = x1
    h2 = _rmsnorm(x1, gm_ref[...], NORM_EPS)
    _to_token_tiles(h2_ref, h2)
    hi, lo = _split_bf16(h2)
    wrh = wrh_ref[...]
    lg_ref[...] = (jnp.dot(hi, wrh, preferred_element_type=F32) + jnp.dot(lo, wrh, preferred_element_type=F32)
                   + jnp.dot(hi, wrl_ref[...], preferred_element_type=F32)) + br_ref[...]


def _merge(o, hg, ga, gb, x2, wa, wl, wo, g_moe, w_router, b_router):
    n, d = x2.shape
    tm = TM_MERGE
    rpt = d // LANES
    row = pl.BlockSpec((tm, d), lambda i: (i, 0))
    full = lambda r, c: pl.BlockSpec((r, c), lambda i: (0, 0))
    wr_hi, wr_lo = _split_bf16(w_router)
    return pl.pallas_call(
        _merge_kernel,
        grid=(n // tm,),
        in_specs=[row, row, row, row, row, full(d, d), full(d, d), full(d, d), full(1, d),
                  full(d, LANES), full(d, LANES), full(1, LANES)],
        out_specs=[row, pl.BlockSpec((tm * rpt, LANES), lambda i: (i, 0)), pl.BlockSpec((tm, LANES), lambda i: (i, 0))],
        out_shape=[jax.ShapeDtypeStruct((n, d), F32), jax.ShapeDtypeStruct((n * rpt, LANES), F32),
                   jax.ShapeDtypeStruct((n, LANES), F32)],
        compiler_params=pltpu.CompilerParams(dimension_semantics=("arbitrary",)),
        name="merge",
    )(o, hg, ga, gb, x2, wa, wl, wo, g_moe, wr_hi, wr_lo, b_router)


ROUTE_LOGIT_ROWS = 40


def _row_pick(x, rows, idx):
    return jnp.sum(jnp.where(rows == idx, x, jnp.zeros_like(x)), axis=0, keepdims=True)


def _route_kernel(lg_ref, pos_ref, wts_ref, tab_ref, *, n, t_tile, seg_tile):
    row = lax.broadcasted_iota(I32, (ROUTE_LOGIT_ROWS, t_tile), 0)
    erow = lax.broadcasted_iota(I32, (N_EXPERTS, t_tile), 0)
    prow = lax.broadcasted_iota(I32, (SUBLANES, t_tile), 0)
    wrow = lax.broadcasted_iota(I32, (LANES, t_tile), 0)
    neg_inf = jnp.float32(-jnp.inf)
    earlier = (lax.broadcasted_iota(I32, (t_tile, t_tile), 0) < lax.broadcasted_iota(I32, (t_tile, t_tile), 1)).astype(BF16)

    def phase1(t, cnt):
        start = pl.multiple_of(t * t_tile, t_tile)
        lg = lg_ref[pl.ds(start, t_tile), :].T[:ROUTE_LOGIT_ROWS]
        is_grp = row < N_GROUPS
        gl = jnp.where(is_grp, lg, neg_inf)
        gmax = jnp.max(gl, axis=0, keepdims=True)
        g_idx = jnp.min(jnp.where(gl == gmax, row, LANES), axis=0, keepdims=True)
        g_w = 1.0 / jnp.sum(jnp.where(is_grp, jnp.exp(lg - gmax), 0.0), axis=0, keepdims=True)
        lo = N_GROUPS + EXPERTS_PER_GROUP * g_idx
        in_grp = (row >= lo) & (row < lo + EXPERTS_PER_GROUP)
        fl = jnp.where(in_grp, lg, neg_inf)
        v1 = jnp.max(fl, axis=0, keepdims=True)
        i1 = jnp.min(jnp.where(in_grp & (fl == v1), row, LANES), axis=0, keepdims=True)
        rest = in_grp & (row != i1)
        fl2 = jnp.where(rest, lg, neg_inf)
        v2 = jnp.max(fl2, axis=0, keepdims=True)
        i2 = jnp.min(jnp.where(rest & (fl2 == v2), row, LANES), axis=0, keepdims=True)
        t2 = jnp.exp(v2 - v1)
        den = 1.0 + t2
        w1 = g_w * (1.0 / den)
        w2 = g_w * (t2 / den)
        e1 = i1 - N_GROUPS
        e2 = i2 - N_GROUPS
        onehot = ((erow == e1) | (erow == e2)).astype(F32)
        before = jnp.dot(onehot.astype(BF16), earlier, preferred_element_type=F32) + cnt
        rank1 = _row_pick(before, erow, e1).astype(I32)
        rank2 = _row_pick(before, erow, e2).astype(I32)
        pos_ref[:, pl.ds(start, t_tile)] = jnp.where(prow == 0, e1, jnp.where(prow == 1, e2, jnp.where(
            prow == 2, rank1, jnp.where(prow == 3, rank2, 0))))
        wts_ref[pl.ds(start, t_tile), :] = jnp.where(wrow == 0, w1, jnp.where(wrow == 1, w2, 0.0)).T
        return cnt + jnp.sum(onehot, axis=1, keepdims=True)

    cnt = lax.fori_loop(0, n // t_tile, phase1, jnp.zeros((N_EXPERTS, 1), F32))

    rows_sq = lax.broadcasted_iota(I32, (LANES, LANES), 0)
    lanes_sq = lax.broadcasted_iota(I32, (LANES, LANES), 1)
    cnt_sq = jnp.broadcast_to(jnp.concatenate([cnt, jnp.zeros((LANES - N_EXPERTS, 1), F32)], axis=0), (LANES, LANES))
    padded = jnp.ceil(cnt_sq * (1.0 / seg_tile)) * seg_tile
    incl = padded
    dist = 1
    while dist < LANES:
        incl = incl + jnp.where(rows_sq >= dist, pltpu.roll(incl, dist, 0), 0.0)
        dist *= 2
    offs = incl - padded
    tab_ref[...] = jnp.where(lanes_sq == 0, offs, jnp.where(lanes_sq == 1, cnt_sq, 0.0)).astype(I32)
    offs_col = offs[:N_EXPERTS, :1].astype(I32)

    def phase2(t, carry):
        cols = pl.ds(pl.multiple_of(t * t_tile, t_tile), t_tile)
        info = pos_ref[:, cols]
        offs_b = jnp.broadcast_to(offs_col, (N_EXPERTS, t_tile))
        p1 = info[2:3] + _row_pick(offs_b, erow, info[0:1])
        p2 = info[3:4] + _row_pick(offs_b, erow, info[1:2])
        pos_ref[:, cols] = jnp.where(prow == 0, p1, jnp.where(prow == 1, p2, 0))
        return carry

    lax.fori_loop(0, n // t_tile, phase2, 0)


def _route(logits):
    n = logits.shape[0]
    return pl.pallas_call(
        functools.partial(_route_kernel, n=n, t_tile=T_ROUTE, seg_tile=TM_EXPERT),
        out_shape=[jax.ShapeDtypeStruct((SUBLANES, n), I32), jax.ShapeDtypeStruct((n, LANES), F32),
                   jax.ShapeDtypeStruct((LANES, LANES), I32)],
        name="route",
    )(logits)


def _token_rows(idx, rpt):
    return pl.ds(pl.multiple_of(idx * rpt, rpt), rpt)


def _dispatch_kernel(seg_ref, cnt_ref, p1_ref, p2_ref, h_ref, hs_ref, zero_scr, sem, *, tm, rpt, n_tiles,
                     seg_tile, cap_tiles):
    def issue(r, carry):
        src = h_ref.at[_token_rows(r, rpt)]
        pltpu.make_async_copy(src, hs_ref.at[_token_rows(p1_ref[r], rpt)], sem).start(priority=0)
        pltpu.make_async_copy(src, hs_ref.at[_token_rows(p2_ref[r], rpt)], sem).start(priority=1)
        return carry

    lax.fori_loop(0, tm, issue, 0, unroll=DMA_ISSUE_UNROLL)
    whole = pltpu.make_async_copy(h_ref, hs_ref.at[pl.ds(0, tm * rpt)], sem)
    whole.wait()
    whole.wait()

    @pl.when(pl.program_id(0) == n_tiles - 1)
    def _():
        zero_scr[...] = jnp.zeros_like(zero_scr)
        zero_row = zero_scr.at[pl.ds(0, rpt)]

        def per_expert(e, carry):
            first_pad = seg_ref[e] + cnt_ref[e]
            n_pad = seg_ref[e + 1] - first_pad

            def fill(r, c):
                pltpu.make_async_copy(zero_row, hs_ref.at[_token_rows(first_pad + r, rpt)], sem).start()
                return c

            def drain(r, c):
                pltpu.make_async_copy(zero_row, hs_ref.at[_token_rows(first_pad + r, rpt)], sem).wait()
                return c

            lax.fori_loop(0, n_pad, fill, 0)
            lax.fori_loop(0, n_pad, drain, 0)
            return carry

        lax.fori_loop(0, N_EXPERTS, per_expert, 0)

        used_tiles = seg_ref[N_EXPERTS] // seg_tile

        def tail_tile(c):
            return hs_ref.at[pl.ds(pl.multiple_of((used_tiles + c) * (seg_tile * rpt), seg_tile * rpt), seg_tile * rpt)]

        def fill_tail(c, carry):
            pltpu.make_async_copy(zero_scr, tail_tile(c), sem).start()
            return carry

        def drain_tail(c, carry):
            pltpu.make_async_copy(zero_scr, tail_tile(c), sem).wait()
            return carry

        lax.fori_loop(0, cap_tiles - used_tiles, fill_tail, 0)
        lax.fori_loop(0, cap_tiles - used_tiles, drain_tail, 0)


def _dispatch(seg_start, counts, pos1, pos2, h2t, rpt):
    n = h2t.shape[0] // rpt
    tm = TM_DISPATCH
    n_tiles = n // tm
    idx = pl.BlockSpec((tm,), lambda i, *_: (i,), memory_space=pltpu.SMEM)
    grid_spec = pltpu.PrefetchScalarGridSpec(
        num_scalar_prefetch=2,
        grid=(n_tiles,),
        in_specs=[idx, idx, pl.BlockSpec((tm * rpt, LANES), lambda i, *_: (i, 0))],
        out_specs=pl.BlockSpec(memory_space=pl.ANY),
        scratch_shapes=[pltpu.VMEM((TM_EXPERT * rpt, LANES), F32), pltpu.SemaphoreType.DMA(())],
    )
    cap_tiles = 2 * n // TM_EXPERT + N_EXPERTS
    return pl.pallas_call(
        functools.partial(_dispatch_kernel, tm=tm, rpt=rpt, n_tiles=n_tiles, seg_tile=TM_EXPERT, cap_tiles=cap_tiles),
        grid_spec=grid_spec,
        out_shape=jax.ShapeDtypeStruct((cap_tiles * TM_EXPERT * rpt, LANES), F32),
        compiler_params=pltpu.CompilerParams(dimension_semantics=("arbitrary",), has_side_effects=True),
        name="dispatch",
    )(seg_start, counts, pos1, pos2, h2t)


def _expert_kernel(t_exp, t_valid, t_new, t_slot, t_next, hs_ref, wg_hbm, wu_hbm, wd_hbm, ys_ref,
                   wg_f32, wu_f32, wd_f32, wg_bf, wu_bf, wd_bf, sems, *, tm, rpt, tiles_per_step):
    def weight_copies(e, slot):
        return [pltpu.make_async_copy(src.at[e], dst.at[slot], sems.at[slot])
                for src, dst in ((wg_hbm, wg_f32), (wu_hbm, wu_f32), (wd_hbm, wd_f32))]

    def one_tile(t, first_token):
        @pl.when(t_valid[t] == 1)
        def _():
            @pl.when(t_new[t] == 1)
            def _():
                slot = t_slot[t]

                @pl.when(t == 0)
                def _():
                    for c in weight_copies(t_exp[t], slot):
                        c.start()

                for c in weight_copies(t_exp[t], slot):
                    c.wait()
                wg_bf[...] = wg_f32[slot].astype(BF16)
                wu_bf[...] = wu_f32[slot].astype(BF16)
                wd_bf[...] = wd_f32[slot].astype(BF16)

                @pl.when(t_next[t] >= 0)
                def _():
                    for c in weight_copies(t_next[t], 1 - slot):
                        c.start()

            h = _from_token_tiles(hs_ref, tm, rpt, first_token).astype(BF16)
            gate = jnp.dot(h, wg_bf[...], preferred_element_type=F32)
            up = jnp.dot(h, wu_bf[...], preferred_element_type=F32)
            hid = ((gate * _sigmoid(gate)) * up).astype(BF16)
            _to_token_tiles(ys_ref, jnp.dot(hid, wd_bf[...], preferred_element_type=F32), first_token)

        @pl.when(t_valid[t] == 0)
        def _():
            ys_ref[pl.ds(first_token * rpt, tm * rpt), :] = jnp.zeros((tm * rpt, LANES), F32)

    for k in range(tiles_per_step):
        one_tile(pl.program_id(0) * tiles_per_step + k, k * tm)


def _tile_table(seg_start, n_tiles, tm):
    total = seg_start[N_EXPERTS] // tm
    t = jnp.arange(n_tiles, dtype=I32)
    blk = jnp.minimum(t, total - 1)
    exp = jnp.sum(seg_start[None, 1:N_EXPERTS + 1] <= (blk * tm)[:, None], axis=1).astype(I32)
    valid = (t < total).astype(I32)
    new = (exp != jnp.concatenate([jnp.full((1,), -1, I32), exp[:-1]])).astype(I32)
    slot = (jnp.cumsum(new) - 1) % 2
    later = jnp.where(exp[None, :] > exp[:, None], exp[None, :], N_EXPERTS)
    nxt = jnp.min(later, axis=1)
    nxt = jnp.where(nxt < N_EXPERTS, nxt, -1).astype(I32)
    return exp, valid, new, slot.astype(I32), nxt


def _experts(hs, seg_start, w_e_gate, w_e_up, w_e_down, rpt):
    d, ff = w_e_gate.shape[-2:]
    tm = TM_EXPERT
    n_tiles = hs.shape[0] // (rpt * tm)
    table = _tile_table(seg_start, n_tiles, tm)
    hbm = pl.BlockSpec(memory_space=pl.ANY)
    per_step = EXPERT_TILES_PER_STEP
    assert n_tiles % per_step == 0
    rows = pl.BlockSpec((per_step * tm * rpt, LANES), lambda s, *_: (s, 0))
    grid_spec = pltpu.PrefetchScalarGridSpec(
        num_scalar_prefetch=len(table),
        grid=(n_tiles // per_step,),
        in_specs=[rows, hbm, hbm, hbm],
        out_specs=rows,
        scratch_shapes=[pltpu.VMEM((2, d, ff), F32), pltpu.VMEM((2, d, ff), F32), pltpu.VMEM((2, ff, d), F32),
                        pltpu.VMEM((d, ff), BF16), pltpu.VMEM((d, ff), BF16), pltpu.VMEM((ff, d), BF16),
                        pltpu.SemaphoreType.DMA((2,))],
    )
    return pl.pallas_call(
        functools.partial(_expert_kernel, tm=tm, rpt=rpt, tiles_per_step=per_step),
        grid_spec=grid_spec,
        out_shape=jax.ShapeDtypeStruct(hs.shape, F32),
        compiler_params=pltpu.CompilerParams(dimension_semantics=("arbitrary",)),
        name="experts",
    )(*table, hs, w_e_gate, w_e_up, w_e_down)


def _combine_kernel(p1_ref, p2_ref, p1n_ref, p2n_ref, wts_ref, x1_ref, p_ref, gp_ref, wpg_ref, wpp_ref, gf_ref,
                    ys_ref, out_ref, ybuf, sems, *, tm, rpt, n_tiles):
    i = pl.program_id(0)
    slot = i % 2

    def gather(a_ref, b_ref, dst):
        def issue(r, carry):
            rows = _token_rows(r, rpt)
            pltpu.make_async_copy(ys_ref.at[_token_rows(a_ref[r], rpt)], ybuf.at[dst, 0, rows],
                                  sems.at[dst]).start(priority=0)
            pltpu.make_async_copy(ys_ref.at[_token_rows(b_ref[r], rpt)], ybuf.at[dst, 1, rows],
                                  sems.at[dst]).start(priority=1)
            return carry

        lax.fori_loop(0, tm, issue, 0, unroll=DMA_ISSUE_UNROLL)

    @pl.when(i == 0)
    def _():
        gather(p1_ref, p2_ref, 0)

    @pl.when(i + 1 < n_tiles)
    def _():
        gather(p1n_ref, p2n_ref, 1 - slot)

    for half in range(2):
        pltpu.make_async_copy(ys_ref.at[pl.ds(0, tm * rpt)], ybuf.at[slot, half], sems.at[slot]).wait()

    wts = wts_ref[...]
    y1 = _from_token_tiles(ybuf.at[slot, 0], tm, rpt)
    y2 = _from_token_tiles(ybuf.at[slot, 1], tm, rpt)
    x2 = x1_ref[...] + (wts[:, 0:1] * y1 + wts[:, 1:2] * y2)
    hp = _rmsnorm(x2, gp_ref[...], NORM_EPS).astype(BF16)
    gate = _sigmoid(jnp.dot(hp, wpg_ref[...], preferred_element_type=F32))
    proj = jnp.dot(p_ref[...].astype(BF16), wpp_ref[...], preferred_element_type=F32)
    x3 = x2 + gate * proj
    out_ref[...] = _rmsnorm(x3, gf_ref[...], NORM_EPS)


def _combine(pos1, pos2, wts, x1, p2d, g_ple, wpg, wpp, g_final, ys, rpt):
    n, d = x1.shape
    pd = p2d.shape[1]
    tm = TM_COMBINE
    n_tiles = n // tm
    idx = pl.BlockSpec((tm,), lambda i: (i,), memory_space=pltpu.SMEM)
    idx_next = pl.BlockSpec((tm,), lambda i: (jnp.minimum(i + 1, n_tiles - 1),), memory_space=pltpu.SMEM)
    row = pl.BlockSpec((tm, d), lambda i: (i, 0))
    full = lambda r, c: pl.BlockSpec((r, c), lambda i: (0, 0))
    return pl.pallas_call(
        functools.partial(_combine_kernel, tm=tm, rpt=rpt, n_tiles=n_tiles),
        grid=(n_tiles,),
        in_specs=[idx, idx, idx_next, idx_next, pl.BlockSpec((tm, LANES), lambda i: (i, 0)), row,
                  pl.BlockSpec((tm, pd), lambda i: (i, 0)), full(1, d), full(d, d), full(pd, d), full(1, d),
                  pl.BlockSpec(memory_space=pl.ANY)],
        out_specs=row,
        out_shape=jax.ShapeDtypeStruct((n, d), F32),
        scratch_shapes=[pltpu.VMEM((2, 2, tm * rpt, LANES), F32), pltpu.SemaphoreType.DMA((2,))],
        compiler_params=pltpu.CompilerParams(dimension_semantics=("arbitrary",)),
        name="combine",
    )(pos1, pos2, pos1, pos2, wts, x1, p2d, g_ple, wpg, wpp, g_final, ys)


def _layer(i, x2, p2d, b, s, g_mix, w_in, lam_q1, lam_k1, lam_q2, lam_k2, g_subln, w_conv, b_conv, w_rg, b_rg,
           w_ig, b_ig, lru_lambda, w_attn_br, w_lru_br, w_out, g_moe, w_rt_group, b_rt_group, w_rt_expert,
           b_rt_expert, w_e_gate, w_e_up, w_e_down, g_ple, w_ple_gate, w_ple_proj):
    n, d = x2.shape
    lam_init = 0.8 - 0.6 * math.exp(-0.3 * i)
    row = lambda a: a.reshape(1, -1)

    q, k, v, xr, gr, ga, gb = _inproj(x2, row(g_mix), w_in.astype(BF16))
    o = _attention(q.reshape(b, s, d), k.reshape(b, s, d), v.reshape(b, s, d),
                   row(lam_q1), row(lam_k1), row(lam_q2), row(lam_k2), row(g_subln), lam_init).reshape(n, d)
    hg = _lru(xr, gr, w_conv, b_conv, w_rg, b_rg, w_ig, b_ig, lru_lambda, b, s)

    pad = LANES - N_GROUPS - N_EXPERTS
    w_router = jnp.concatenate(
        [w_rt_group, w_rt_expert.transpose(1, 0, 2).reshape(d, N_EXPERTS), jnp.zeros((d, pad), F32)], axis=1)
    b_router = jnp.concatenate([b_rt_group, b_rt_expert.reshape(N_EXPERTS), jnp.zeros((pad,), F32)]).reshape(1, LANES)
    x1, h2, logits = _merge(o, hg, ga, gb, x2, w_attn_br.astype(BF16), w_lru_br.astype(BF16), w_out.astype(BF16),
                            row(g_moe), w_router, b_router)

    pos, wts, tab = _route(logits)
    pos1, pos2 = pos[0], pos[1]
    rpt = d // LANES
    seg_start, counts = tab[:N_EXPERTS + 1, 0], tab[:N_EXPERTS, 1]
    hs = _dispatch(seg_start, counts, pos1, pos2, h2, rpt)
    ys = _experts(hs, seg_start, w_e_gate, w_e_up, w_e_down, rpt)
    return pos1, pos2, wts, x1, ys


def kernel(x, p, g_mix, w_in, lam_q1, lam_k1, lam_q2, lam_k2, g_subln, w_conv, b_conv, w_rg, b_rg, w_ig, b_ig, lru_lambda, w_attn_br, w_lru_br, w_out, g_moe, w_rt_group, b_rt_group, w_rt_expert, b_rt_expert, w_e_gate, w_e_up, w_e_down, g_ple, w_ple_gate, w_ple_proj, g_final):
    b, s, d = x.shape
    depth = p.shape[0]
    assert depth == 1, "the final RMSNorm is fused into the last layer's combine step; one layer supported"
    n = b * s
    x2 = x.reshape(n, d)
    i = 0
    p2d = p[i].reshape(n, -1)
    pos1, pos2, wts, x1, ys = _layer(
        i, x2, p2d, b, s, g_mix[i], w_in[i], lam_q1[i], lam_k1[i], lam_q2[i], lam_k2[i], g_subln[i], w_conv[i],
        b_conv[i], w_rg[i], b_rg[i], w_ig[i], b_ig[i], lru_lambda[i], w_attn_br[i], w_lru_br[i], w_out[i],
        g_moe[i], w_rt_group[i], b_rt_group[i], w_rt_expert[i], b_rt_expert[i], w_e_gate[i], w_e_up[i],
        w_e_down[i], g_ple[i], w_ple_gate[i], w_ple_proj[i])
    out = _combine(pos1, pos2, wts, x1, p2d, g_ple[i].reshape(1, d), w_ple_gate[i].astype(BF16),
                   w_ple_proj[i].astype(BF16), g_final.reshape(1, d), ys, d // LANES)
    return out.reshape(b, s, d)
```

```python
import functools
import math

import jax
import jax.numpy as jnp
from jax import lax
from jax.experimental import pallas as pl
from jax.experimental.pallas import tpu as pltpu

F32 = jnp.float32
BF16 = jnp.bfloat16
I32 = jnp.int32

N_HEADS = 8
HEAD_DIM = 64
V_DIM = 2 * HEAD_DIM
SUBLN_EPS = 1e-5
NORM_EPS = 1e-6
LRU_BLOCKS = 16
CONV_WIDTH = 4
LRU_C = 8.0
N_GROUPS = 4
EXPERTS_PER_GROUP = 8
N_EXPERTS = N_GROUPS * EXPERTS_PER_GROUP
N_IN_PARTS = 7
LOG2_E = 1.4426950408889634
Q_SCALE = HEAD_DIM ** -0.5 * LOG2_E

LANES = 128
SUBLANES = 8
BF16_SUBLANES = 16
MXU_DIM = 256

TM_INPROJ = 512
T_ATTN = 256
T_LRU = 512
TM_MERGE = 512
T_ROUTE = 512
TM_DISPATCH = 1024
TM_COMBINE = 256
TM_EXPERT = 256
EXPERT_TILES_PER_STEP = 4
DMA_ISSUE_UNROLL = 8
MASK_NEG = -1e30


def _sigmoid(x):
    return 0.5 * jnp.tanh(0.5 * x) + 0.5


def _rmsnorm(x, g, eps):
    return (x * lax.rsqrt(jnp.mean(x * x, axis=-1, keepdims=True) + eps)) * g


def _inproj_kernel(x_ref, g_ref, w_ref, *out_refs, d):
    h = _rmsnorm(x_ref[...], g_ref[...], NORM_EPS).astype(BF16)
    for c, o_ref in enumerate(out_refs):
        z = jnp.dot(h, w_ref[:, c * d:(c + 1) * d], preferred_element_type=F32)
        if c == 0:
            z = z * Q_SCALE
        o_ref[...] = z.astype(o_ref.dtype)


def _inproj(x2, g_mix, w_in_bf):
    n, d = x2.shape
    tm = TM_INPROJ
    out_dtypes = (BF16, BF16, BF16, F32, F32, F32, F32)
    row_spec = pl.BlockSpec((tm, d), lambda i: (i, 0))
    return pl.pallas_call(
        functools.partial(_inproj_kernel, d=d),
        grid=(n // tm,),
        in_specs=[
            row_spec,
            pl.BlockSpec((1, d), lambda i: (0, 0)),
            pl.BlockSpec((d, N_IN_PARTS * d), lambda i: (0, 0), pipeline_mode=pl.Buffered(1)),
        ],
        out_specs=[row_spec] * N_IN_PARTS,
        out_shape=[jax.ShapeDtypeStruct((n, d), dt) for dt in out_dtypes],
        compiler_params=pltpu.CompilerParams(dimension_semantics=("arbitrary",)),
        name="inproj",
    )(x2, g_mix, w_in_bf)


V_AUG = V_DIM + BF16_SUBLANES
PIPE_GROUP = 4
PIPE_LAG = 2
PIPE_RING = 2 * PIPE_LAG
PIPE_SLOTS = PIPE_GROUP * PIPE_RING


def _attn_kernel(lq1_ref, lk1_ref, lq2_ref, lk2_ref, g_ref, q_ref, k_ref, v_ref, o_ref,
                 qt_scr, vt_scr, acc_scr, m_scr, bias_scr, s_scr, p_scr, cmax_scr, alpha_scr, *, t, lam_init):
    nq = q_ref.shape[0] // t
    w = 2 * t

    feat = lax.broadcasted_iota(I32, (t, V_DIM), 1)
    ones_pad = jnp.where(lax.broadcasted_iota(I32, (V_AUG - V_DIM, t), 0) == 0, 1.0, 0.0).astype(F32)

    def prep(i, carry):
        rows = pl.ds(pl.multiple_of(i * t, t), t)
        q = q_ref[rows, :].astype(F32)
        q1t = jnp.where(feat < HEAD_DIM, q, 0.0).T
        q2t = jnp.where(feat >= HEAD_DIM, q, 0.0).T
        qt_scr[i] = jnp.concatenate([q1t, q2t], axis=1).astype(BF16)
        vt = v_ref[rows, :].astype(F32).T
        vt_scr[i] = jnp.concatenate([vt, ones_pad], axis=0).astype(BF16)
        m_scr[i] = jnp.full((1, w), MASK_NEG, F32)
        acc_scr[i] = jnp.zeros((V_AUG, w), F32)
        return carry

    lax.fori_loop(0, nq, prep, 0, unroll=16)
    key = lax.broadcasted_iota(I32, (t, w), 0)
    qry = lax.broadcasted_iota(I32, (t, w), 1) & (t - 1)
    bias_scr[...] = jnp.where(key <= qry, 0.0, MASK_NEG)

    def stage_a(j, i, slot, masked):
        kj = k_ref[pl.ds(pl.multiple_of(j * t, t), t), :]
        s = jnp.dot(kj, qt_scr[i], preferred_element_type=F32)
        if masked:
            s = s + bias_scr[...]
        s_scr[slot] = s
        cmax_scr[slot] = jnp.max(s, axis=0, keepdims=True)

    def stage_b(i, slot):
        m_old = m_scr[i]
        m_new = jnp.maximum(m_old, cmax_scr[slot])
        m_scr[i] = m_new
        p_scr[slot] = jnp.exp2(s_scr[slot] - m_new).astype(BF16)
        alpha_scr[slot] = jnp.exp2(m_old - m_new)

    def stage_c(j, i, slot):
        acc_scr[i] = alpha_scr[slot] * acc_scr[i] + jnp.dot(vt_scr[j], p_scr[slot], preferred_element_type=F32)

    def run(n_steps, n_masked, first, nxt):
        body_steps = PIPE_RING * PIPE_GROUP
        assert n_steps % body_steps == 0 and n_masked % body_steps == 0 and n_masked >= body_steps
        n_groups = n_steps // PIPE_GROUP
        n_masked_groups = n_masked // PIPE_GROUP

        def steps_of(st):
            out = []
            for _ in range(PIPE_GROUP):
                out.append(st)
                st = nxt(*st)
            return out, st

        def turn(t_mod, a=None, b=None, c=None, masked=False):
            base_a = (t_mod % PIPE_RING) * PIPE_GROUP
            base_b = ((t_mod - PIPE_LAG) % PIPE_RING) * PIPE_GROUP
            a_steps, following = steps_of(a) if a is not None else (None, None)
            b_steps = steps_of(b)[0] if b is not None else None
            c_steps = steps_of(c)[0] if c is not None else None
            for kk in range(PIPE_GROUP):
                if a_steps is not None:
                    stage_a(*a_steps[kk], base_a + kk, masked)
            for kk in range(PIPE_GROUP):
                if b_steps is not None:
                    stage_b(b_steps[kk][1], base_b + kk)
            for kk in range(PIPE_GROUP):
                if c_steps is not None:
                    stage_c(*c_steps[kk], base_a + kk)
            return following

        f = [first]
        for tt in range(PIPE_RING):
            f.append(turn(tt, a=f[tt], b=f[tt - PIPE_LAG] if tt >= PIPE_LAG else None, masked=True))

        def ring(_, f, masked):
            f = list(f)
            for r in range(PIPE_RING):
                f.append(turn(r, a=f[-1], b=f[-1 - PIPE_LAG], c=f[-1 - PIPE_RING], masked=masked))
            return tuple(f[-(PIPE_RING + 1):])

        f = lax.fori_loop(0, (n_masked_groups - PIPE_RING) // PIPE_RING, functools.partial(ring, masked=True), tuple(f))
        f = lax.fori_loop(0, (n_groups - n_masked_groups) // PIPE_RING, functools.partial(ring, masked=False), f)
        for r in range(PIPE_RING):
            turn(r, b=f[r + PIPE_LAG] if r < PIPE_LAG else None, c=f[r])

    def next_step(j, i):
        wrap = i + 1 >= nq
        return jnp.where(wrap, 0, j + 1), jnp.where(wrap, i - j + 1, i + 1)

    zero = jnp.int32(0)
    run(nq * (nq + 1) // 2, nq, (zero, zero), next_step)

    lam = (jnp.exp(jnp.sum(lq1_ref[...] * lk1_ref[...], axis=1, keepdims=True))
           - jnp.exp(jnp.sum(lq2_ref[...] * lk2_ref[...], axis=1, keepdims=True)) + lam_init)
    gain = g_ref[...]

    def finish(i, carry):
        acc = acc_scr[i]
        o1 = acc[:V_DIM, :t] / acc[V_DIM:V_DIM + 1, :t]
        o2 = acc[:V_DIM, t:] / acc[V_DIM:V_DIM + 1, t:]
        o = o1 - lam * o2
        y = ((o * lax.rsqrt(jnp.mean(o * o, axis=0, keepdims=True) + SUBLN_EPS)) * gain) * (1.0 - lam_init)
        o_ref[pl.ds(pl.multiple_of(i * t, t), t), :] = y.T.astype(o_ref.dtype)
        return carry

    lax.fori_loop(0, nq, finish, 0, unroll=16)


def _attention(q, k, v, lam_q1, lam_k1, lam_q2, lam_k2, g_subln, lam_init):
    b, s, d = q.shape
    t = T_ATTN
    assert s % t == 0 and d == N_HEADS * V_DIM
    nq = s // t
    small = lambda w: pl.BlockSpec((1, w), lambda bi, hi: (0, 0))
    head = pl.BlockSpec((None, s, V_DIM), lambda bi, hi: (bi, 0, hi))
    return pl.pallas_call(
        functools.partial(_attn_kernel, t=t, lam_init=lam_init),
        grid=(b, N_HEADS),
        in_specs=[small(HEAD_DIM), small(HEAD_DIM), small(HEAD_DIM), small(HEAD_DIM),
                  pl.BlockSpec((V_DIM, 1), lambda bi, hi: (0, 0)), head, head, head],
        out_specs=head,
        out_shape=jax.ShapeDtypeStruct((b, s, d), BF16),
        scratch_shapes=[
            pltpu.VMEM((nq, V_DIM, 2 * t), BF16),
            pltpu.VMEM((nq, V_AUG, t), BF16),
            pltpu.VMEM((nq, V_AUG, 2 * t), F32),
            pltpu.VMEM((nq, 1, 2 * t), F32),
            pltpu.VMEM((t, 2 * t), F32),
            pltpu.VMEM((PIPE_SLOTS, t, 2 * t), F32),
            pltpu.VMEM((PIPE_SLOTS, t, 2 * t), BF16),
            pltpu.VMEM((PIPE_SLOTS, 1, 2 * t), F32),
            pltpu.VMEM((PIPE_SLOTS, 1, 2 * t), F32),
        ],
        compiler_params=pltpu.CompilerParams(dimension_semantics=("arbitrary", "arbitrary")),
        name="attn",
    )(lam_q1, lam_k1, lam_q2, lam_k2, g_subln.reshape(V_DIM, 1), q, k, v)


def _lru_kernel(xr_ref, gr_ref, wc_ref, bc_ref, wrg_ref, brg_ref, wig_ref, big_ref, lam_ref, o_ref,
                xprev_ref, hcar_ref, a_scr, u_scr, *, t_tile):
    @pl.when(pl.program_id(1) == 0)
    def _():
        xprev_ref[...] = jnp.zeros_like(xprev_ref)
        hcar_ref[...] = jnp.zeros_like(hcar_ref)

    x = xr_ref[...]
    w = x.shape[1]
    prev = xprev_ref[...]
    wc = wc_ref[...]
    row8 = lax.broadcasted_iota(I32, (SUBLANES, w), 0)
    xc = bc_ref[...]
    for j in range(CONV_WIDTH):
        shift = CONV_WIDTH - 1 - j
        if shift == 0:
            xs = x
        else:
            rolled = pltpu.roll(x, shift, 0)
            head = jnp.where(row8 < shift, pltpu.roll(prev, shift, 0), rolled[:SUBLANES])
            xs = jnp.concatenate([head, rolled[SUBLANES:]], axis=0)
        xc = xc + xs * wc[j:j + 1]
    xprev_ref[...] = x[t_tile - SUBLANES:]

    xcb = xc.astype(BF16)
    n_grp = w // MXU_DIM
    pre_r = jnp.concatenate(
        [jnp.dot(xcb[:, g * MXU_DIM:(g + 1) * MXU_DIM], wrg_ref[g], preferred_element_type=F32) for g in range(n_grp)],
        axis=1) + brg_ref[...]
    pre_i = jnp.concatenate(
        [jnp.dot(xcb[:, g * MXU_DIM:(g + 1) * MXU_DIM], wig_ref[g], preferred_element_type=F32) for g in range(n_grp)],
        axis=1) + big_ref[...]
    r = _sigmoid(pre_r)
    ig = _sigmoid(pre_i)
    neg_lam = -lam_ref[...]
    softplus = jnp.maximum(neg_lam, 0.0) + jnp.log1p(jnp.exp(-jnp.abs(neg_lam)))
    log_a = (-LRU_C) * r * softplus
    a = jnp.exp(log_a)
    th = jnp.tanh(log_a)
    num = -2.0 * th
    mult = jnp.where(num > 0.0, num * lax.rsqrt(num * (1.0 - th)), 0.0)
    u = mult * (ig * xc)

    def scan_steps(a, u, pos, length, axis):
        dist = 1
        while dist < length:
            keep = pos >= dist
            u = jnp.where(keep, a * pltpu.roll(u, dist, axis) + u, u)
            a = jnp.where(keep, a * pltpu.roll(a, dist, axis), a)
            dist *= 2
        return a, u

    n_grp_t = t_tile // SUBLANES
    grouped = (n_grp_t, SUBLANES, w)
    a, u = scan_steps(a.reshape(grouped), u.reshape(grouped), lax.broadcasted_iota(I32, grouped, 1), SUBLANES, 1)
    a, u = a.reshape(x.shape), u.reshape(x.shape)
    n_lt = w // LANES
    for c in range(n_lt):
        a_scr[c] = a[:, c * LANES:(c + 1) * LANES]
        u_scr[c] = u[:, c * LANES:(c + 1) * LANES]
    last = pl.ds(SUBLANES - 1, n_grp_t, stride=SUBLANES)
    grow = lax.broadcasted_iota(I32, (n_grp_t, w), 0)
    ag, ug = scan_steps(jnp.concatenate([a_scr[c, last, :] for c in range(n_lt)], axis=1),
                        jnp.concatenate([u_scr[c, last, :] for c in range(n_lt)], axis=1), grow, n_grp_t, 0)
    h_end = ag * hcar_ref[...] + ug
    h_in = jnp.where(grow == 0, hcar_ref[...], pltpu.roll(h_end, 1, 0))
    hcar_ref[...] = h_end[n_grp_t - 1:]
    for c in range(n_lt):
        for k in range(SUBLANES):
            a_scr[c, pl.ds(k, n_grp_t, stride=SUBLANES), :] = h_in[:, c * LANES:(c + 1) * LANES]
    h = a * jnp.concatenate([a_scr[c] for c in range(n_lt)], axis=1) + u
    o_ref[...] = (h * jax.nn.gelu(gr_ref[...])).astype(o_ref.dtype)


def _block_diag_groups(wblk):
    nb, bw, _ = wblk.shape
    per = MXU_DIM // bw
    g = nb // per
    eye = jnp.eye(per, dtype=wblk.dtype)
    w5 = wblk.reshape(g, per, bw, bw)
    return jnp.einsum("gawv,ab->gawbv", w5, eye).reshape(g, MXU_DIM, MXU_DIM).astype(BF16)


def _lru(xr, gr, w_conv, b_conv, w_rg, b_rg, w_ig, b_ig, lru_lambda, b, s):
    n, w = xr.shape
    t = T_LRU
    n_t = s // t
    row_spec = pl.BlockSpec((t, w), lambda bi, ti: (bi * n_t + ti, 0))
    vec = lambda r: pl.BlockSpec((r, w), lambda bi, ti: (0, 0))
    blk = pl.BlockSpec((w // MXU_DIM, MXU_DIM, MXU_DIM), lambda bi, ti: (0, 0, 0))
    return pl.pallas_call(
        functools.partial(_lru_kernel, t_tile=t),
        grid=(b, n_t),
        in_specs=[row_spec, row_spec, vec(CONV_WIDTH), vec(1), blk, vec(1), blk, vec(1), vec(1)],
        out_specs=row_spec,
        out_shape=jax.ShapeDtypeStruct((n, w), BF16),
        scratch_shapes=[pltpu.VMEM((SUBLANES, w), F32), pltpu.VMEM((1, w), F32),
                        pltpu.VMEM((w // LANES, t, LANES), F32), pltpu.VMEM((w // LANES, t, LANES), F32)],
        compiler_params=pltpu.CompilerParams(dimension_semantics=("arbitrary", "arbitrary")),
        name="lru",
    )(xr, gr, w_conv, b_conv.reshape(1, w), _block_diag_groups(w_rg), b_rg.reshape(1, w),
      _block_diag_groups(w_ig), b_ig.reshape(1, w), lru_lambda.reshape(1, w))


def _split_bf16(x):
    hi = x.astype(BF16)
    return hi, (x - hi.astype(F32)).astype(BF16)


def _to_token_tiles(ref, x, first_token=0):
    tm, d = x.shape
    rpt = d // LANES
    for c in range(rpt):
        ref[pl.ds(first_token * rpt + c, tm, stride=rpt), :] = x[:, c * LANES:(c + 1) * LANES]


def _from_token_tiles(ref, tm, rpt, first_token=0):
    return jnp.concatenate([ref[pl.ds(first_token * rpt + c, tm, stride=rpt), :] for c in range(rpt)], axis=1)


def _merge_kernel(o_ref, hg_ref, ga_ref, gb_ref, x_ref, wa_ref, wl_ref, wo_ref, gm_ref, wrh_ref, wrl_ref, br_ref,
                  x1_ref, h2_ref, lg_ref):
    attn_br = jnp.dot(o_ref[...], wa_ref[...], preferred_element_type=F32)
    lru_br = jnp.dot(hg_ref[...], wl_ref[...], preferred_element_type=F32)
    mixed = _sigmoid(ga_ref[...]) * attn_br + _sigmoid(gb_ref[...]) * lru_br
    x1 = x_ref[...] + jnp.dot(mixed.astype(BF16), wo_ref[...], preferred_element_type=F32)
    x1_ref[...] = x1
    h2 = _rmsnorm(x1, gm_ref[...], NORM_EPS)
    _to_token_tiles(h2_ref, h2)
    hi, lo = _split_bf16(h2)
    wrh = wrh_ref[...]
    lg_ref[...] = (jnp.dot(hi, wrh, preferred_element_type=F32) + jnp.dot(lo, wrh, preferred_element_type=F32)
                   + jnp.dot(hi, wrl_ref[...], preferred_element_type=F32)) + br_ref[...]


def _merge(o, hg, ga, gb, x2, wa, wl, wo, g_moe, w_router, b_router):
    n, d = x2.shape
    tm = TM_MERGE
    rpt = d // LANES
    row = pl.BlockSpec((tm, d), lambda i: (i, 0))
    full = lambda r, c: pl.BlockSpec((r, c), lambda i: (0, 0))
    wr_hi, wr_lo = _split_bf16(w_router)
    return pl.pallas_call(
        _merge_kernel,
        grid=(n // tm,),
        in_specs=[row, row, row, row, row, full(d, d), full(d, d), full(d, d), full(1, d),
                  full(d, LANES), full(d, LANES), full(1, LANES)],
        out_specs=[row, pl.BlockSpec((tm * rpt, LANES), lambda i: (i, 0)), pl.BlockSpec((tm, LANES), lambda i: (i, 0))],
        out_shape=[jax.ShapeDtypeStruct((n, d), F32), jax.ShapeDtypeStruct((n * rpt, LANES), F32),
                   jax.ShapeDtypeStruct((n, LANES), F32)],
        compiler_params=pltpu.CompilerParams(dimension_semantics=("arbitrary",)),
        name="merge",
    )(o, hg, ga, gb, x2, wa, wl, wo, g_moe, wr_hi, wr_lo, b_router)


ROUTE_LOGIT_ROWS = 40


def _row_pick(x, rows, idx):
    return jnp.sum(jnp.where(rows == idx, x, jnp.zeros_like(x)), axis=0, keepdims=True)


def _route_kernel(lg_ref, pos_ref, wts_ref, tab_ref, *, n, t_tile, seg_tile):
    row = lax.broadcasted_iota(I32, (ROUTE_LOGIT_ROWS, t_tile), 0)
    erow = lax.broadcasted_iota(I32, (N_EXPERTS, t_tile), 0)
    prow = lax.broadcasted_iota(I32, (SUBLANES, t_tile), 0)
    wrow = lax.broadcasted_iota(I32, (LANES, t_tile), 0)
    neg_inf = jnp.float32(-jnp.inf)
    earlier = (lax.broadcasted_iota(I32, (t_tile, t_tile), 0) < lax.broadcasted_iota(I32, (t_tile, t_tile), 1)).astype(BF16)

    def phase1(t, cnt):
        start = pl.multiple_of(t * t_tile, t_tile)
        lg = lg_ref[pl.ds(start, t_tile), :].T[:ROUTE_LOGIT_ROWS]
        is_grp = row < N_GROUPS
        gl = jnp.where(is_grp, lg, neg_inf)
        gmax = jnp.max(gl, axis=0, keepdims=True)
        g_idx = jnp.min(jnp.where(gl == gmax, row, LANES), axis=0, keepdims=True)
        g_w = 1.0 / jnp.sum(jnp.where(is_grp, jnp.exp(lg - gmax), 0.0), axis=0, keepdims=True)
        lo = N_GROUPS + EXPERTS_PER_GROUP * g_idx
        in_grp = (row >= lo) & (row < lo + EXPERTS_PER_GROUP)
        fl = jnp.where(in_grp, lg, neg_inf)
        v1 = jnp.max(fl, axis=0, keepdims=True)
        i1 = jnp.min(jnp.where(in_grp & (fl == v1), row, LANES), axis=0, keepdims=True)
        rest = in_grp & (row != i1)
        fl2 = jnp.where(rest, lg, neg_inf)
        v2 = jnp.max(fl2, axis=0, keepdims=True)
        i2 = jnp.min(jnp.where(rest & (fl2 == v2), row, LANES), axis=0, keepdims=True)
        t2 = jnp.exp(v2 - v1)
        den = 1.0 + t2
        w1 = g_w * (1.0 / den)
        w2 = g_w * (t2 / den)
        e1 = i1 - N_GROUPS
        e2 = i2 - N_GROUPS
        onehot = ((erow == e1) | (erow == e2)).astype(F32)
        before = jnp.dot(onehot.astype(BF16), earlier, preferred_element_type=F32) + cnt
        rank1 = _row_pick(before, erow, e1).astype(I32)
        rank2 = _row_pick(before, erow, e2).astype(I32)
        pos_ref[:, pl.ds(start, t_tile)] = jnp.where(prow == 0, e1, jnp.where(prow == 1, e2, jnp.where(
            prow == 2, rank1, jnp.where(prow == 3, rank2, 0))))
        wts_ref[pl.ds(start, t_tile), :] = jnp.where(wrow == 0, w1, jnp.where(wrow == 1, w2, 0.0)).T
        return cnt + jnp.sum(onehot, axis=1, keepdims=True)

    cnt = lax.fori_loop(0, n // t_tile, phase1, jnp.zeros((N_EXPERTS, 1), F32))

    rows_sq = lax.broadcasted_iota(I32, (LANES, LANES), 0)
    lanes_sq = lax.broadcasted_iota(I32, (LANES, LANES), 1)
    cnt_sq = jnp.broadcast_to(jnp.concatenate([cnt, jnp.zeros((LANES - N_EXPERTS, 1), F32)], axis=0), (LANES, LANES))
    padded = jnp.ceil(cnt_sq * (1.0 / seg_tile)) * seg_tile
    incl = padded
    dist = 1
    while dist < LANES:
        incl = incl + jnp.where(rows_sq >= dist, pltpu.roll(incl, dist, 0), 0.0)
        dist *= 2
    offs = incl - padded
    tab_ref[...] = jnp.where(lanes_sq == 0, offs, jnp.where(lanes_sq == 1, cnt_sq, 0.0)).astype(I32)
    offs_col = offs[:N_EXPERTS, :1].astype(I32)

    def phase2(t, carry):
        cols = pl.ds(pl.multiple_of(t * t_tile, t_tile), t_tile)
        info = pos_ref[:, cols]
        offs_b = jnp.broadcast_to(offs_col, (N_EXPERTS, t_tile))
        p1 = info[2:3] + _row_pick(offs_b, erow, info[0:1])
        p2 = info[3:4] + _row_pick(offs_b, erow, info[1:2])
        pos_ref[:, cols] = jnp.where(prow == 0, p1, jnp.where(prow == 1, p2, 0))
        return carry

    lax.fori_loop(0, n // t_tile, phase2, 0)


def _route(logits):
    n = logits.shape[0]
    return pl.pallas_call(
        functools.partial(_route_kernel, n=n, t_tile=T_ROUTE, seg_tile=TM_EXPERT),
        out_shape=[jax.ShapeDtypeStruct((SUBLANES, n), I32), jax.ShapeDtypeStruct((n, LANES), F32),
                   jax.ShapeDtypeStruct((LANES, LANES), I32)],
        name="route",
    )(logits)


def _token_rows(idx, rpt):
    return pl.ds(pl.multiple_of(idx * rpt, rpt), rpt)


def _dispatch_kernel(seg_ref, cnt_ref, p1_ref, p2_ref, h_ref, hs_ref, zero_scr, sem, *, tm, rpt, n_tiles,
                     seg_tile, cap_tiles):
    def issue(r, carry):
        src = h_ref.at[_token_rows(r, rpt)]
        pltpu.make_async_copy(src, hs_ref.at[_token_rows(p1_ref[r], rpt)], sem).start(priority=0)
        pltpu.make_async_copy(src, hs_ref.at[_token_rows(p2_ref[r], rpt)], sem).start(priority=1)
        return carry

    lax.fori_loop(0, tm, issue, 0, unroll=DMA_ISSUE_UNROLL)
    whole = pltpu.make_async_copy(h_ref, hs_ref.at[pl.ds(0, tm * rpt)], sem)
    whole.wait()
    whole.wait()

    @pl.when(pl.program_id(0) == n_tiles - 1)
    def _():
        zero_scr[...] = jnp.zeros_like(zero_scr)
        zero_row = zero_scr.at[pl.ds(0, rpt)]

        def per_expert(e, carry):
            first_pad = seg_ref[e] + cnt_ref[e]
            n_pad = seg_ref[e + 1] - first_pad

            def fill(r, c):
                pltpu.make_async_copy(zero_row, hs_ref.at[_token_rows(first_pad + r, rpt)], sem).start()
                return c

            def drain(r, c):
                pltpu.make_async_copy(zero_row, hs_ref.at[_token_rows(first_pad + r, rpt)], sem).wait()
                return c

            lax.fori_loop(0, n_pad, fill, 0)
            lax.fori_loop(0, n_pad, drain, 0)
            return carry

        lax.fori_loop(0, N_EXPERTS, per_expert, 0)

        used_tiles = seg_ref[N_EXPERTS] // seg_tile

        def tail_tile(c):
            return hs_ref.at[pl.ds(pl.multiple_of((used_tiles + c) * (seg_tile * rpt), seg_tile * rpt), seg_tile * rpt)]

        def fill_tail(c, carry):
            pltpu.make_async_copy(zero_scr, tail_tile(c), sem).start()
            return carry

        def drain_tail(c, carry):
            pltpu.make_async_copy(zero_scr, tail_tile(c), sem).wait()
            return carry

        lax.fori_loop(0, cap_tiles - used_tiles, fill_tail, 0)
        lax.fori_loop(0, cap_tiles - used_tiles, drain_tail, 0)


def _dispatch(seg_start, counts, pos1, pos2, h2t, rpt):
    n = h2t.shape[0] // rpt
    tm = TM_DISPATCH
    n_tiles = n // tm
    idx = pl.BlockSpec((tm,), lambda i, *_: (i,), memory_space=pltpu.SMEM)
    grid_spec = pltpu.PrefetchScalarGridSpec(
        num_scalar_prefetch=2,
        grid=(n_tiles,),
        in_specs=[idx, idx, pl.BlockSpec((tm * rpt, LANES), lambda i, *_: (i, 0))],
        out_specs=pl.BlockSpec(memory_space=pl.ANY),
        scratch_shapes=[pltpu.VMEM((TM_EXPERT * rpt, LANES), F32), pltpu.SemaphoreType.DMA(())],
    )
    cap_tiles = 2 * n // TM_EXPERT + N_EXPERTS
    return pl.pallas_call(
        functools.partial(_dispatch_kernel, tm=tm, rpt=rpt, n_tiles=n_tiles, seg_tile=TM_EXPERT, cap_tiles=cap_tiles),
        grid_spec=grid_spec,
        out_shape=jax.ShapeDtypeStruct((cap_tiles * TM_EXPERT * rpt, LANES), F32),
        compiler_params=pltpu.CompilerParams(dimension_semantics=("arbitrary",), has_side_effects=True),
        name="dispatch",
    )(seg_start, counts, pos1, pos2, h2t)


def _expert_kernel(t_exp, t_valid, t_new, t_slot, t_next, hs_ref, wg_hbm, wu_hbm, wd_hbm, ys_ref,
                   wg_f32, wu_f32, wd_f32, wg_bf, wu_bf, wd_bf, sems, *, tm, rpt, tiles_per_step):
    def weight_copies(e, slot):
        return [pltpu.make_async_copy(src.at[e], dst.at[slot], sems.at[slot])
                for src, dst in ((wg_hbm, wg_f32), (wu_hbm, wu_f32), (wd_hbm, wd_f32))]

    def one_tile(t, first_token):
        @pl.when(t_valid[t] == 1)
        def _():
            @pl.when(t_new[t] == 1)
            def _():
                slot = t_slot[t]

                @pl.when(t == 0)
                def _():
                    for c in weight_copies(t_exp[t], slot):
                        c.start()

                for c in weight_copies(t_exp[t], slot):
                    c.wait()
                wg_bf[...] = wg_f32[slot].astype(BF16)
                wu_bf[...] = wu_f32[slot].astype(BF16)
                wd_bf[...] = wd_f32[slot].astype(BF16)

                @pl.when(t_next[t] >= 0)
                def _():
                    for c in weight_copies(t_next[t], 1 - slot):
                        c.start()

            h = _from_token_tiles(hs_ref, tm, rpt, first_token).astype(BF16)
            gate = jnp.dot(h, wg_bf[...], preferred_element_type=F32)
            up = jnp.dot(h, wu_bf[...], preferred_element_type=F32)
            hid = ((gate * _sigmoid(gate)) * up).astype(BF16)
            _to_token_tiles(ys_ref, jnp.dot(hid, wd_bf[...], preferred_element_type=F32), first_token)

        @pl.when(t_valid[t] == 0)
        def _():
            ys_ref[pl.ds(first_token * rpt, tm * rpt), :] = jnp.zeros((tm * rpt, LANES), F32)

    for k in range(tiles_per_step):
        one_tile(pl.program_id(0) * tiles_per_step + k, k * tm)


def _tile_table(seg_start, n_tiles, tm):
    total = seg_start[N_EXPERTS] // tm
    t = jnp.arange(n_tiles, dtype=I32)
    blk = jnp.minimum(t, total - 1)
    exp = jnp.sum(seg_start[None, 1:N_EXPERTS + 1] <= (blk * tm)[:, None], axis=1).astype(I32)
    valid = (t < total).astype(I32)
    new = (exp != jnp.concatenate([jnp.full((1,), -1, I32), exp[:-1]])).astype(I32)
    slot = (jnp.cumsum(new) - 1) % 2
    later = jnp.where(exp[None, :] > exp[:, None], exp[None, :], N_EXPERTS)
    nxt = jnp.min(later, axis=1)
    nxt = jnp.where(nxt < N_EXPERTS, nxt, -1).astype(I32)
    return exp, valid, new, slot.astype(I32), nxt


def _experts(hs, seg_start, w_e_gate, w_e_up, w_e_down, rpt):
    d, ff = w_e_gate.shape[-2:]
    tm = TM_EXPERT
    n_tiles = hs.shape[0] // (rpt * tm)
    table = _tile_table(seg_start, n_tiles, tm)
    hbm = pl.BlockSpec(memory_space=pl.ANY)
    per_step = EXPERT_TILES_PER_STEP
    assert n_tiles % per_step == 0
    rows = pl.BlockSpec((per_step * tm * rpt, LANES), lambda s, *_: (s, 0))
    grid_spec = pltpu.PrefetchScalarGridSpec(
        num_scalar_prefetch=len(table),
        grid=(n_tiles // per_step,),
        in_specs=[rows, hbm, hbm, hbm],
        out_specs=rows,
        scratch_shapes=[pltpu.VMEM((2, d, ff), F32), pltpu.VMEM((2, d, ff), F32), pltpu.VMEM((2, ff, d), F32),
                        pltpu.VMEM((d, ff), BF16), pltpu.VMEM((d, ff), BF16), pltpu.VMEM((ff, d), BF16),
                        pltpu.SemaphoreType.DMA((2,))],
    )
    return pl.pallas_call(
        functools.partial(_expert_kernel, tm=tm, rpt=rpt, tiles_per_step=per_step),
        grid_spec=grid_spec,
        out_shape=jax.ShapeDtypeStruct(hs.shape, F32),
        compiler_params=pltpu.CompilerParams(dimension_semantics=("arbitrary",)),
        name="experts",
    )(*table, hs, w_e_gate, w_e_up, w_e_down)


def _combine_kernel(p1_ref, p2_ref, p1n_ref, p2n_ref, wts_ref, x1_ref, p_ref, gp_ref, wpg_ref, wpp_ref, gf_ref,
                    ys_ref, out_ref, ybuf, sems, *, tm, rpt, n_tiles):
    i = pl.program_id(0)
    slot = i % 2

    def gather(a_ref, b_ref, dst):
        def issue(r, carry):
            rows = _token_rows(r, rpt)
            pltpu.make_async_copy(ys_ref.at[_token_rows(a_ref[r], rpt)], ybuf.at[dst, 0, rows],
                                  sems.at[dst]).start(priority=0)
            pltpu.make_async_copy(ys_ref.at[_token_rows(b_ref[r], rpt)], ybuf.at[dst, 1, rows],
                                  sems.at[dst]).start(priority=1)
            return carry

        lax.fori_loop(0, tm, issue, 0, unroll=DMA_ISSUE_UNROLL)

    @pl.when(i == 0)
    def _():
        gather(p1_ref, p2_ref, 0)

    @pl.when(i + 1 < n_tiles)
    def _():
        gather(p1n_ref, p2n_ref, 1 - slot)

    for half in range(2):
        pltpu.make_async_copy(ys_ref.at[pl.ds(0, tm * rpt)], ybuf.at[slot, half], sems.at[slot]).wait()

    wts = wts_ref[...]
    y1 = _from_token_tiles(ybuf.at[slot, 0], tm, rpt)
    y2 = _from_token_tiles(ybuf.at[slot, 1], tm, rpt)
    x2 = x1_ref[...] + (wts[:, 0:1] * y1 + wts[:, 1:2] * y2)
    hp = _rmsnorm(x2, gp_ref[...], NORM_EPS).astype(BF16)
    gate = _sigmoid(jnp.dot(hp, wpg_ref[...], preferred_element_type=F32))
    proj = jnp.dot(p_ref[...].astype(BF16), wpp_ref[...], preferred_element_type=F32)
    x3 = x2 + gate * proj
    out_ref[...] = _rmsnorm(x3, gf_ref[...], NORM_EPS)


def _combine(pos1, pos2, wts, x1, p2d, g_ple, wpg, wpp, g_final, ys, rpt):
    n, d = x1.shape
    pd = p2d.shape[1]
    tm = TM_COMBINE
    n_tiles = n // tm
    idx = pl.BlockSpec((tm,), lambda i: (i,), memory_space=pltpu.SMEM)
    idx_next = pl.BlockSpec((tm,), lambda i: (jnp.minimum(i + 1, n_tiles - 1),), memory_space=pltpu.SMEM)
    row = pl.BlockSpec((tm, d), lambda i: (i, 0))
    full = lambda r, c: pl.BlockSpec((r, c), lambda i: (0, 0))
    return pl.pallas_call(
        functools.partial(_combine_kernel, tm=tm, rpt=rpt, n_tiles=n_tiles),
        grid=(n_tiles,),
        in_specs=[idx, idx, idx_next, idx_next, pl.BlockSpec((tm, LANES), lambda i: (i, 0)), row,
                  pl.BlockSpec((tm, pd), lambda i: (i, 0)), full(1, d), full(d, d), full(pd, d), full(1, d),
                  pl.BlockSpec(memory_space=pl.ANY)],
        out_specs=row,
        out_shape=jax.ShapeDtypeStruct((n, d), F32),
        scratch_shapes=[pltpu.VMEM((2, 2, tm * rpt, LANES), F32), pltpu.SemaphoreType.DMA((2,))],
        compiler_params=pltpu.CompilerParams(dimension_semantics=("arbitrary",)),
        name="combine",
    )(pos1, pos2, pos1, pos2, wts, x1, p2d, g_ple, wpg, wpp, g_final, ys)


def _layer(i, x2, p2d, b, s, g_mix, w_in, lam_q1, lam_k1, lam_q2, lam_k2, g_subln, w_conv, b_conv, w_rg, b_rg,
           w_ig, b_ig, lru_lambda, w_attn_br, w_lru_br, w_out, g_moe, w_rt_group, b_rt_group, w_rt_expert,
           b_rt_expert, w_e_gate, w_e_up, w_e_down, g_ple, w_ple_gate, w_ple_proj):
    n, d = x2.shape
    lam_init = 0.8 - 0.6 * math.exp(-0.3 * i)
    row = lambda a: a.reshape(1, -1)

    q, k, v, xr, gr, ga, gb = _inproj(x2, row(g_mix), w_in.astype(BF16))
    o = _attention(q.reshape(b, s, d), k.reshape(b, s, d), v.reshape(b, s, d),
                   row(lam_q1), row(lam_k1), row(lam_q2), row(lam_k2), row(g_subln), lam_init).reshape(n, d)
    hg = _lru(xr, gr, w_conv, b_conv, w_rg, b_rg, w_ig, b_ig, lru_lambda, b, s)

    pad = LANES - N_GROUPS - N_EXPERTS
    w_router = jnp.concatenate(
        [w_rt_group, w_rt_expert.transpose(1, 0, 2).reshape(d, N_EXPERTS), jnp.zeros((d, pad), F32)], axis=1)
    b_router = jnp.concatenate([b_rt_group, b_rt_expert.reshape(N_EXPERTS), jnp.zeros((pad,), F32)]).reshape(1, LANES)
    x1, h2, logits = _merge(o, hg, ga, gb, x2, w_attn_br.astype(BF16), w_lru_br.astype(BF16), w_out.astype(BF16),
                            row(g_moe), w_router, b_router)

    pos, wts, tab = _route(logits)
    pos1, pos2 = pos[0], pos[1]
    rpt = d // LANES
    seg_start, counts = tab[:N_EXPERTS + 1, 0], tab[:N_EXPERTS, 1]
    hs = _dispatch(seg_start, counts, pos1, pos2, h2, rpt)
    ys = _experts(hs, seg_start, w_e_gate, w_e_up, w_e_down, rpt)
    return pos1, pos2, wts, x1, ys


def kernel(x, p, g_mix, w_in, lam_q1, lam_k1, lam_q2, lam_k2, g_subln, w_conv, b_conv, w_rg, b_rg, w_ig, b_ig, lru_lambda, w_attn_br, w_lru_br, w_out, g_moe, w_rt_group, b_rt_group, w_rt_expert, b_rt_expert, w_e_gate, w_e_up, w_e_down, g_ple, w_ple_gate, w_ple_proj, g_final):
    b, s, d = x.shape
    depth = p.shape[0]
    assert depth == 1, "the final RMSNorm is fused into the last layer's combine step; one layer supported"
    n = b * s
    x2 = x.reshape(n, d)
    i = 0
    p2d = p[i].reshape(n, -1)
    pos1, pos2, wts, x1, ys = _layer(
        i, x2, p2d, b, s, g_mix[i], w_in[i], lam_q1[i], lam_k1[i], lam_q2[i], lam_k2[i], g_subln[i], w_conv[i],
        b_conv[i], w_rg[i], b_rg[i], w_ig[i], b_ig[i], lru_lambda[i], w_attn_br[i], w_lru_br[i], w_out[i],
        g_moe[i], w_rt_group[i], b_rt_group[i], w_rt_expert[i], b_rt_expert[i], w_e_gate[i], w_e_up[i],
        w_e_down[i], g_ple[i], w_ple_gate[i], w_ple_proj[i])
    out = _combine(pos1, pos2, wts, x1, p2d, g_ple[i].reshape(1, d), w_ple_gate[i].astype(BF16),
                   w_ple_proj[i].astype(BF16), g_final.reshape(1, d), ys, d // LANES)
    return out.reshape(b, s, d)
```

```python
import functools
import math

import jax
import jax.numpy as jnp
from jax import lax
from jax.experimental import pallas as pl
from jax.experimental.pallas import tpu as pltpu

F32 = jnp.float32
BF16 = jnp.bfloat16
I32 = jnp.int32

N_HEADS = 8
HEAD_DIM = 64
V_DIM = 2 * HEAD_DIM
SUBLN_EPS = 1e-5
NORM_EPS = 1e-6
LRU_BLOCKS = 16
CONV_WIDTH = 4
LRU_C = 8.0
N_GROUPS = 4
EXPERTS_PER_GROUP = 8
N_EXPERTS = N_GROUPS * EXPERTS_PER_GROUP
N_IN_PARTS = 7
LOG2_E = 1.4426950408889634
Q_SCALE = HEAD_DIM ** -0.5 * LOG2_E

LANES = 128
SUBLANES = 8
BF16_SUBLANES = 16
MXU_DIM = 256

TM_INPROJ = 512
T_ATTN = 256
T_LRU = 512
TM_MERGE = 512
T_ROUTE = 1024
TM_DISPATCH = 2048
TM_COMBINE = 256
TM_EXPERT = 256
EXPERT_TILES_PER_STEP = 4
DMA_ISSUE_UNROLL = 8
MASK_NEG = -1e30


def _sigmoid(x):
    return 0.5 * jnp.tanh(0.5 * x) + 0.5


def _rmsnorm(x, g, eps):
    return (x * lax.rsqrt(jnp.mean(x * x, axis=-1, keepdims=True) + eps)) * g


def _inproj_kernel(x_ref, g_ref, w_ref, *out_refs, d):
    h = _rmsnorm(x_ref[...], g_ref[...], NORM_EPS).astype(BF16)
    for c, o_ref in enumerate(out_refs):
        z = jnp.dot(h, w_ref[:, c * d:(c + 1) * d], preferred_element_type=F32)
        if c == 0:
            z = z * Q_SCALE
        o_ref[...] = z.astype(o_ref.dtype)


def _inproj(x2, g_mix, w_in_bf):
    n, d = x2.shape
    tm = TM_INPROJ
    out_dtypes = (BF16, BF16, BF16, F32, F32, F32, F32)
    row_spec = pl.BlockSpec((tm, d), lambda i: (i, 0))
    return pl.pallas_call(
        functools.partial(_inproj_kernel, d=d),
        grid=(n // tm,),
        in_specs=[
            row_spec,
            pl.BlockSpec((1, d), lambda i: (0, 0)),
            pl.BlockSpec((d, N_IN_PARTS * d), lambda i: (0, 0), pipeline_mode=pl.Buffered(1)),
        ],
        out_specs=[row_spec] * N_IN_PARTS,
        out_shape=[jax.ShapeDtypeStruct((n, d), dt) for dt in out_dtypes],
        compiler_params=pltpu.CompilerParams(dimension_semantics=("arbitrary",)),
        name="inproj",
    )(x2, g_mix, w_in_bf)


V_AUG = V_DIM + BF16_SUBLANES
PIPE_GROUP = 4
PIPE_LAG = 2
PIPE_RING = 2 * PIPE_LAG
PIPE_SLOTS = PIPE_GROUP * PIPE_RING


def _attn_kernel(lq1_ref, lk1_ref, lq2_ref, lk2_ref, g_ref, q_ref, k_ref, v_ref, o_ref,
                 qt_scr, vt_scr, acc_scr, m_scr, bias_scr, s_scr, p_scr, cmax_scr, alpha_scr, *, t, lam_init):
    nq = q_ref.shape[0] // t
    w = 2 * t

    feat = lax.broadcasted_iota(I32, (t, V_DIM), 1)
    ones_pad = jnp.where(lax.broadcasted_iota(I32, (V_AUG - V_DIM, t), 0) == 0, 1.0, 0.0).astype(F32)

    def prep(i, carry):
        rows = pl.ds(pl.multiple_of(i * t, t), t)
        q = q_ref[rows, :].astype(F32)
        q1t = jnp.where(feat < HEAD_DIM, q, 0.0).T
        q2t = jnp.where(feat >= HEAD_DIM, q, 0.0).T
        qt_scr[i] = jnp.concatenate([q1t, q2t], axis=1).astype(BF16)
        vt = v_ref[rows, :].astype(F32).T
        vt_scr[i] = jnp.concatenate([vt, ones_pad], axis=0).astype(BF16)
        m_scr[i] = jnp.full((1, w), MASK_NEG, F32)
        acc_scr[i] = jnp.zeros((V_AUG, w), F32)
        return carry

    lax.fori_loop(0, nq, prep, 0, unroll=16)
    key = lax.broadcasted_iota(I32, (t, w), 0)
    qry = lax.broadcasted_iota(I32, (t, w), 1) & (t - 1)
    bias_scr[...] = jnp.where(key <= qry, 0.0, MASK_NEG)

    def stage_a(j, i, slot, masked):
        kj = k_ref[pl.ds(pl.multiple_of(j * t, t), t), :]
        s = jnp.dot(kj, qt_scr[i], preferred_element_type=F32)
        if masked:
            s = s + bias_scr[...]
        s_scr[slot] = s
        cmax_scr[slot] = jnp.max(s, axis=0, keepdims=True)

    def stage_b(i, slot):
        m_old = m_scr[i]
        m_new = jnp.maximum(m_old, cmax_scr[slot])
        m_scr[i] = m_new
        p_scr[slot] = jnp.exp2(s_scr[slot] - m_new).astype(BF16)
        alpha_scr[slot] = jnp.exp2(m_old - m_new)

    def stage_c(j, i, slot):
        acc_scr[i] = alpha_scr[slot] * acc_scr[i] + jnp.dot(vt_scr[j], p_scr[slot], preferred_element_type=F32)

    def run(n_steps, n_masked, first, nxt):
        body_steps = PIPE_RING * PIPE_GROUP
        assert n_steps % body_steps == 0 and n_masked % body_steps == 0 and n_masked >= body_steps
        n_groups = n_steps // PIPE_GROUP
        n_masked_groups = n_masked // PIPE_GROUP

        def steps_of(st):
            out = []
            for _ in range(PIPE_GROUP):
                out.append(st)
                st = nxt(*st)
            return out, st

        def turn(t_mod, a=None, b=None, c=None, masked=False):
            base_a = (t_mod % PIPE_RING) * PIPE_GROUP
            base_b = ((t_mod - PIPE_LAG) % PIPE_RING) * PIPE_GROUP
            a_steps, following = steps_of(a) if a is not None else (None, None)
            b_steps = steps_of(b)[0] if b is not None else None
            c_steps = steps_of(c)[0] if c is not None else None
            for kk in range(PIPE_GROUP):
                if a_steps is not None:
                    stage_a(*a_steps[kk], base_a + kk, masked)
            for kk in range(PIPE_GROUP):
                if b_steps is not None:
                    stage_b(b_steps[kk][1], base_b + kk)
            for kk in range(PIPE_GROUP):
                if c_steps is not None:
                    stage_c(*c_steps[kk], base_a + kk)
            return following

        f = [first]
        for tt in range(PIPE_RING):
            f.append(turn(tt, a=f[tt], b=f[tt - PIPE_LAG] if tt >= PIPE_LAG else None, masked=True))

        def ring(_, f, masked):
            f = list(f)
            for r in range(PIPE_RING):
                f.append(turn(r, a=f[-1], b=f[-1 - PIPE_LAG], c=f[-1 - PIPE_RING], masked=masked))
            return tuple(f[-(PIPE_RING + 1):])

        f = lax.fori_loop(0, (n_masked_groups - PIPE_RING) // PIPE_RING, functools.partial(ring, masked=True), tuple(f))
        f = lax.fori_loop(0, (n_groups - n_masked_groups) // PIPE_RING, functools.partial(ring, masked=False), f)
        for r in range(PIPE_RING):
            turn(r, b=f[r + PIPE_LAG] if r < PIPE_LAG else None, c=f[r])

    def next_step(j, i):
        wrap = i + 1 >= nq
        return jnp.where(wrap, 0, j + 1), jnp.where(wrap, i - j + 1, i + 1)

    zero = jnp.int32(0)
    run(nq * (nq + 1) // 2, nq, (zero, zero), next_step)

    lam = (jnp.exp(jnp.sum(lq1_ref[...] * lk1_ref[...], axis=1, keepdims=True))
           - jnp.exp(jnp.sum(lq2_ref[...] * lk2_ref[...], axis=1, keepdims=True)) + lam_init)
    gain = g_ref[...]

    def finish(i, carry):
        acc = acc_scr[i]
        o1 = acc[:V_DIM, :t] / acc[V_DIM:V_DIM + 1, :t]
        o2 = acc[:V_DIM, t:] / acc[V_DIM:V_DIM + 1, t:]
        o = o1 - lam * o2
        y = ((o * lax.rsqrt(jnp.mean(o * o, axis=0, keepdims=True) + SUBLN_EPS)) * gain) * (1.0 - lam_init)
        o_ref[pl.ds(pl.multiple_of(i * t, t), t), :] = y.T.astype(o_ref.dtype)
        return carry

    lax.fori_loop(0, nq, finish, 0, unroll=16)


def _attention(q, k, v, lam_q1, lam_k1, lam_q2, lam_k2, g_subln, lam_init):
    b, s, d = q.shape
    t = T_ATTN
    assert s % t == 0 and d == N_HEADS * V_DIM
    nq = s // t
    small = lambda w: pl.BlockSpec((1, w), lambda bi, hi: (0, 0))
    head = pl.BlockSpec((None, s, V_DIM), lambda bi, hi: (bi, 0, hi))
    return pl.pallas_call(
        functools.partial(_attn_kernel, t=t, lam_init=lam_init),
        grid=(b, N_HEADS),
        in_specs=[small(HEAD_DIM), small(HEAD_DIM), small(HEAD_DIM), small(HEAD_DIM),
                  pl.BlockSpec((V_DIM, 1), lambda bi, hi: (0, 0)), head, head, head],
        out_specs=head,
        out_shape=jax.ShapeDtypeStruct((b, s, d), BF16),
        scratch_shapes=[
            pltpu.VMEM((nq, V_DIM, 2 * t), BF16),
            pltpu.VMEM((nq, V_AUG, t), BF16),
            pltpu.VMEM((nq, V_AUG, 2 * t), F32),
            pltpu.VMEM((nq, 1, 2 * t), F32),
            pltpu.VMEM((t, 2 * t), F32),
            pltpu.VMEM((PIPE_SLOTS, t, 2 * t), F32),
            pltpu.VMEM((PIPE_SLOTS, t, 2 * t), BF16),
            pltpu.VMEM((PIPE_SLOTS, 1, 2 * t), F32),
            pltpu.VMEM((PIPE_SLOTS, 1, 2 * t), F32),
        ],
        compiler_params=pltpu.CompilerParams(dimension_semantics=("arbitrary", "arbitrary")),
        name="attn",
    )(lam_q1, lam_k1, lam_q2, lam_k2, g_subln.reshape(V_DIM, 1), q, k, v)


def _lru_kernel(xr_ref, gr_ref, wc_ref, bc_ref, wrg_ref, brg_ref, wig_ref, big_ref, lam_ref, o_ref,
                xprev_ref, hcar_ref, a_scr, u_scr, *, t_tile):
    @pl.when(pl.program_id(1) == 0)
    def _():
        xprev_ref[...] = jnp.zeros_like(xprev_ref)
        hcar_ref[...] = jnp.zeros_like(hcar_ref)

    x = xr_ref[...]
    w = x.shape[1]
    prev = xprev_ref[...]
    wc = wc_ref[...]
    row8 = lax.broadcasted_iota(I32, (SUBLANES, w), 0)
    xc = bc_ref[...]
    for j in range(CONV_WIDTH):
        shift = CONV_WIDTH - 1 - j
        if shift == 0:
            xs = x
        else:
            rolled = pltpu.roll(x, shift, 0)
            head = jnp.where(row8 < shift, pltpu.roll(prev, shift, 0), rolled[:SUBLANES])
            xs = jnp.concatenate([head, rolled[SUBLANES:]], axis=0)
        xc = xc + xs * wc[j:j + 1]
    xprev_ref[...] = x[t_tile - SUBLANES:]

    xcb = xc.astype(BF16)
    n_grp = w // MXU_DIM
    pre_r = jnp.concatenate(
        [jnp.dot(xcb[:, g * MXU_DIM:(g + 1) * MXU_DIM], wrg_ref[g], preferred_element_type=F32) for g in range(n_grp)],
        axis=1) + brg_ref[...]
    pre_i = jnp.concatenate(
        [jnp.dot(xcb[:, g * MXU_DIM:(g + 1) * MXU_DIM], wig_ref[g], preferred_element_type=F32) for g in range(n_grp)],
        axis=1) + big_ref[...]
    r = _sigmoid(pre_r)
    ig = _sigmoid(pre_i)
    neg_lam = -lam_ref[...]
    softplus = jnp.maximum(neg_lam, 0.0) + jnp.log1p(jnp.exp(-jnp.abs(neg_lam)))
    log_a = (-LRU_C) * r * softplus
    a = jnp.exp(log_a)
    th = jnp.tanh(log_a)
    num = -2.0 * th
    mult = jnp.where(num > 0.0, num * lax.rsqrt(num * (1.0 - th)), 0.0)
    u = mult * (ig * xc)

    def scan_steps(a, u, pos, length, axis):
        dist = 1
        while dist < length:
            keep = pos >= dist
            u = jnp.where(keep, a * pltpu.roll(u, dist, axis) + u, u)
            a = jnp.where(keep, a * pltpu.roll(a, dist, axis), a)
            dist *= 2
        return a, u

    n_grp_t = t_tile // SUBLANES
    grouped = (n_grp_t, SUBLANES, w)
    a, u = scan_steps(a.reshape(grouped), u.reshape(grouped), lax.broadcasted_iota(I32, grouped, 1), SUBLANES, 1)
    a, u = a.reshape(x.shape), u.reshape(x.shape)
    n_lt = w // LANES
    for c in range(n_lt):
        a_scr[c] = a[:, c * LANES:(c + 1) * LANES]
        u_scr[c] = u[:, c * LANES:(c + 1) * LANES]
    last = pl.ds(SUBLANES - 1, n_grp_t, stride=SUBLANES)
    grow = lax.broadcasted_iota(I32, (n_grp_t, w), 0)
    ag, ug = scan_steps(jnp.concatenate([a_scr[c, last, :] for c in range(n_lt)], axis=1),
                        jnp.concatenate([u_scr[c, last, :] for c in range(n_lt)], axis=1), grow, n_grp_t, 0)
    h_end = ag * hcar_ref[...] + ug
    h_in = jnp.where(grow == 0, hcar_ref[...], pltpu.roll(h_end, 1, 0))
    hcar_ref[...] = h_end[n_grp_t - 1:]
    for c in range(n_lt):
        for k in range(SUBLANES):
            a_scr[c, pl.ds(k, n_grp_t, stride=SUBLANES), :] = h_in[:, c * LANES:(c + 1) * LANES]
    h = a * jnp.concatenate([a_scr[c] for c in range(n_lt)], axis=1) + u
    o_ref[...] = (h * jax.nn.gelu(gr_ref[...])).astype(o_ref.dtype)


def _block_diag_groups(wblk):
    nb, bw, _ = wblk.shape
    per = MXU_DIM // bw
    g = nb // per
    eye = jnp.eye(per, dtype=wblk.dtype)
    w5 = wblk.reshape(g, per, bw, bw)
    return jnp.einsum("gawv,ab->gawbv", w5, eye).reshape(g, MXU_DIM, MXU_DIM).astype(BF16)


def _lru(xr, gr, w_conv, b_conv, w_rg, b_rg, w_ig, b_ig, lru_lambda, b, s):
    n, w = xr.shape
    t = T_LRU
    n_t = s // t
    row_spec = pl.BlockSpec((t, w), lambda bi, ti: (bi * n_t + ti, 0))
    vec = lambda r: pl.BlockSpec((r, w), lambda bi, ti: (0, 0))
    blk = pl.BlockSpec((w // MXU_DIM, MXU_DIM, MXU_DIM), lambda bi, ti: (0, 0, 0))
    return pl.pallas_call(
        functools.partial(_lru_kernel, t_tile=t),
        grid=(b, n_t),
        in_specs=[row_spec, row_spec, vec(CONV_WIDTH), vec(1), blk, vec(1), blk, vec(1), vec(1)],
        out_specs=row_spec,
        out_shape=jax.ShapeDtypeStruct((n, w), BF16),
        scratch_shapes=[pltpu.VMEM((SUBLANES, w), F32), pltpu.VMEM((1, w), F32),
                        pltpu.VMEM((w // LANES, t, LANES), F32), pltpu.VMEM((w // LANES, t, LANES), F32)],
        compiler_params=pltpu.CompilerParams(dimension_semantics=("arbitrary", "arbitrary")),
        name="lru",
    )(xr, gr, w_conv, b_conv.reshape(1, w), _block_diag_groups(w_rg), b_rg.reshape(1, w),
      _block_diag_groups(w_ig), b_ig.reshape(1, w), lru_lambda.reshape(1, w))


def _split_bf16(x):
    hi = x.astype(BF16)
    return hi, (x - hi.astype(F32)).astype(BF16)


def _to_token_tiles(ref, x, first_token=0):
    tm, d = x.shape
    rpt = d // LANES
    for c in range(rpt):
        ref[pl.ds(first_token * rpt + c, tm, stride=rpt), :] = x[:, c * LANES:(c + 1) * LANES]


def _from_token_tiles(ref, tm, rpt, first_token=0):
    return jnp.concatenate([ref[pl.ds(first_token * rpt + c, tm, stride=rpt), :] for c in range(rpt)], axis=1)


def _merge_kernel(o_ref, hg_ref, ga_ref, gb_ref, x_ref, wa_ref, wl_ref, wo_ref, gm_ref, wrh_ref, wrl_ref, br_ref,
                  x1_ref, h2_ref, lg_ref):
    attn_br = jnp.dot(o_ref[...], wa_ref[...], preferred_element_type=F32)
    lru_br = jnp.dot(hg_ref[...], wl_ref[...], preferred_element_type=F32)
    mixed = _sigmoid(ga_ref[...]) * attn_br + _sigmoid(gb_ref[...]) * lru_br
    x1 = x_ref[...] + jnp.dot(mixed.astype(BF16), wo_ref[...], preferred_element_type=F32)
    x1_ref[...] = x1
    h2 = _rmsnorm(x1, gm_ref[...], NORM_EPS)
    _to_token_tiles(h2_ref, h2)
    hi, lo = _split_bf16(h2)
    wrh = wrh_ref[...]
    lg_ref[...] = (jnp.dot(hi, wrh, preferred_element_type=F32) + jnp.dot(lo, wrh, preferred_element_type=F32)
                   + jnp.dot(hi, wrl_ref[...], preferred_element_type=F32)) + br_ref[...]


def _merge(o, hg, ga, gb, x2, wa, wl, wo, g_moe, w_router, b_router):
    n, d = x2.shape
    tm = TM_MERGE
    rpt = d // LANES
    row = pl.BlockSpec((tm, d), lambda i: (i, 0))
    full = lambda r, c: pl.BlockSpec((r, c), lambda i: (0, 0))
    wr_hi, wr_lo = _split_bf16(w_router)
    return pl.pallas_call(
        _merge_kernel,
        grid=(n // tm,),
        in_specs=[row, row, row, row, row, full(d, d), full(d, d), full(d, d), full(1, d),
                  full(d, LANES), full(d, LANES), full(1, LANES)],
        out_specs=[row, pl.BlockSpec((tm * rpt, LANES), lambda i: (i, 0)), pl.BlockSpec((tm, LANES), lambda i: (i, 0))],
        out_shape=[jax.ShapeDtypeStruct((n, d), F32), jax.ShapeDtypeStruct((n * rpt, LANES), F32),
                   jax.ShapeDtypeStruct((n, LANES), F32)],
        compiler_params=pltpu.CompilerParams(dimension_semantics=("arbitrary",)),
        name="merge",
    )(o, hg, ga, gb, x2, wa, wl, wo, g_moe, wr_hi, wr_lo, b_router)


ROUTE_LOGIT_ROWS = 40


def _row_pick(x, rows, idx):
    return jnp.sum(jnp.where(rows == idx, x, jnp.zeros_like(x)), axis=0, keepdims=True)


def _route_kernel(lg_ref, pos_ref, wts_ref, tab_ref, *, n, t_tile, seg_tile):
    row = lax.broadcasted_iota(I32, (ROUTE_LOGIT_ROWS, t_tile), 0)
    erow = lax.broadcasted_iota(I32, (N_EXPERTS, t_tile), 0)
    prow = lax.broadcasted_iota(I32, (SUBLANES, t_tile), 0)
    wrow = lax.broadcasted_iota(I32, (LANES, t_tile), 0)
    neg_inf = jnp.float32(-jnp.inf)
    earlier = (lax.broadcasted_iota(I32, (t_tile, t_tile), 0) < lax.broadcasted_iota(I32, (t_tile, t_tile), 1)).astype(BF16)

    def phase1(t, cnt):
        start = pl.multiple_of(t * t_tile, t_tile)
        lg = lg_ref[pl.ds(start, t_tile), :].T[:ROUTE_LOGIT_ROWS]
        is_grp = row < N_GROUPS
        gl = jnp.where(is_grp, lg, neg_inf)
        gmax = jnp.max(gl, axis=0, keepdims=True)
        g_idx = jnp.min(jnp.where(gl == gmax, row, LANES), axis=0, keepdims=True)
        g_w = 1.0 / jnp.sum(jnp.where(is_grp, jnp.exp(lg - gmax), 0.0), axis=0, keepdims=True)
        lo = N_GROUPS + EXPERTS_PER_GROUP * g_idx
        in_grp = (row >= lo) & (row < lo + EXPERTS_PER_GROUP)
        fl = jnp.where(in_grp, lg, neg_inf)
        v1 = jnp.max(fl, axis=0, keepdims=True)
        i1 = jnp.min(jnp.where(in_grp & (fl == v1), row, LANES), axis=0, keepdims=True)
        rest = in_grp & (row != i1)
        fl2 = jnp.where(rest, lg, neg_inf)
        v2 = jnp.max(fl2, axis=0, keepdims=True)
        i2 = jnp.min(jnp.where(rest & (fl2 == v2), row, LANES), axis=0, keepdims=True)
        t2 = jnp.exp(v2 - v1)
        den = 1.0 + t2
        w1 = g_w * (1.0 / den)
        w2 = g_w * (t2 / den)
        e1 = i1 - N_GROUPS
        e2 = i2 - N_GROUPS
        onehot = ((erow == e1) | (erow == e2)).astype(F32)
        before = jnp.dot(onehot.astype(BF16), earlier, preferred_element_type=F32) + cnt
        rank1 = _row_pick(before, erow, e1).astype(I32)
        rank2 = _row_pick(before, erow, e2).astype(I32)
        pos_ref[:, pl.ds(start, t_tile)] = jnp.where(prow == 0, e1, jnp.where(prow == 1, e2, jnp.where(
            prow == 2, rank1, jnp.where(prow == 3, rank2, 0))))
        wts_ref[pl.ds(start, t_tile), :] = jnp.where(wrow == 0, w1, jnp.where(wrow == 1, w2, 0.0)).T
        return cnt + jnp.sum(onehot, axis=1, keepdims=True)

    cnt = lax.fori_loop(0, n // t_tile, phase1, jnp.zeros((N_EXPERTS, 1), F32))

    rows_sq = lax.broadcasted_iota(I32, (LANES, LANES), 0)
    lanes_sq = lax.broadcasted_iota(I32, (LANES, LANES), 1)
    cnt_sq = jnp.broadcast_to(jnp.concatenate([cnt, jnp.zeros((LANES - N_EXPERTS, 1), F32)], axis=0), (LANES, LANES))
    padded = jnp.ceil(cnt_sq * (1.0 / seg_tile)) * seg_tile
    incl = padded
    dist = 1
    while dist < LANES:
        incl = incl + jnp.where(rows_sq >= dist, pltpu.roll(incl, dist, 0), 0.0)
        dist *= 2
    offs = incl - padded
    tab_ref[...] = jnp.where(lanes_sq == 0, offs, jnp.where(lanes_sq == 1, cnt_sq, 0.0)).astype(I32)
    offs_col = offs[:N_EXPERTS, :1].astype(I32)

    def phase2(t, carry):
        cols = pl.ds(pl.multiple_of(t * t_tile, t_tile), t_tile)
        info = pos_ref[:, cols]
        offs_b = jnp.broadcast_to(offs_col, (N_EXPERTS, t_tile))
        p1 = info[2:3] + _row_pick(offs_b, erow, info[0:1])
        p2 = info[3:4] + _row_pick(offs_b, erow, info[1:2])
        pos_ref[:, cols] = jnp.where(prow == 0, p1, jnp.where(prow == 1, p2, 0))
        return carry

    lax.fori_loop(0, n // t_tile, phase2, 0)


def _route(logits):
    n = logits.shape[0]
    return pl.pallas_call(
        functools.partial(_route_kernel, n=n, t_tile=T_ROUTE, seg_tile=TM_EXPERT),
        out_shape=[jax.ShapeDtypeStruct((SUBLANES, n), I32), jax.ShapeDtypeStruct((n, LANES), F32),
                   jax.ShapeDtypeStruct((LANES, LANES), I32)],
        name="route",
    )(logits)


def _token_rows(idx, rpt):
    return pl.ds(pl.multiple_of(idx * rpt, rpt), rpt)


def _dispatch_kernel(seg_ref, cnt_ref, p1_ref, p2_ref, h_ref, hs_ref, zero_scr, sem, *, tm, rpt, n_tiles,
                     seg_tile, cap_tiles):
    def issue(r, carry):
        src = h_ref.at[_token_rows(r, rpt)]
        pltpu.make_async_copy(src, hs_ref.at[_token_rows(p1_ref[r], rpt)], sem).start(priority=0)
        pltpu.make_async_copy(src, hs_ref.at[_token_rows(p2_ref[r], rpt)], sem).start(priority=1)
        return carry

    lax.fori_loop(0, tm, issue, 0, unroll=DMA_ISSUE_UNROLL)
    whole = pltpu.make_async_copy(h_ref, hs_ref.at[pl.ds(0, tm * rpt)], sem)
    whole.wait()
    whole.wait()

    @pl.when(pl.program_id(0) == n_tiles - 1)
    def _():
        zero_scr[...] = jnp.zeros_like(zero_scr)
        zero_row = zero_scr.at[pl.ds(0, rpt)]

        def per_expert(e, carry):
            first_pad = seg_ref[e] + cnt_ref[e]
            n_pad = seg_ref[e + 1] - first_pad

            def fill(r, c):
                pltpu.make_async_copy(zero_row, hs_ref.at[_token_rows(first_pad + r, rpt)], sem).start()
                return c

            def drain(r, c):
                pltpu.make_async_copy(zero_row, hs_ref.at[_token_rows(first_pad + r, rpt)], sem).wait()
                return c

            lax.fori_loop(0, n_pad, fill, 0)
            lax.fori_loop(0, n_pad, drain, 0)
            return carry

        lax.fori_loop(0, N_EXPERTS, per_expert, 0)

        used_tiles = seg_ref[N_EXPERTS] // seg_tile

        def tail_tile(c):
            return hs_ref.at[pl.ds(pl.multiple_of((used_tiles + c) * (seg_tile * rpt), seg_tile * rpt), seg_tile * rpt)]

        def fill_tail(c, carry):
            pltpu.make_async_copy(zero_scr, tail_tile(c), sem).start()
            return carry

        def drain_tail(c, carry):
            pltpu.make_async_copy(zero_scr, tail_tile(c), sem).wait()
            return carry

        lax.fori_loop(0, cap_tiles - used_tiles, fill_tail, 0)
        lax.fori_loop(0, cap_tiles - used_tiles, drain_tail, 0)


def _dispatch(seg_start, counts, pos1, pos2, h2t, rpt):
    n = h2t.shape[0] // rpt
    tm = TM_DISPATCH
    n_tiles = n // tm
    idx = pl.BlockSpec((tm,), lambda i, *_: (i,), memory_space=pltpu.SMEM)
    grid_spec = pltpu.PrefetchScalarGridSpec(
        num_scalar_prefetch=2,
        grid=(n_tiles,),
        in_specs=[idx, idx, pl.BlockSpec((tm * rpt, LANES), lambda i, *_: (i, 0))],
        out_specs=pl.BlockSpec(memory_space=pl.ANY),
        scratch_shapes=[pltpu.VMEM((TM_EXPERT * rpt, LANES), F32), pltpu.SemaphoreType.DMA(())],
    )
    cap_tiles = 2 * n // TM_EXPERT + N_EXPERTS
    return pl.pallas_call(
        functools.partial(_dispatch_kernel, tm=tm, rpt=rpt, n_tiles=n_tiles, seg_tile=TM_EXPERT, cap_tiles=cap_tiles),
        grid_spec=grid_spec,
        out_shape=jax.ShapeDtypeStruct((cap_tiles * TM_EXPERT * rpt, LANES), F32),
        compiler_params=pltpu.CompilerParams(dimension_semantics=("arbitrary",), has_side_effects=True),
        name="dispatch",
    )(seg_start, counts, pos1, pos2, h2t)


def _expert_kernel(t_exp, t_valid, t_new, t_slot, t_next, hs_ref, wg_hbm, wu_hbm, wd_hbm, ys_ref,
                   wg_f32, wu_f32, wd_f32, wg_bf, wu_bf, wd_bf, sems, *, tm, rpt, tiles_per_step):
    def weight_copies(e, slot):
        return [pltpu.make_async_copy(src.at[e], dst.at[slot], sems.at[slot])
                for src, dst in ((wg_hbm, wg_f32), (wu_hbm, wu_f32), (wd_hbm, wd_f32))]

    def one_tile(t, first_token):
        @pl.when(t_valid[t] == 1)
        def _():
            @pl.when(t_new[t] == 1)
            def _():
                slot = t_slot[t]

                @pl.when(t == 0)
                def _():
                    for c in weight_copies(t_exp[t], slot):
                        c.start()

                for c in weight_copies(t_exp[t], slot):
                    c.wait()
                wg_bf[...] = wg_f32[slot].astype(BF16)
                wu_bf[...] = wu_f32[slot].astype(BF16)
                wd_bf[...] = wd_f32[slot].astype(BF16)

                @pl.when(t_next[t] >= 0)
                def _():
                    for c in weight_copies(t_next[t], 1 - slot):
                        c.start()

            h = _from_token_tiles(hs_ref, tm, rpt, first_token).astype(BF16)
            gate = jnp.dot(h, wg_bf[...], preferred_element_type=F32)
            up = jnp.dot(h, wu_bf[...], preferred_element_type=F32)
            hid = ((gate * _sigmoid(gate)) * up).astype(BF16)
            _to_token_tiles(ys_ref, jnp.dot(hid, wd_bf[...], preferred_element_type=F32), first_token)

        @pl.when(t_valid[t] == 0)
        def _():
            ys_ref[pl.ds(first_token * rpt, tm * rpt), :] = jnp.zeros((tm * rpt, LANES), F32)

    for k in range(tiles_per_step):
        one_tile(pl.program_id(0) * tiles_per_step + k, k * tm)


def _tile_table(seg_start, n_tiles, tm):
    total = seg_start[N_EXPERTS] // tm
    t = jnp.arange(n_tiles, dtype=I32)
    blk = jnp.minimum(t, total - 1)
    exp = jnp.sum(seg_start[None, 1:N_EXPERTS + 1] <= (blk * tm)[:, None], axis=1).astype(I32)
    valid = (t < total).astype(I32)
    new = (exp != jnp.concatenate([jnp.full((1,), -1, I32), exp[:-1]])).astype(I32)
    slot = (jnp.cumsum(new) - 1) % 2
    later = jnp.where(exp[None, :] > exp[:, None], exp[None, :], N_EXPERTS)
    nxt = jnp.min(later, axis=1)
    nxt = jnp.where(nxt < N_EXPERTS, nxt, -1).astype(I32)
    return exp, valid, new, slot.astype(I32), nxt


def _experts(hs, seg_start, w_e_gate, w_e_up, w_e_down, rpt):
    d, ff = w_e_gate.shape[-2:]
    tm = TM_EXPERT
    n_tiles = hs.shape[0] // (rpt * tm)
    table = _tile_table(seg_start, n_tiles, tm)
    hbm = pl.BlockSpec(memory_space=pl.ANY)
    per_step = EXPERT_TILES_PER_STEP
    assert n_tiles % per_step == 0
    rows = pl.BlockSpec((per_step * tm * rpt, LANES), lambda s, *_: (s, 0))
    grid_spec = pltpu.PrefetchScalarGridSpec(
        num_scalar_prefetch=len(table),
        grid=(n_tiles // per_step,),
        in_specs=[rows, hbm, hbm, hbm],
        out_specs=rows,
        scratch_shapes=[pltpu.VMEM((2, d, ff), F32), pltpu.VMEM((2, d, ff), F32), pltpu.VMEM((2, ff, d), F32),
                        pltpu.VMEM((d, ff), BF16), pltpu.VMEM((d, ff), BF16), pltpu.VMEM((ff, d), BF16),
                        pltpu.SemaphoreType.DMA((2,))],
    )
    return pl.pallas_call(
        functools.partial(_expert_kernel, tm=tm, rpt=rpt, tiles_per_step=per_step),
        grid_spec=grid_spec,
        out_shape=jax.ShapeDtypeStruct(hs.shape, F32),
        compiler_params=pltpu.CompilerParams(dimension_semantics=("arbitrary",)),
        name="experts",
    )(*table, hs, w_e_gate, w_e_up, w_e_down)


def _combine_kernel(p1_ref, p2_ref, p1n_ref, p2n_ref, wts_ref, x1_ref, p_ref, gp_ref, wpg_ref, wpp_ref, gf_ref,
                    ys_ref, out_ref, ybuf, sems, *, tm, rpt, n_tiles):
    i = pl.program_id(0)
    slot = i % 2

    def gather(a_ref, b_ref, dst):
        def issue(r, carry):
            rows = _token_rows(r, rpt)
            pltpu.make_async_copy(ys_ref.at[_token_rows(a_ref[r], rpt)], ybuf.at[dst, 0, rows],
                                  sems.at[dst]).start(priority=0)
            pltpu.make_async_copy(ys_ref.at[_token_rows(b_ref[r], rpt)], ybuf.at[dst, 1, rows],
                                  sems.at[dst]).start(priority=1)
            return carry

        lax.fori_loop(0, tm, issue, 0, unroll=DMA_ISSUE_UNROLL)

    @pl.when(i == 0)
    def _():
        gather(p1_ref, p2_ref, 0)

    @pl.when(i + 1 < n_tiles)
    def _():
        gather(p1n_ref, p2n_ref, 1 - slot)

    for half in range(2):
        pltpu.make_async_copy(ys_ref.at[pl.ds(0, tm * rpt)], ybuf.at[slot, half], sems.at[slot]).wait()

    wts = wts_ref[...]
    y1 = _from_token_tiles(ybuf.at[slot, 0], tm, rpt)
    y2 = _from_token_tiles(ybuf.at[slot, 1], tm, rpt)
    x2 = x1_ref[...] + (wts[:, 0:1] * y1 + wts[:, 1:2] * y2)
    hp = _rmsnorm(x2, gp_ref[...], NORM_EPS).astype(BF16)
    gate = _sigmoid(jnp.dot(hp, wpg_ref[...], preferred_element_type=F32))
    proj = jnp.dot(p_ref[...].astype(BF16), wpp_ref[...], preferred_element_type=F32)
    x3 = x2 + gate * proj
    out_ref[...] = _rmsnorm(x3, gf_ref[...], NORM_EPS)


def _combine(pos1, pos2, wts, x1, p2d, g_ple, wpg, wpp, g_final, ys, rpt):
    n, d = x1.shape
    pd = p2d.shape[1]
    tm = TM_COMBINE
    n_tiles = n // tm
    idx = pl.BlockSpec((tm,), lambda i: (i,), memory_space=pltpu.SMEM)
    idx_next = pl.BlockSpec((tm,), lambda i: (jnp.minimum(i + 1, n_tiles - 1),), memory_space=pltpu.SMEM)
    row = pl.BlockSpec((tm, d), lambda i: (i, 0))
    full = lambda r, c: pl.BlockSpec((r, c), lambda i: (0, 0))
    return pl.pallas_call(
        functools.partial(_combine_kernel, tm=tm, rpt=rpt, n_tiles=n_tiles),
        grid=(n_tiles,),
        in_specs=[idx, idx, idx_next, idx_next, pl.BlockSpec((tm, LANES), lambda i: (i, 0)), row,
                  pl.BlockSpec((tm, pd), lambda i: (i, 0)), full(1, d), full(d, d), full(pd, d), full(1, d),
                  pl.BlockSpec(memory_space=pl.ANY)],
        out_specs=row,
        out_shape=jax.ShapeDtypeStruct((n, d), F32),
        scratch_shapes=[pltpu.VMEM((2, 2, tm * rpt, LANES), F32), pltpu.SemaphoreType.DMA((2,))],
        compiler_params=pltpu.CompilerParams(dimension_semantics=("arbitrary",)),
        name="combine",
    )(pos1, pos2, pos1, pos2, wts, x1, p2d, g_ple, wpg, wpp, g_final, ys)


def _layer(i, x2, p2d, b, s, g_mix, w_in, lam_q1, lam_k1, lam_q2, lam_k2, g_subln, w_conv, b_conv, w_rg, b_rg,
           w_ig, b_ig, lru_lambda, w_attn_br, w_lru_br, w_out, g_moe, w_rt_group, b_rt_group, w_rt_expert,
           b_rt_expert, w_e_gate, w_e_up, w_e_down, g_ple, w_ple_gate, w_ple_proj):
    n, d = x2.shape
    lam_init = 0.8 - 0.6 * math.exp(-0.3 * i)
    row = lambda a: a.reshape(1, -1)

    q, k, v, xr, gr, ga, gb = _inproj(x2, row(g_mix), w_in.astype(BF16))
    o = _attention(q.reshape(b, s, d), k.reshape(b, s, d), v.reshape(b, s, d),
                   row(lam_q1), row(lam_k1), row(lam_q2), row(lam_k2), row(g_subln), lam_init).reshape(n, d)
    hg = _lru(xr, gr, w_conv, b_conv, w_rg, b_rg, w_ig, b_ig, lru_lambda, b, s)

    pad = LANES - N_GROUPS - N_EXPERTS
    w_router = jnp.concatenate(
        [w_rt_group, w_rt_expert.transpose(1, 0, 2).reshape(d, N_EXPERTS), jnp.zeros((d, pad), F32)], axis=1)
    b_router = jnp.concatenate([b_rt_group, b_rt_expert.reshape(N_EXPERTS), jnp.zeros((pad,), F32)]).reshape(1, LANES)
    x1, h2, logits = _merge(o, hg, ga, gb, x2, w_attn_br.astype(BF16), w_lru_br.astype(BF16), w_out.astype(BF16),
                            row(g_moe), w_router, b_router)

    pos, wts, tab = _route(logits)
    pos1, pos2 = pos[0], pos[1]
    rpt = d // LANES
    seg_start, counts = tab[:N_EXPERTS + 1, 0], tab[:N_EXPERTS, 1]
    hs = _dispatch(seg_start, counts, pos1, pos2, h2, rpt)
    ys = _experts(hs, seg_start, w_e_gate, w_e_up, w_e_down, rpt)
    return pos1, pos2, wts, x1, ys


def kernel(x, p, g_mix, w_in, lam_q1, lam_k1, lam_q2, lam_k2, g_subln, w_conv, b_conv, w_rg, b_rg, w_ig, b_ig, lru_lambda, w_attn_br, w_lru_br, w_out, g_moe, w_rt_group, b_rt_group, w_rt_expert, b_rt_expert, w_e_gate, w_e_up, w_e_down, g_ple, w_ple_gate, w_ple_proj, g_final):
    b, s, d = x.shape
    depth = p.shape[0]
    assert depth == 1, "the final RMSNorm is fused into the last layer's combine step; one layer supported"
    n = b * s
    x2 = x.reshape(n, d)
    i = 0
    p2d = p[i].reshape(n, -1)
    pos1, pos2, wts, x1, ys = _layer(
        i, x2, p2d, b, s, g_mix[i], w_in[i], lam_q1[i], lam_k1[i], lam_q2[i], lam_k2[i], g_subln[i], w_conv[i],
        b_conv[i], w_rg[i], b_rg[i], w_ig[i], b_ig[i], lru_lambda[i], w_attn_br[i], w_lru_br[i], w_out[i],
        g_moe[i], w_rt_group[i], b_rt_group[i], w_rt_expert[i], b_rt_expert[i], w_e_gate[i], w_e_up[i],
        w_e_down[i], g_ple[i], w_ple_gate[i], w_ple_proj[i])
    out = _combine(pos1, pos2, wts, x1, p2d, g_ple[i].reshape(1, d), w_ple_gate[i].astype(BF16),
                   w_ple_proj[i].astype(BF16), g_final.reshape(1, d), ys, d // LANES)
    return out.reshape(b, s, d)
```
